```python
import math
import jax
import jax.numpy as jnp
from jax import lax
import numpy as np

D_MODEL = 2048
BATCH = 1
SEQ = 8192
DEPTH = 4

GRID_W = 64
CTX_LEN = 256
N_EVEN = (DEPTH + 1) // 2
N_ODD = DEPTH // 2
EPS = 1e-6
NEG_INF = -1e30

A_WIDTH = D_MODEL // 2
A_GROUPS = 8
A_GROUP_DIM = A_WIDTH // A_GROUPS
CHUNK = 128
B_WIDTH = D_MODEL // 2
CONV_WIDTH = 31
EVEN_IN = 2 * A_WIDTH + 2 * B_WIDTH
EVEN_MIX = A_WIDTH + B_WIDTH

HEAD_DIM = 128
N_Q_HEADS = 8
N_KV_HEADS = 2
Q_PER_KV = N_Q_HEADS // N_KV_HEADS
C_WIDTH = N_Q_HEADS * HEAD_DIM
KV_WIDTH = N_KV_HEADS * HEAD_DIM
WINDOW = 128
BLOCK = 128
ROPE_BASE = 10000.0
ATTN_SCALE = HEAD_DIM ** -0.5
S_WIDTH = D_MODEL // 2
S_GROUP_CH = 16
S_GROUPS = S_WIDTH // S_GROUP_CH
S_STATE = 64
ODD_IN = C_WIDTH + 2 * KV_WIDTH + S_WIDTH
ODD_MIX = C_WIDTH + S_WIDTH

N_EXPERTS = 16
N_EXPERT_GROUPS = 4
EXPERTS_PER_GROUP = N_EXPERTS // N_EXPERT_GROUPS
TOP_K = 2
D_EXPERT = 1408
MOE_BLOCK = 128

kernel_name = 'hybrid_sgu_conformer_swa_s5_moe_dit'


def rms_norm(x, g):
    xf = x.astype(jnp.float32)
    y = xf * lax.rsqrt(jnp.mean(xf * xf, axis=-1, keepdims=True) + EPS)
    return (y * g.astype(jnp.float32)).astype(x.dtype)


def layer_norm(x, g, b):
    xf = x.astype(jnp.float32)
    xc = xf - jnp.mean(xf, axis=-1, keepdims=True)
    var = jnp.mean(xc * xc, axis=-1, keepdims=True)
    return (xc * lax.rsqrt(var + EPS) * g.astype(jnp.float32) + b.astype(jnp.float32)).astype(x.dtype)


def adaln(cond, w_mod, b_mod):
    return jnp.split(jax.nn.silu(cond) @ w_mod + b_mod, 6, axis=-1)


def modulate(h, shift, scale):
    return h * (1.0 + scale) + shift


def axial_rope_tables(n_tokens):
    rows = n_tokens // GRID_W
    row = jnp.broadcast_to(jnp.arange(rows, dtype=jnp.float32)[:, None], (rows, GRID_W)).reshape(-1)
    col = jnp.broadcast_to(jnp.arange(GRID_W, dtype=jnp.float32)[None, :], (rows, GRID_W)).reshape(-1)
    half = HEAD_DIM // 2
    inv_freq = ROPE_BASE ** (-jnp.arange(0, half, 2, dtype=jnp.float32) / half)
    ang_r = row[:, None] * inv_freq
    ang_c = col[:, None] * inv_freq
    return jnp.cos(ang_r), jnp.sin(ang_r), jnp.cos(ang_c), jnp.sin(ang_c)


def _rotate(x, cos, sin):
    x1, x2 = jnp.split(x, 2, axis=-1)
    cos = cos[None, :, None, :]
    sin = sin[None, :, None, :]
    return jnp.concatenate([x1 * cos - x2 * sin, x2 * cos + x1 * sin], axis=-1)


def axial_rope(x, tables):
    cos_r, sin_r, cos_c, sin_c = tables
    x_row, x_col = jnp.split(x, 2, axis=-1)
    return jnp.concatenate([_rotate(x_row, cos_r, sin_r), _rotate(x_col, cos_c, sin_c)], axis=-1).astype(x.dtype)


def spatial_gating(u, v, ln_g, ln_b, w_s, b_s):
    bsz, n, _ = u.shape
    v = layer_norm(jax.nn.gelu(v), ln_g, ln_b)
    v = v.reshape(bsz, n // CHUNK, CHUNK, A_GROUPS, A_GROUP_DIM)
    mixed = jnp.einsum('gts,bnsgc->bntgc', w_s, v) + b_s.T[None, None, :, :, None]
    return jax.nn.gelu(u) * mixed.reshape(bsz, n, A_WIDTH)


def conformer_conv(a, g, conv_w, conv_b, ln_g, ln_b):
    h = a * jax.nn.sigmoid(g)
    h = lax.conv_general_dilated(h, conv_w[:, None, :], window_strides=(1,),
                                 padding=[(CONV_WIDTH // 2, CONV_WIDTH // 2)],
                                 dimension_numbers=('NWC', 'WIO', 'NWC'),
                                 feature_group_count=B_WIDTH) + conv_b
    return jax.nn.silu(layer_norm(h, ln_g, ln_b))


def even_mixer(h, w_in, w_out, sgu_ln_g, sgu_ln_b, sgu_w, sgu_b, conv_w, conv_b, conv_ln_g, conv_ln_b):
    u, v, a, g = jnp.split(h @ w_in, [A_WIDTH, 2 * A_WIDTH, 2 * A_WIDTH + B_WIDTH], axis=-1)
    y = jnp.concatenate([spatial_gating(u, v, sgu_ln_g, sgu_ln_b, sgu_w, sgu_b),
                         conformer_conv(a, g, conv_w, conv_b, conv_ln_g, conv_ln_b)], axis=-1)
    return y @ w_out


def windowed_attention(q, k, v, k_ctx, v_ctx, sink):
    bsz, n = q.shape[:2]
    n_ctx = k_ctx.shape[1]
    nb = n // BLOCK
    qb = q.reshape(bsz, nb, BLOCK, N_KV_HEADS, Q_PER_KV, HEAD_DIM)

    def band(t):
        tp = jnp.pad(t, ((0, 0), (BLOCK, BLOCK), (0, 0), (0, 0)))
        tp = tp.reshape(bsz, nb + 2, BLOCK, N_KV_HEADS, HEAD_DIM)
        return jnp.concatenate([tp[:, :-2], tp[:, 1:-1], tp[:, 2:]], axis=2)

    kb, vb = band(k), band(v)
    s_loc = jnp.einsum('bnqhgd,bnkhd->bnhgqk', qb, kb, preferred_element_type=jnp.float32) * ATTN_SCALE
    s_ctx = jnp.einsum('bnqhgd,bchd->bnhgqc', qb, k_ctx, preferred_element_type=jnp.float32) * ATTN_SCALE
    qpos = (jnp.arange(nb)[:, None] * BLOCK + jnp.arange(BLOCK)[None, :])[:, :, None]
    kpos = (jnp.arange(nb)[:, None] * BLOCK - BLOCK + jnp.arange(3 * BLOCK)[None, :])[:, None, :]
    valid = (jnp.abs(kpos - qpos) <= WINDOW) & (kpos >= 0) & (kpos < n)
    s_loc = jnp.where(valid[None, :, None, None], s_loc, NEG_INF)
    s_sink = jnp.broadcast_to(sink.astype(jnp.float32).reshape(1, 1, N_KV_HEADS, Q_PER_KV, 1, 1),
                              s_loc.shape[:-1] + (1,))
    probs = jax.nn.softmax(jnp.concatenate([s_loc, s_ctx, s_sink], axis=-1), axis=-1)
    p_loc = probs[..., :3 * BLOCK].astype(v.dtype)
    p_ctx = probs[..., 3 * BLOCK:3 * BLOCK + n_ctx].astype(v.dtype)
    out = (jnp.einsum('bnhgqk,bnkhd->bnqhgd', p_loc, vb)
           + jnp.einsum('bnhgqc,bchd->bnqhgd', p_ctx, v_ctx))
    return out.reshape(bsz, n, C_WIDTH)


def context_attention(q, k, v, sink):
    bsz, n = q.shape[:2]
    qg = q.reshape(bsz, n, N_KV_HEADS, Q_PER_KV, HEAD_DIM)
    s = jnp.einsum('bqhgd,bkhd->bhgqk', qg, k, preferred_element_type=jnp.float32) * ATTN_SCALE
    s_sink = jnp.broadcast_to(sink.astype(jnp.float32).reshape(1, N_KV_HEADS, Q_PER_KV, 1, 1), s.shape[:-1] + (1,))
    p = jax.nn.softmax(jnp.concatenate([s, s_sink], axis=-1), axis=-1)[..., :-1].astype(v.dtype)
    return jnp.einsum('bhgqk,bkhd->bqhgd', p, v).reshape(bsz, n, C_WIDTH)


def s5_discretize(lam_re, lam_im, log_dt, b_re, b_im):
    f32 = jnp.float32
    lam = lax.complex(lam_re.astype(f32), lam_im.astype(f32))
    dt = jnp.exp(log_dt.astype(f32))[:, None]
    lam_bar = jnp.exp(lam * dt)
    b_bar = ((lam_bar - 1.0) / lam)[..., None] * lax.complex(b_re.astype(f32), b_im.astype(f32))
    return lam_bar, b_bar


def diag_scan(lam_bar, bu, reverse):
    def combine(e1, e2):
        a1, b1 = e1
        a2, b2 = e2
        return a1 * a2, a2 * b1 + b2
    a = jnp.broadcast_to(lam_bar, bu.shape)
    return lax.associative_scan(combine, (a, bu), reverse=reverse, axis=1)[1]


def s5_readout(h, c_re, c_im):
    f32 = jnp.float32
    return (jnp.einsum('btgp,ghp->btgh', h.real, c_re.astype(f32))
            - jnp.einsum('btgp,ghp->btgh', h.imag, c_im.astype(f32)))


def bidirectional_s5(u_ctx, u_lat, lam_re, lam_im, log_dt, b_re, b_im, c_re, c_im, d_skip, glu_w, glu_b, need_ctx):
    f32 = jnp.float32
    out_dtype = u_lat.dtype

    def grouped(u):
        return u.reshape(u.shape[0], u.shape[1], S_GROUPS, S_GROUP_CH).astype(f32)

    uc, ul = grouped(u_ctx), grouped(u_lat)
    d_g = d_skip.reshape(S_GROUPS, S_GROUP_CH).astype(f32)
    y_lat = d_g * ul
    y_ctx = d_g * uc if need_ctx else None
    for d, reverse in ((0, False), (1, True)):
        lam_bar, b_bar = s5_discretize(lam_re[d], lam_im[d], log_dt[d], b_re[d], b_im[d])
        h_ctx = diag_scan(lam_bar, jnp.einsum('btgh,gph->btgp', uc, b_bar), reverse)
        h_end = h_ctx[:, 0] if reverse else h_ctx[:, -1]
        entry = -1 if reverse else 0
        bu = jnp.einsum('btgh,gph->btgp', ul, b_bar)
        bu = bu.at[:, entry].add(lam_bar * h_end)
        y_lat = y_lat + s5_readout(diag_scan(lam_bar, bu, reverse), c_re[d], c_im[d])
        if need_ctx:
            y_ctx = y_ctx + s5_readout(h_ctx, c_re[d], c_im[d])

    def glu(y):
        z = jax.nn.gelu(y.reshape(y.shape[0], y.shape[1], S_WIDTH))
        return (z * jax.nn.sigmoid(z @ glu_w.astype(f32) + glu_b.astype(f32))).astype(out_dtype)

    return (glu(y_ctx) if need_ctx else None), glu(y_lat)


def odd_mixer(h_ctx, h_lat, rope_tables, w_in, w_out, sink, lam_re, lam_im, log_dt, b_re, b_im,
              c_re, c_im, d_skip, glu_w, glu_b, need_ctx):
    def project(h):
        bsz, n, _ = h.shape
        q, k, v, s = jnp.split(h @ w_in, [C_WIDTH, C_WIDTH + KV_WIDTH, C_WIDTH + 2 * KV_WIDTH], axis=-1)
        return (q.reshape(bsz, n, N_Q_HEADS, HEAD_DIM), k.reshape(bsz, n, N_KV_HEADS, HEAD_DIM),
                v.reshape(bsz, n, N_KV_HEADS, HEAD_DIM), s)

    q_c, k_c, v_c, s_c = project(h_ctx)
    q_l, k_l, v_l, s_l = project(h_lat)
    att_lat = windowed_attention(axial_rope(q_l, rope_tables), axial_rope(k_l, rope_tables), v_l, k_c, v_c, sink)
    ssm_ctx, ssm_lat = bidirectional_s5(s_c, s_l, lam_re, lam_im, log_dt, b_re, b_im, c_re, c_im,
                                        d_skip, glu_w, glu_b, need_ctx)
    y_lat = jnp.concatenate([att_lat, ssm_lat], axis=-1) @ w_out
    y_ctx = (jnp.concatenate([context_attention(q_c, k_c, v_c, sink), ssm_ctx], axis=-1) @ w_out
             if need_ctx else None)
    return y_ctx, y_lat


def route(h, w_router, b_router):
    aff = jax.nn.sigmoid(h.astype(jnp.float32) @ w_router.astype(jnp.float32))
    grouped = (aff + b_router.astype(jnp.float32)).reshape(-1, N_EXPERT_GROUPS, EXPERTS_PER_GROUP)
    group_score = jnp.sum(lax.top_k(grouped, TOP_K)[0], axis=-1)
    g_sel = jnp.argmax(group_score, axis=-1)
    in_group = jnp.take_along_axis(grouped, g_sel[:, None, None], axis=1)[:, 0]
    _, local = lax.top_k(in_group, TOP_K)
    idx = (g_sel[:, None] * EXPERTS_PER_GROUP + local).astype(jnp.int32)
    w = jnp.take_along_axis(aff, idx, axis=1)
    return idx, w / jnp.sum(w, axis=-1, keepdims=True)


def moe_ffn(h, w_router, b_router, w_gate, w_up, w_down):
    n_tok, d = h.shape
    idx, wts = route(h, w_router, b_router)
    n_assign = n_tok * TOP_K
    flat_e = idx.reshape(-1)
    flat_tok = jnp.arange(n_assign, dtype=jnp.int32) // TOP_K
    order = jnp.argsort(flat_e)
    sorted_e = flat_e[order]
    counts = jnp.bincount(flat_e, length=N_EXPERTS)
    padded = (counts + MOE_BLOCK - 1) // MOE_BLOCK * MOE_BLOCK
    pad_end = jnp.cumsum(padded)
    pad_start = pad_end - padded
    start = jnp.cumsum(counts) - counts
    dest = pad_start[sorted_e] + jnp.arange(n_assign) - start[sorted_e]
    n_blocks = -(-n_assign // MOE_BLOCK) + N_EXPERTS
    n_rows = n_blocks * MOE_BLOCK
    row_tok = jnp.full((n_rows,), n_tok, jnp.int32).at[dest].set(flat_tok[order])
    row_w = jnp.zeros((n_rows,), jnp.float32).at[dest].set(wts.reshape(-1)[order])
    block_e = jnp.minimum(jnp.searchsorted(pad_end, jnp.arange(n_blocks) * MOE_BLOCK, side='right'),
                          N_EXPERTS - 1)
    h_pad = jnp.concatenate([h, jnp.zeros((1, d), h.dtype)], axis=0)
    xb = h_pad[row_tok].reshape(n_blocks, MOE_BLOCK, d)

    def expert_block(args):
        xe, e = args
        return (jax.nn.silu(xe @ w_gate[e]) * (xe @ w_up[e])) @ w_down[e]

    yb = lax.map(expert_block, (xb, block_e)).reshape(n_rows, d)
    out = jnp.zeros_like(h_pad).at[row_tok].add(yb * row_w[:, None].astype(yb.dtype))
    return out[:n_tok]


def setup_inputs(seed: int = 0) -> dict:
    key = jax.random.key(seed)
    ks = iter(jax.random.split(key, 48))
    f32 = jnp.float32

    def nrm(shape, scale):
        return scale * jax.random.normal(next(ks), shape, f32)

    def gain(shape):
        return 1.0 + nrm(shape, 0.02)

    D = D_MODEL
    lam_im = jnp.pi * jnp.arange(S_STATE, dtype=f32) + nrm((N_ODD, 2, S_GROUPS, S_STATE), 0.01)
    return {
        'x': nrm((BATCH, SEQ, D), 1.0),
        'c': nrm((BATCH, D), 1.0),
        'ctx': nrm((BATCH, CTX_LEN, D), 1.0),
        'c_ctx': nrm((D,), 1.0),
        'w_mod': nrm((DEPTH, D, 6 * D), 0.5 * D ** -0.5),
        'b_mod': nrm((DEPTH, 6 * D), 0.01),
        'g_mix': gain((DEPTH, D)),
        'g_ffn': gain((DEPTH, D)),
        'w_in_even': nrm((N_EVEN, D, EVEN_IN), D ** -0.5),
        'w_out_even': nrm((N_EVEN, EVEN_MIX, D), EVEN_MIX ** -0.5),
        'sgu_ln_g': gain((N_EVEN, A_WIDTH)),
        'sgu_ln_b': nrm((N_EVEN, A_WIDTH), 0.02),
        'sgu_w': nrm((N_EVEN, A_GROUPS, CHUNK, CHUNK), CHUNK ** -0.5),
        'sgu_b': 1.0 + nrm((N_EVEN, A_GROUPS, CHUNK), 0.02),
        'conv_w': nrm((N_EVEN, CONV_WIDTH, B_WIDTH), CONV_WIDTH ** -0.5),
        'conv_b': nrm((N_EVEN, B_WIDTH), 0.02),
        'conv_ln_g': gain((N_EVEN, B_WIDTH)),
        'conv_ln_b': nrm((N_EVEN, B_WIDTH), 0.02),
        'w_in_odd': nrm((N_ODD, D, ODD_IN), D ** -0.5),
        'w_out_odd': nrm((N_ODD, ODD_MIX, D), ODD_MIX ** -0.5),
        'attn_sink': nrm((N_ODD, N_Q_HEADS), 0.5),
        'ssm_lam_re': -0.5 + nrm((N_ODD, 2, S_GROUPS, S_STATE), 0.01),
        'ssm_lam_im': lam_im,
        'ssm_log_dt': jax.random.uniform(next(ks), (N_ODD, 2, S_GROUPS), dtype=f32,
                                         minval=math.log(1e-3), maxval=math.log(1e-1)),
        'ssm_b_re': nrm((N_ODD, 2, S_GROUPS, S_STATE, S_GROUP_CH), (2 * S_GROUP_CH) ** -0.5),
        'ssm_b_im': nrm((N_ODD, 2, S_GROUPS, S_STATE, S_GROUP_CH), (2 * S_GROUP_CH) ** -0.5),
        'ssm_c_re': nrm((N_ODD, 2, S_GROUPS, S_GROUP_CH, S_STATE), 0.7),
        'ssm_c_im': nrm((N_ODD, 2, S_GROUPS, S_GROUP_CH, S_STATE), 0.7),
        'ssm_d': nrm((N_ODD, S_WIDTH), 1.0),
        'glu_w': nrm((N_ODD, S_WIDTH, S_WIDTH), S_WIDTH ** -0.5),
        'glu_b': nrm((N_ODD, S_WIDTH), 0.02),
        'w_router': nrm((D, N_EXPERTS), D ** -0.5),
        'b_router': nrm((N_EXPERTS,), 0.01),
        'w_gate': nrm((DEPTH, N_EXPERTS, D, D_EXPERT), D ** -0.5),
        'w_up': nrm((DEPTH, N_EXPERTS, D, D_EXPERT), D ** -0.5),
        'w_down': nrm((DEPTH, N_EXPERTS, D_EXPERT, D), D_EXPERT ** -0.5),
        'g_final': gain((D,)),
    }


def reference(x, c, ctx, c_ctx, w_mod, b_mod, g_mix, g_ffn, w_in_even, w_out_even, sgu_ln_g, sgu_ln_b,
              sgu_w, sgu_b, conv_w, conv_b, conv_ln_g, conv_ln_b, w_in_odd, w_out_odd, attn_sink,
              ssm_lam_re, ssm_lam_im, ssm_log_dt, ssm_b_re, ssm_b_im, ssm_c_re, ssm_c_im, ssm_d,
              glu_w, glu_b, w_router, b_router, w_gate, w_up, w_down, g_final):
    bsz, n_lat, _ = x.shape
    n_ctx = ctx.shape[1]
    rope_tables = axial_rope_tables(n_lat)
    x_lat, x_ctx = x, ctx
    for l in range(DEPTH):
        need_ctx = l < DEPTH - 1
        odd = l % 2 == 1
        j = l // 2
        sh1, sc1, gt1, sh2, sc2, gt2 = adaln(c[:, None, :], w_mod[l], b_mod[l])
        csh1, csc1, cgt1, csh2, csc2, cgt2 = adaln(c_ctx, w_mod[l], b_mod[l])
        h_lat = modulate(rms_norm(x_lat, g_mix[l]), sh1, sc1)
        if need_ctx or odd:
            h_ctx = modulate(rms_norm(x_ctx, g_mix[l]), csh1, csc1)
        if odd:
            y_ctx, y_lat = odd_mixer(h_ctx, h_lat, rope_tables, w_in_odd[j], w_out_odd[j], attn_sink[j],
                                     ssm_lam_re[j], ssm_lam_im[j], ssm_log_dt[j], ssm_b_re[j], ssm_b_im[j],
                                     ssm_c_re[j], ssm_c_im[j], ssm_d[j], glu_w[j], glu_b[j], need_ctx)
        else:
            even_args = (w_in_even[j], w_out_even[j], sgu_ln_g[j], sgu_ln_b[j], sgu_w[j], sgu_b[j],
                         conv_w[j], conv_b[j], conv_ln_g[j], conv_ln_b[j])
            y_lat = even_mixer(h_lat, *even_args)
            y_ctx = even_mixer(h_ctx, *even_args) if need_ctx else None
        x_lat = x_lat + gt1 * y_lat
        if need_ctx:
            x_ctx = x_ctx + cgt1 * y_ctx
        f_lat_in = modulate(rms_norm(x_lat, g_ffn[l]), sh2, sc2).reshape(-1, D_MODEL)
        if need_ctx:
            f_ctx_in = modulate(rms_norm(x_ctx, g_ffn[l]), csh2, csc2).reshape(-1, D_MODEL)
            f = moe_ffn(jnp.concatenate([f_ctx_in, f_lat_in], axis=0), w_router, b_router,
                        w_gate[l], w_up[l], w_down[l])
            x_ctx = x_ctx + cgt2 * f[:bsz * n_ctx].reshape(x_ctx.shape)
            f_lat = f[bsz * n_ctx:]
        else:
            f_lat = moe_ffn(f_lat_in, w_router, b_router, w_gate[l], w_up[l], w_down[l])
        x_lat = x_lat + gt2 * f_lat.reshape(x_lat.shape)
    return rms_norm(x_lat, g_final)
```

```python
import functools
import math

import jax
import jax.numpy as jnp
from jax import lax
from jax.experimental import pallas as pl
from jax.experimental.pallas import tpu as pltpu

F32 = jnp.float32
BF16 = jnp.bfloat16

D_MODEL = 2048
DEPTH = 4
GRID_W = 64
EPS = 1e-6
NEG_INF = -1e30

A_WIDTH = 1024
A_GROUPS = 8
CHUNK = 128
B_WIDTH = 1024
CONV_WIDTH = 31
CONV_HALO = 16

HEAD_DIM = 128
N_Q_HEADS = 8
N_KV_HEADS = 2
Q_PER_KV = 4
C_WIDTH = 1024
KV_WIDTH = 256
WINDOW = 128
ATTN_BLOCK = 128
ROPE_BASE = 10000.0
ATTN_SCALE = HEAD_DIM ** -0.5
S_WIDTH = 1024
S_GROUP_CH = 16
S_GROUPS = 64
S_STATE = 64
S_CHUNK = 32
S_GROUP_BATCH = 8
ODD_IN = C_WIDTH + 2 * KV_WIDTH + S_WIDTH

N_EXPERTS = 16
N_EXPERT_GROUPS = 4
EXPERTS_PER_GROUP = 4
TOP_K = 2
D_EXPERT = 1408
MOE_BM = 256

TM = 256
VMEM_LIMIT = 56 * 1024 * 1024


def _cparams(n_axes=1, vmem=VMEM_LIMIT):
    return pltpu.CompilerParams(dimension_semantics=("arbitrary",) * n_axes, vmem_limit_bytes=vmem)


def _dot(a, b):
    return jnp.dot(a, b, preferred_element_type=F32)


def _dot_nt(a, b):
    return lax.dot_general(a, b, (((1,), (1,)), ((), ())), preferred_element_type=F32)


ADALN_TN = 1024


def _adaln_kernel(cond_ref, w_ref, b_ref, o_ref):
    c = cond_ref[...]
    s = (c * jax.nn.sigmoid(c)).astype(BF16)
    o_ref[0] = _dot(s, w_ref[0].astype(BF16)) + b_ref[0]


def adaln_all(cond8, w_mod, b_mod):
    depth, d, n6 = w_mod.shape
    return pl.pallas_call(
        _adaln_kernel,
        grid=(depth, n6 // ADALN_TN),
        in_specs=[
            pl.BlockSpec((8, d), lambda l, j: (0, 0)),
            pl.BlockSpec((1, d, ADALN_TN), lambda l, j: (l, 0, j)),
            pl.BlockSpec((1, 1, ADALN_TN), lambda l, j: (l, 0, j)),
        ],
        out_specs=pl.BlockSpec((1, 8, ADALN_TN), lambda l, j: (l, 0, j)),
        out_shape=jax.ShapeDtypeStruct((depth, 8, n6), F32),
        compiler_params=_cparams(2),
        name="adaln",
    )(cond8, w_mod, b_mod.reshape(depth, 1, n6))


def _mod_spec(n_ctx_blocks):
    return pl.BlockSpec((1, 6, D_MODEL), lambda i: (jnp.where(i < n_ctx_blocks, 1, 0), 0, 0))


def _rms_mod(x, g, shift, scale):
    ms = jnp.mean(x * x, axis=-1, keepdims=True)
    y = x * lax.rsqrt(ms + EPS) * g
    return y * (1.0 + scale) + shift


INPROJ_NC = 512


def _inproj_kernel(*refs, combine):
    if combine:
        x_ref, o0_ref, o1_ref, modp_ref, g_ref, mod_ref, w_ref, xo_ref, p_ref = refs
        x = x_ref[...] + modp_ref[0][5:6] * (o0_ref[...] + o1_ref[...])
        xo_ref[...] = x
    else:
        x_ref, g_ref, mod_ref, w_ref, p_ref = refs
        x = x_ref[...]
    m = mod_ref[0]
    h = _rms_mod(x, g_ref[...], m[0:1], m[1:2]).astype(BF16)
    n = w_ref.shape[1]
    for j in range(0, n, INPROJ_NC):
        p_ref[:, j:j + INPROJ_NC] = _dot(h, w_ref[:, j:j + INPROJ_NC])


def inproj(x, g, mods_l, w_bf16, n_ctx_blocks, moe_out=None, mods_prev=None):
    nt, d = x.shape
    n = w_bf16.shape[1]
    nblk = nt // TM
    row = pl.BlockSpec((TM, d), lambda i: (i, 0))
    g_spec = pl.BlockSpec((1, d), lambda i: (0, 0))
    w_spec = pl.BlockSpec((d, n), lambda i: (0, 0), pipeline_mode=pl.Buffered(1))
    p_spec = pl.BlockSpec((TM, n), lambda i: (i, 0))
    p_shape = jax.ShapeDtypeStruct((nt, n), F32)
    if moe_out is None:
        return None, pl.pallas_call(
            functools.partial(_inproj_kernel, combine=False),
            grid=(nblk,),
            in_specs=[row, g_spec, _mod_spec(n_ctx_blocks), w_spec],
            out_specs=p_spec,
            out_shape=p_shape,
            compiler_params=_cparams(),
            name="inproj",
        )(x, g.reshape(1, d), mods_l, w_bf16)
    o1 = pl.BlockSpec((TM, d), lambda i: (i + nblk, 0))
    return pl.pallas_call(
        functools.partial(_inproj_kernel, combine=True),
        grid=(nblk,),
        in_specs=[row, row, o1, _mod_spec(n_ctx_blocks), g_spec, _mod_spec(n_ctx_blocks), w_spec],
        out_specs=[row, p_spec],
        out_shape=[jax.ShapeDtypeStruct((nt, d), F32), p_shape],
        compiler_params=_cparams(),
        name="combine_inproj",
    )(x, moe_out, moe_out, mods_prev, g.reshape(1, d), mods_l, w_bf16)


CONV_RC = 64


def _layer_norm(x, g, b):
    mu = jnp.mean(x, axis=-1, keepdims=True)
    xc = x - mu
    var = jnp.mean(xc * xc, axis=-1, keepdims=True)
    return xc * lax.rsqrt(var + EPS) * g + b


def _even_kernel(u_ref, v_ref, a_ref, g_ref, ap_ref, gp_ref, an_ref, gn_ref,
                 lng_ref, lnb_ref, ws_ref, bs_ref, cw_ref, cb_ref, clg_ref, clb_ref,
                 ya_ref, yb_ref, hpad_ref, cacc_ref, *, n_ctx_blocks, n_blocks):
    i = pl.program_id(0)
    for c in range(TM // CHUNK):
        rows = slice(c * CHUNK, (c + 1) * CHUNK)
        vn = _layer_norm(jax.nn.gelu(v_ref[rows, :]), lng_ref[...], lnb_ref[...]).astype(BF16)
        for grp in range(A_GROUPS):
            cols = slice(grp * CHUNK, (grp + 1) * CHUNK)
            mixed = _dot(ws_ref[grp], vn[:, cols]) + bs_ref[grp]
            ya_ref[rows, cols] = (jax.nn.gelu(u_ref[rows, cols]) * mixed).astype(BF16)
    first = jnp.logical_or(i == 0, i == n_ctx_blocks)
    last = jnp.logical_or(i == n_ctx_blocks - 1, i == n_blocks - 1)
    hpad_ref[0:CONV_HALO, :] = jnp.where(first, 0.0, ap_ref[...] * jax.nn.sigmoid(gp_ref[...]))
    hpad_ref[CONV_HALO:CONV_HALO + TM, :] = a_ref[...] * jax.nn.sigmoid(g_ref[...])
    hpad_ref[CONV_HALO + TM:, :] = jnp.where(last, 0.0, an_ref[...] * jax.nn.sigmoid(gn_ref[...]))
    off = CONV_HALO - CONV_WIDTH // 2
    for cc in range(B_WIDTH // 128):
        cols = slice(cc * 128, (cc + 1) * 128)
        for rc in range(TM // CONV_RC):
            acc = jnp.zeros((CONV_RC, 128), F32)
            for k in range(CONV_WIDTH):
                r0 = rc * CONV_RC + k + off
                acc = acc + cw_ref[k:k + 1, cols] * hpad_ref[r0:r0 + CONV_RC, cols]
            cacc_ref[rc * CONV_RC:(rc + 1) * CONV_RC, cols] = acc
    hc = _layer_norm(cacc_ref[...] + cb_ref[...], clg_ref[...], clb_ref[...])
    yb_ref[...] = (hc * jax.nn.sigmoid(hc)).astype(BF16)


def even_mixer(p, ln_g, ln_b, ws_bf16, bs_full, conv_w, conv_b, cln_g, cln_b, n_ctx_blocks):
    nt = p.shape[0]
    nblk = nt // TM
    hb = TM // CONV_HALO
    last_h = nt // CONV_HALO - 1
    col = lambda j: pl.BlockSpec((TM, 1024), lambda i: (i, j))
    prev = lambda j: pl.BlockSpec((CONV_HALO, 1024), lambda i: (jnp.maximum(i * hb - 1, 0), j))
    nxt = lambda j: pl.BlockSpec((CONV_HALO, 1024), lambda i: (jnp.minimum((i + 1) * hb, last_h), j))
    vec = pl.BlockSpec((1, 1024), lambda i: (0, 0))
    out = pl.BlockSpec((TM, 1024), lambda i: (i, 0))
    return pl.pallas_call(
        functools.partial(_even_kernel, n_ctx_blocks=n_ctx_blocks, n_blocks=nblk),
        grid=(nblk,),
        in_specs=[col(0), col(1), col(2), col(3), prev(2), prev(3), nxt(2), nxt(3),
                  vec, vec,
                  pl.BlockSpec((A_GROUPS, CHUNK, CHUNK), lambda i: (0, 0, 0)),
                  pl.BlockSpec((A_GROUPS, CHUNK, CHUNK), lambda i: (0, 0, 0)),
                  pl.BlockSpec((CONV_WIDTH, 1024), lambda i: (0, 0)),
                  vec, vec, vec],
        out_specs=[out, out],
        out_shape=[jax.ShapeDtypeStruct((nt, 1024), BF16)] * 2,
        scratch_shapes=[pltpu.VMEM((TM + 2 * CONV_HALO, 1024), F32), pltpu.VMEM((TM, 1024), F32)],
        compiler_params=_cparams(),
        name="even_mixer",
    )(p, p, p, p, p, p, p, p, ln_g.reshape(1, -1), ln_b.reshape(1, -1), ws_bf16, bs_full,
      conv_w, conv_b.reshape(1, -1), cln_g.reshape(1, -1), cln_b.reshape(1, -1))


def _outproj_kernel(ya_ref, yb_ref, wa_ref, wb_ref, x_ref, mod_ref, g_ref, wr_ref, xo_ref, f_ref, lg_ref):
    m = mod_ref[0]
    y = _dot(ya_ref[...], wa_ref[...]) + _dot(yb_ref[...], wb_ref[...])
    x = x_ref[...] + m[2:3] * y
    xo_ref[...] = x
    f = _rms_mod(x, g_ref[...], m[3:4], m[4:5])
    f_ref[...] = f
    f_hi = f.astype(BF16)
    f_lo = (f - f_hi.astype(F32)).astype(BF16)
    wr = wr_ref[...]
    w_hi = wr.astype(BF16)
    w_lo = (wr - w_hi.astype(F32)).astype(BF16)
    lg_ref[...] = _dot_nt(w_hi, f_hi) + _dot_nt(w_lo, f_hi) + _dot_nt(w_hi, f_lo)


def outproj(ya, yb, wa, wb, x, mods_l, g_ffn, w_router_t, n_ctx_blocks):
    nt, d = x.shape
    nblk = nt // TM
    half = pl.BlockSpec((TM, 1024), lambda i: (i, 0))
    wsp = pl.BlockSpec((1024, d), lambda i: (0, 0), pipeline_mode=pl.Buffered(1))
    row = pl.BlockSpec((TM, d), lambda i: (i, 0))
    return pl.pallas_call(
        _outproj_kernel,
        grid=(nblk,),
        in_specs=[half, half, wsp, wsp, row, _mod_spec(n_ctx_blocks),
                  pl.BlockSpec((1, d), lambda i: (0, 0)),
                  pl.BlockSpec((N_EXPERTS, d), lambda i: (0, 0))],
        out_specs=[row, row, pl.BlockSpec((N_EXPERTS, TM), lambda i: (0, i))],
        out_shape=[jax.ShapeDtypeStruct((nt, d), F32), jax.ShapeDtypeStruct((nt, d), F32),
                   jax.ShapeDtypeStruct((N_EXPERTS, nt), F32)],
        compiler_params=_cparams(),
        name="outproj",
    )(ya, yb, wa, wb, x, mods_l, g_ffn.reshape(1, d), w_router_t)


N_LOC = 3 * ATTN_BLOCK


def _rope(x, cos, sin):
    lane = lax.broadcasted_iota(jnp.int32, x.shape, 1)
    swapped = jnp.where(lane % 64 < 32, pltpu.roll(x, 96, axis=1), pltpu.roll(x, 32, axis=1))
    return x * cos + swapped * sin


def _attn_kernel(sink_ref, q_ref, kp_ref, kc_ref, kn_ref, vp_ref, vc_ref, vn_ref, kx_ref, vx_ref,
                 cosp_ref, cosc_ref, cosn_ref, sinp_ref, sinc_ref, sinn_ref, o_ref,
                 qs_ref, kbuf_ref, vbuf_ref, *, n_ctx_blocks, n_blocks, n_ctx):
    i = pl.program_id(0)
    nkeys = N_LOC + n_ctx
    nq = Q_PER_KV * ATTN_BLOCK
    is_lat = jnp.where(i >= n_ctx_blocks, 1, 0)
    prev_ok = jnp.where(i - 1 >= n_ctx_blocks, is_lat, 0)
    next_ok = jnp.where(i + 1 <= n_blocks - 1, is_lat, 0)
    qi = lax.broadcasted_iota(jnp.int32, (nq, nkeys), 0) & (ATTN_BLOCK - 1)
    kj = lax.broadcasted_iota(jnp.int32, (nq, nkeys), 1)
    rel = kj - ATTN_BLOCK - qi
    blk_ok = jnp.where(kj < ATTN_BLOCK, prev_ok, jnp.where(kj < 2 * ATTN_BLOCK, is_lat, next_ok))
    rel = jnp.where(blk_ok > 0, rel, WINDOW + 1)
    valid = jnp.logical_or(kj >= N_LOC, jnp.logical_and(rel >= -WINDOW, rel <= WINDOW))
    rowh = lax.broadcasted_iota(jnp.int32, (nq, 1), 0) // ATTN_BLOCK
    for h in range(N_KV_HEADS):
        hc = slice(h * HEAD_DIM, (h + 1) * HEAD_DIM)
        kbuf_ref[0:ATTN_BLOCK, :] = _rope(kp_ref[:, hc], cosp_ref[...], sinp_ref[...]).astype(BF16)
        kbuf_ref[ATTN_BLOCK:2 * ATTN_BLOCK, :] = _rope(kc_ref[:, hc], cosc_ref[...], sinc_ref[...]).astype(BF16)
        kbuf_ref[2 * ATTN_BLOCK:N_LOC, :] = _rope(kn_ref[:, hc], cosn_ref[...], sinn_ref[...]).astype(BF16)
        kbuf_ref[N_LOC:, :] = kx_ref[:, hc].astype(BF16)
        vbuf_ref[0:ATTN_BLOCK, :] = vp_ref[:, hc].astype(BF16)
        vbuf_ref[ATTN_BLOCK:2 * ATTN_BLOCK, :] = vc_ref[:, hc].astype(BF16)
        vbuf_ref[2 * ATTN_BLOCK:N_LOC, :] = vn_ref[:, hc].astype(BF16)
        vbuf_ref[N_LOC:, :] = vx_ref[:, hc].astype(BF16)
        sink = jnp.zeros((Q_PER_KV * ATTN_BLOCK, 1), F32)
        for gq in range(Q_PER_KV):
            head = h * Q_PER_KV + gq
            qc = slice(head * HEAD_DIM, (head + 1) * HEAD_DIM)
            qs_ref[gq * ATTN_BLOCK:(gq + 1) * ATTN_BLOCK, :] = _rope(
                q_ref[:, qc], cosc_ref[...], sinc_ref[...]).astype(BF16)
            sink = jnp.where(rowh == gq, sink_ref[head], sink)
        s = _dot_nt(qs_ref[...], kbuf_ref[...]) * ATTN_SCALE
        s = jnp.where(valid, s, NEG_INF)
        mx = jnp.maximum(jnp.max(s, axis=-1, keepdims=True), sink)
        p = jnp.exp(s - mx)
        den = jnp.sum(p, axis=-1, keepdims=True) + jnp.exp(sink - mx)
        o = _dot(p.astype(BF16), vbuf_ref[...]) / den
        for gq in range(Q_PER_KV):
            head = h * Q_PER_KV + gq
            o_ref[:, head * HEAD_DIM:(head + 1) * HEAD_DIM] = o[gq * ATTN_BLOCK:(gq + 1) * ATTN_BLOCK].astype(BF16)


def attention(p, sink, cos_t, sin_t, n_ctx):
    nt = p.shape[0]
    nblk = nt // ATTN_BLOCK
    ncb = n_ctx // ATTN_BLOCK
    kcol, vcol = 2048 // KV_WIDTH, 2048 // KV_WIDTH + 1
    pm = lambda i: jnp.maximum(i - 1, 0)
    nx = lambda i: jnp.minimum(i + 1, nblk - 1)
    kv = lambda f, c: pl.BlockSpec((ATTN_BLOCK, KV_WIDTH), lambda i, s: (f(i), c))
    tab = lambda f: pl.BlockSpec((ATTN_BLOCK, HEAD_DIM), lambda i, s: (f(i), 0))
    same = lambda i: i
    grid_spec = pltpu.PrefetchScalarGridSpec(
        num_scalar_prefetch=1,
        grid=(nblk,),
        in_specs=[pl.BlockSpec((ATTN_BLOCK, C_WIDTH), lambda i, s: (i, 0)),
                  kv(pm, kcol), kv(same, kcol), kv(nx, kcol), kv(pm, vcol), kv(same, vcol), kv(nx, vcol),
                  pl.BlockSpec((n_ctx, KV_WIDTH), lambda i, s: (0, kcol)),
                  pl.BlockSpec((n_ctx, KV_WIDTH), lambda i, s: (0, vcol)),
                  tab(pm), tab(same), tab(nx), tab(pm), tab(same), tab(nx)],
        out_specs=pl.BlockSpec((ATTN_BLOCK, C_WIDTH), lambda i, s: (i, 0)),
        scratch_shapes=[pltpu.VMEM((Q_PER_KV * ATTN_BLOCK, HEAD_DIM), BF16),
                        pltpu.VMEM((N_LOC + n_ctx, HEAD_DIM), BF16),
                        pltpu.VMEM((N_LOC + n_ctx, HEAD_DIM), BF16)],
    )
    return pl.pallas_call(
        functools.partial(_attn_kernel, n_ctx_blocks=ncb, n_blocks=nblk, n_ctx=n_ctx),
        grid_spec=grid_spec,
        out_shape=jax.ShapeDtypeStruct((nt, C_WIDTH), BF16),
        compiler_params=_cparams(),
        name="attention",
    )(sink, p, p, p, p, p, p, p, p, p, cos_t, cos_t, cos_t, sin_t, sin_t, sin_t)


def rope_tables(n_ctx, n_lat):
    rows = n_lat // GRID_W
    row = jnp.broadcast_to(jnp.arange(rows, dtype=F32)[:, None], (rows, GRID_W)).reshape(-1)
    col = jnp.broadcast_to(jnp.arange(GRID_W, dtype=F32)[None, :], (rows, GRID_W)).reshape(-1)
    half = HEAD_DIM // 2
    inv_freq = ROPE_BASE ** (-jnp.arange(0, half, 2, dtype=F32) / half)
    ang_r = row[:, None] * inv_freq
    ang_c = col[:, None] * inv_freq
    cos = jnp.concatenate([jnp.cos(ang_r), jnp.cos(ang_r), jnp.cos(ang_c), jnp.cos(ang_c)], axis=-1)
    sin = jnp.concatenate([-jnp.sin(ang_r), jnp.sin(ang_r), -jnp.sin(ang_c), jnp.sin(ang_c)], axis=-1)
    cos = jnp.concatenate([jnp.ones((n_ctx, HEAD_DIM), F32), cos], axis=0)
    sin = jnp.concatenate([jnp.zeros((n_ctx, HEAD_DIM), F32), sin], axis=0)
    return cos, sin


S_LH = S_CHUNK * S_GROUP_CH
S_HALF = S_GROUP_BATCH * 128


def _s5_kernel(u_ref, t_ref, ws_ref, wo_ref, lam_ref, y_ref, s_ref, hf_ref, hr_ref, *, n_chunks, n_ctx_chunks):
    for g in range(S_GROUP_BATCH):
        s = _dot(u_ref[g].astype(BF16), ws_ref[g])
        s_ref[:, g * 128:(g + 1) * 128] = s[:, 0:128]
        s_ref[:, S_HALF + g * 128:S_HALF + (g + 1) * 128] = s[:, 128:256]
    hf_ref[...] = jnp.zeros(hf_ref.shape, F32)
    hr_ref[...] = jnp.zeros(hr_ref.shape, F32)
    lam = lam_ref[0]
    lam_re, lam_im = lam[:, :S_HALF], lam[:, S_HALF:]
    is_fwd = lax.broadcasted_iota(jnp.int32, (1, 2 * S_HALF), 1) % 128 < S_STATE

    def step(k, state):
        st_re, st_im = state
        cf = k
        cr = jnp.where(k < n_ctx_chunks, n_ctx_chunks - 1 - k, n_chunks - 1 - (k - n_ctx_chunks))
        st = jnp.concatenate([st_re, st_im], axis=1)
        hf_ref[pl.ds(cf, 1), :] = st
        hr_ref[pl.ds(cr, 1), :] = st
        s_in = jnp.where(is_fwd, s_ref[pl.ds(cf, 1), :], s_ref[pl.ds(cr, 1), :])
        new_re = lam_re * st_re - lam_im * st_im + s_in[:, :S_HALF]
        new_im = lam_re * st_im + lam_im * st_re + s_in[:, S_HALF:]
        return new_re, new_im

    zero = jnp.zeros((1, S_HALF), F32)
    lax.fori_loop(0, n_chunks, step, (zero, zero))
    hin = jnp.where(is_fwd, hf_ref[...], hr_ref[...]).astype(BF16)
    for g in range(S_GROUP_BATCH):
        hin_g = jnp.concatenate([hin[:, g * 128:(g + 1) * 128],
                                 hin[:, S_HALF + g * 128:S_HALF + (g + 1) * 128]], axis=1)
        y_ref[g] = _dot(u_ref[g].astype(BF16), t_ref[g]) + _dot(hin_g, wo_ref[g])


def s5_scan(u_g, t_m, w_s, w_o, lam_rows, n_chunks, n_ctx_chunks):
    g, ncp, lh = u_g.shape
    nb = g // S_GROUP_BATCH
    blk = lambda a, b: pl.BlockSpec((S_GROUP_BATCH, a, b), lambda i: (i, 0, 0))
    return pl.pallas_call(
        functools.partial(_s5_kernel, n_chunks=n_chunks, n_ctx_chunks=n_ctx_chunks),
        grid=(nb,),
        in_specs=[blk(ncp, lh), blk(lh, lh), blk(lh, 256), blk(256, lh),
                  pl.BlockSpec((1, 1, 2 * S_HALF), lambda i: (i, 0, 0))],
        out_specs=blk(ncp, lh),
        out_shape=jax.ShapeDtypeStruct((g, ncp, lh), F32),
        scratch_shapes=[pltpu.VMEM((ncp, 2 * S_HALF), F32)] * 3,
        compiler_params=_cparams(),
        name="s5_scan",
    )(u_g, t_m, w_s, w_o, lam_rows)


def s5_weights(lam_re, lam_im, log_dt, b_re, b_im, c_re, c_im):
    hp = lax.Precision.HIGHEST
    L = S_CHUNK
    lam = lax.complex(lam_re.astype(F32), lam_im.astype(F32))
    dt = jnp.exp(log_dt.astype(F32))[..., None]
    lam_bar = jnp.exp(lam * dt)
    b_bar = ((lam_bar - 1.0) / lam)[..., None] * lax.complex(b_re.astype(F32), b_im.astype(F32))
    c = lax.complex(c_re.astype(F32), c_im.astype(F32))
    taus = jnp.arange(L + 1, dtype=F32)
    pw = jnp.exp((lam * dt)[None] * taus[:, None, None, None])

    def real_einsum(spec, a, b):
        return (jnp.einsum(spec, a.real, b.real, precision=hp) - jnp.einsum(spec, a.imag, b.imag, precision=hp))

    cb = pw[:L, :, :, None, :] * c[None]
    kern = real_einsum('tdgop,dgpi->tdgoi', cb, b_bar)
    kf, kr = kern[:, 0], kern[:, 1]
    kfull = jnp.concatenate([kr[L - 1:0:-1], (kf[0] + kr[0])[None], kf[1:]], axis=0)
    idx = jnp.arange(L)[None, :] - jnp.arange(L)[:, None] + (L - 1)
    t_m = kfull[idx]
    t_m = jnp.transpose(t_m, (2, 0, 4, 1, 3)).reshape(S_GROUPS, S_LH, S_LH)
    pf = pw[L - 1::-1][:L, 0]
    pr = pw[:L, 1]
    wsf = pf[:, :, :, None] * b_bar[0][None]
    wsr = pr[:, :, :, None] * b_bar[1][None]
    w_s = jnp.stack([wsf.real, wsr.real, wsf.imag, wsr.imag], axis=0)
    w_s = jnp.transpose(w_s, (2, 1, 4, 0, 3)).reshape(S_GROUPS, S_LH, 4 * S_STATE)
    of = pw[1:L + 1, 0][:, :, None, :] * c[0][None]
    orv = pw[L:0:-1, 1][:, :, None, :] * c[1][None]
    w_o = jnp.stack([of.real, orv.real, -of.imag, -orv.imag], axis=0)
    w_o = jnp.transpose(w_o, (2, 0, 4, 1, 3)).reshape(S_GROUPS, 4 * S_STATE, S_LH)
    lam_l = pw[L]
    nb = S_GROUPS // S_GROUP_BATCH
    lre = jnp.concatenate([lam_l[0].real, lam_l[1].real], axis=-1).reshape(nb, 1, S_HALF)
    lim = jnp.concatenate([lam_l[0].imag, lam_l[1].imag], axis=-1).reshape(nb, 1, S_HALF)
    return t_m.astype(BF16), w_s.astype(BF16), w_o.astype(BF16), jnp.concatenate([lre, lim], axis=-1)


def _glu_kernel(y_ref, u_ref, d_ref, w_ref, b_ref, o_ref):
    z = jax.nn.gelu(y_ref[...] + d_ref[...] * u_ref[...])
    gate = jax.nn.sigmoid(_dot(z.astype(BF16), w_ref[...]) + b_ref[...])
    o_ref[...] = (z * gate).astype(BF16)


def s5_glu(y_ssm, p, d_skip, glu_w_bf16, glu_b):
    nt = y_ssm.shape[0]
    vec = pl.BlockSpec((1, S_WIDTH), lambda i: (0, 0))
    return pl.pallas_call(
        _glu_kernel,
        grid=(nt // TM,),
        in_specs=[pl.BlockSpec((TM, S_WIDTH), lambda i: (i, 0)),
                  pl.BlockSpec((TM, S_WIDTH), lambda i: (i, 1)),
                  vec, pl.BlockSpec((S_WIDTH, S_WIDTH), lambda i: (0, 0)), vec],
        out_specs=pl.BlockSpec((TM, S_WIDTH), lambda i: (i, 0)),
        out_shape=jax.ShapeDtypeStruct((nt, S_WIDTH), BF16),
        compiler_params=_cparams(),
        name="s5_glu",
    )(y_ssm, p, d_skip.reshape(1, -1), glu_w_bf16, glu_b.reshape(1, -1))


MOE_FCHUNKS = ((0, 512), (512, 512), (1024, 384))


def _moe_kernel(be_ref, tok_ref, dst_ref, nv_ref, nu_ref,
                f_hbm, rw_ref, wg_ref, wu_ref, wd_ref, out_hbm,
                xbuf, ybuf, hbuf, gsem, ssem):
    b = pl.program_id(0)
    nb = pl.num_programs(0)
    n_used = nu_ref[0]

    def gather_copy(blk, r, slot):
        tok = tok_ref[blk * MOE_BM + r]
        return pltpu.make_async_copy(f_hbm.at[pl.ds(tok, 1), :], xbuf.at[slot, pl.ds(r, 1), :], gsem.at[slot])

    def scatter_copy(blk, r, slot):
        dst = dst_ref[blk * MOE_BM + r]
        return pltpu.make_async_copy(ybuf.at[slot, pl.ds(r, 1), :], out_hbm.at[pl.ds(dst, 1), :], ssem.at[slot])

    def start_gather(blk, slot):
        def body(r, carry):
            gather_copy(blk, r, slot).start()
            return carry
        lax.fori_loop(0, MOE_BM, body, 0)

    def wait_gather(blk, slot):
        def body(r, carry):
            gather_copy(blk, r, slot).wait()
            return carry
        lax.fori_loop(0, MOE_BM, body, 0)

    def wait_scatter(blk, slot):
        def body(r, carry):
            scatter_copy(blk, r, slot).wait()
            return carry
        lax.fori_loop(0, nv_ref[blk], body, 0)

    slot = b % 2

    @pl.when(jnp.logical_and(b == 0, n_used > 0))
    def _():
        start_gather(0, 0)

    @pl.when(b + 1 < n_used)
    def _():
        start_gather(b + 1, 1 - slot)

    @pl.when(b < n_used)
    def _():
        wait_gather(b, slot)
        x = xbuf[slot].astype(BF16)
        for f0, fw in MOE_FCHUNKS:
            hg = _dot(x, wg_ref[0, :, f0:f0 + fw])
            hu = _dot(x, wu_ref[0, :, f0:f0 + fw])
            hbuf[:, f0:f0 + fw] = (hg * jax.nn.sigmoid(hg) * hu).astype(BF16)
        ybuf[slot] = _dot(hbuf[...], wd_ref[0]) * rw_ref[0]

    @pl.when(jnp.logical_and(b >= 1, b - 1 < n_used))
    def _():
        wait_scatter(b - 1, 1 - slot)

    @pl.when(b < n_used)
    def _():
        def body(r, carry):
            scatter_copy(b, r, slot).start()
            return carry
        lax.fori_loop(0, nv_ref[b], body, 0)

    @pl.when(jnp.logical_and(b == nb - 1, b < n_used))
    def _():
        wait_scatter(b, slot)


def moe_experts(f, row_w, block_e, row_tok, row_dst, n_valid, n_used, wg, wu, wd):
    nt, d = f.shape
    n_blocks = block_e.shape[0]
    fexp = wg.shape[2]
    grid_spec = pltpu.PrefetchScalarGridSpec(
        num_scalar_prefetch=5,
        grid=(n_blocks,),
        in_specs=[pl.BlockSpec(memory_space=pl.ANY),
                  pl.BlockSpec((1, MOE_BM, 1), lambda b, be, *_: (b, 0, 0)),
                  pl.BlockSpec((1, d, fexp), lambda b, be, *_: (be[b], 0, 0)),
                  pl.BlockSpec((1, d, fexp), lambda b, be, *_: (be[b], 0, 0)),
                  pl.BlockSpec((1, fexp, d), lambda b, be, *_: (be[b], 0, 0))],
        out_specs=pl.BlockSpec(memory_space=pl.ANY),
        scratch_shapes=[pltpu.VMEM((2, MOE_BM, d), F32), pltpu.VMEM((2, MOE_BM, d), F32),
                        pltpu.VMEM((MOE_BM, fexp), BF16),
                        pltpu.SemaphoreType.DMA((2,)), pltpu.SemaphoreType.DMA((2,))],
    )
    return pl.pallas_call(
        _moe_kernel,
        grid_spec=grid_spec,
        out_shape=jax.ShapeDtypeStruct((TOP_K * nt, d), F32),
        compiler_params=_cparams(),
        name="moe_experts",
    )(block_e, row_tok, row_dst, n_valid, n_used, f, row_w.reshape(n_blocks, MOE_BM, 1), wg, wu, wd)


def route(logits_t, b_router):
    nt = logits_t.shape[1]
    aff = jax.nn.sigmoid(logits_t.T)
    grouped = (aff + b_router.astype(F32)).reshape(-1, N_EXPERT_GROUPS, EXPERTS_PER_GROUP)
    group_score = jnp.sum(lax.top_k(grouped, TOP_K)[0], axis=-1)
    g_sel = jnp.argmax(group_score, axis=-1)
    in_group = jnp.take_along_axis(grouped, g_sel[:, None, None], axis=1)[:, 0]
    _, local = lax.top_k(in_group, TOP_K)
    idx = (g_sel[:, None] * EXPERTS_PER_GROUP + local).astype(jnp.int32)
    w = jnp.take_along_axis(aff, idx, axis=1)
    wts = w / jnp.sum(w, axis=-1, keepdims=True)

    n_assign = nt * TOP_K
    n_blocks = -(-n_assign // MOE_BM) + N_EXPERTS
    n_rows = n_blocks * MOE_BM
    flat_e = idx.reshape(-1)
    order = jnp.argsort(flat_e).astype(jnp.int32)
    sorted_e = flat_e[order]
    counts = jnp.bincount(flat_e, length=N_EXPERTS).astype(jnp.int32)
    padded = (counts + MOE_BM - 1) // MOE_BM * MOE_BM
    pad_end = jnp.cumsum(padded)
    pad_start = pad_end - padded
    start = jnp.cumsum(counts) - counts
    pos = pad_start[sorted_e] + jnp.arange(n_assign, dtype=jnp.int32) - start[sorted_e]
    tok = order // TOP_K
    row_tok = jnp.zeros((n_rows,), jnp.int32).at[pos].set(tok)
    row_dst = jnp.zeros((n_rows,), jnp.int32).at[pos].set((order % TOP_K) * nt + tok)
    row_w = jnp.zeros((n_rows,), F32).at[pos].set(wts.reshape(-1)[order])
    blk_start = jnp.arange(n_blocks, dtype=jnp.int32) * MOE_BM
    block_e = jnp.minimum(jnp.searchsorted(pad_end, blk_start, side='right'), N_EXPERTS - 1).astype(jnp.int32)
    n_valid = jnp.clip(counts[block_e] - (blk_start - pad_start[block_e]), 0, MOE_BM).astype(jnp.int32)
    n_used = (pad_end[-1] // MOE_BM).astype(jnp.int32).reshape(1)
    n_valid = jnp.where(jnp.arange(n_blocks) < n_used[0], n_valid, 0)
    return row_w, block_e, row_tok, row_dst, n_valid, n_used


def _final_kernel(x_ref, o0_ref, o1_ref, mod_ref, g_ref, out_ref):
    x = x_ref[...] + mod_ref[0][5:6] * (o0_ref[...] + o1_ref[...])
    ms = jnp.mean(x * x, axis=-1, keepdims=True)
    out_ref[...] = x * lax.rsqrt(ms + EPS) * g_ref[...]


def final_norm(x, moe_out, mods_l, g_final, n_ctx_blocks):
    nt, d = x.shape
    nblk = nt // TM
    nlat = nblk - n_ctx_blocks
    return pl.pallas_call(
        _final_kernel,
        grid=(nlat,),
        in_specs=[pl.BlockSpec((TM, d), lambda i: (i + n_ctx_blocks, 0)),
                  pl.BlockSpec((TM, d), lambda i: (i + n_ctx_blocks, 0)),
                  pl.BlockSpec((TM, d), lambda i: (i + n_ctx_blocks + nblk, 0)),
                  pl.BlockSpec((1, 6, d), lambda i: (0, 0, 0)),
                  pl.BlockSpec((1, d), lambda i: (0, 0))],
        out_specs=pl.BlockSpec((TM, d), lambda i: (i, 0)),
        out_shape=jax.ShapeDtypeStruct((nlat * TM, d), F32),
        compiler_params=_cparams(),
        name="final_norm",
    )(x, moe_out, moe_out, mods_l, g_final.reshape(1, d))


def kernel(x, c, ctx, c_ctx, w_mod, b_mod, g_mix, g_ffn, w_in_even, w_out_even, sgu_ln_g, sgu_ln_b, sgu_w, sgu_b, conv_w, conv_b, conv_ln_g, conv_ln_b, w_in_odd, w_out_odd, attn_sink, ssm_lam_re, ssm_lam_im, ssm_log_dt, ssm_b_re, ssm_b_im, ssm_c_re, ssm_c_im, ssm_d, glu_w, glu_b, w_router, b_router, w_gate, w_up, w_down, g_final):
    bsz, n_lat, d = x.shape
    n_ctx = ctx.shape[1]
    assert bsz == 1 and n_ctx % TM == 0 and n_lat % TM == 0
    nt = n_ctx + n_lat
    ncb = n_ctx // TM
    n_chunks = nt // S_CHUNK
    ncp = -(-n_chunks // 16) * 16

    xs = jnp.concatenate([ctx[0], x[0]], axis=0)
    cond8 = jnp.concatenate([c, c_ctx[None, :], jnp.zeros((6, d), F32)], axis=0)
    mods = adaln_all(cond8, w_mod, b_mod)[:, :2].reshape(DEPTH, 2, 6, d)
    cos_t, sin_t = rope_tables(n_ctx, n_lat)
    w_router_t = w_router.T

    moe_out = None
    for l in range(DEPTH):
        j = l // 2
        odd = l % 2 == 1
        if odd:
            wi = w_in_odd[j]
            w_in = jnp.concatenate([wi[:, :C_WIDTH], wi[:, C_WIDTH + 2 * KV_WIDTH:],
                                    wi[:, C_WIDTH:C_WIDTH + 2 * KV_WIDTH]], axis=1).astype(BF16)
            w_out = w_out_odd[j].astype(BF16)
        else:
            w_in = w_in_even[j].astype(BF16)
            w_out = w_out_even[j].astype(BF16)
        x_new, p = inproj(xs, g_mix[l], mods[l], w_in, ncb, moe_out, mods[l - 1] if l > 0 else None)
        if x_new is not None:
            xs = x_new
        if odd:
            ya = attention(p, attn_sink[j], cos_t, sin_t, n_ctx)
            t_m, w_s, w_o, lam_rows = s5_weights(ssm_lam_re[j], ssm_lam_im[j], ssm_log_dt[j], ssm_b_re[j],
                                                 ssm_b_im[j], ssm_c_re[j], ssm_c_im[j])
            u = p[:, C_WIDTH:C_WIDTH + S_WIDTH].reshape(n_chunks, S_CHUNK, S_GROUPS, S_GROUP_CH)
            u_g = jnp.transpose(u, (2, 0, 1, 3)).reshape(S_GROUPS, n_chunks, S_LH)
            u_g = jnp.pad(u_g, ((0, 0), (0, ncp - n_chunks), (0, 0)))
            y_g = s5_scan(u_g, t_m, w_s, w_o, lam_rows, n_chunks, n_ctx // S_CHUNK)
            y_ssm = jnp.transpose(y_g[:, :n_chunks].reshape(S_GROUPS, n_chunks, S_CHUNK, S_GROUP_CH),
                                  (1, 2, 0, 3)).reshape(nt, S_WIDTH)
            yb = s5_glu(y_ssm, p, ssm_d[j], glu_w[j].astype(BF16), glu_b[j])
        else:
            bs_full = jnp.broadcast_to(sgu_b[j][:, :, None], (A_GROUPS, CHUNK, CHUNK)).astype(F32)
            ya, yb = even_mixer(p, sgu_ln_g[j], sgu_ln_b[j], sgu_w[j].astype(BF16), bs_full,
                                conv_w[j], conv_b[j], conv_ln_g[j], conv_ln_b[j], ncb)
        xs, f, logits_t = outproj(ya, yb, w_out[:1024], w_out[1024:], xs, mods[l], g_ffn[l], w_router_t, ncb)
        row_w, block_e, row_tok, row_dst, n_valid, n_used = route(logits_t, b_router)
        moe_out = moe_experts(f, row_w, block_e, row_tok, row_dst, n_valid, n_used,
                              w_gate[l].astype(BF16), w_up[l].astype(BF16), w_down[l].astype(BF16))
    out = final_norm(xs, moe_out, mods[DEPTH - 1], g_final, ncb)
    return out.reshape(bsz, n_lat, d)
```

```python
import functools
import math

import jax
import jax.numpy as jnp
from jax import lax
from jax.experimental import pallas as pl
from jax.experimental.pallas import tpu as pltpu

F32 = jnp.float32
BF16 = jnp.bfloat16

D_MODEL = 2048
DEPTH = 4
GRID_W = 64
EPS = 1e-6
NEG_INF = -1e30

A_WIDTH = 1024
A_GROUPS = 8
CHUNK = 128
B_WIDTH = 1024
CONV_WIDTH = 31
CONV_HALO = 16

HEAD_DIM = 128
N_Q_HEADS = 8
N_KV_HEADS = 2
Q_PER_KV = 4
C_WIDTH = 1024
KV_WIDTH = 256
WINDOW = 128
ATTN_BLOCK = 128
ROPE_BASE = 10000.0
ATTN_SCALE = HEAD_DIM ** -0.5
S_WIDTH = 1024
S_GROUP_CH = 16
S_GROUPS = 64
S_STATE = 64
S_CHUNK = 32
S_GROUP_BATCH = 8
ODD_IN = C_WIDTH + 2 * KV_WIDTH + S_WIDTH

N_EXPERTS = 16
N_EXPERT_GROUPS = 4
EXPERTS_PER_GROUP = 4
TOP_K = 2
D_EXPERT = 1408
MOE_BM = 256

TM = 256
VMEM_LIMIT = 56 * 1024 * 1024


def _cparams(n_axes=1, vmem=VMEM_LIMIT):
    return pltpu.CompilerParams(dimension_semantics=("arbitrary",) * n_axes, vmem_limit_bytes=vmem)


def _dot(a, b):
    return jnp.dot(a, b, preferred_element_type=F32)


def _dot_nt(a, b):
    return lax.dot_general(a, b, (((1,), (1,)), ((), ())), preferred_element_type=F32)


ADALN_TN = 1024


def _adaln_kernel(cond_ref, w_ref, b_ref, o_ref):
    c = cond_ref[...]
    s = (c * jax.nn.sigmoid(c)).astype(BF16)
    o_ref[0] = _dot(s, w_ref[0].astype(BF16)) + b_ref[0]


def adaln_all(cond8, w_mod, b_mod):
    depth, d, n6 = w_mod.shape
    return pl.pallas_call(
        _adaln_kernel,
        grid=(depth, n6 // ADALN_TN),
        in_specs=[
            pl.BlockSpec((8, d), lambda l, j: (0, 0)),
            pl.BlockSpec((1, d, ADALN_TN), lambda l, j: (l, 0, j)),
            pl.BlockSpec((1, 1, ADALN_TN), lambda l, j: (l, 0, j)),
        ],
        out_specs=pl.BlockSpec((1, 8, ADALN_TN), lambda l, j: (l, 0, j)),
        out_shape=jax.ShapeDtypeStruct((depth, 8, n6), F32),
        compiler_params=_cparams(2),
        name="adaln",
    )(cond8, w_mod, b_mod.reshape(depth, 1, n6))


def _mod_spec(n_ctx_blocks):
    return pl.BlockSpec((1, 6, D_MODEL), lambda i, *_: (jnp.where(i < n_ctx_blocks, 1, 0), 0, 0))


def _rms_mod(x, g, shift, scale):
    ms = jnp.mean(x * x, axis=-1, keepdims=True)
    y = x * lax.rsqrt(ms + EPS) * g
    return y * (1.0 + scale) + shift


SLAB = D_MODEL // 128


def _slab_to_rows(ref, n_rows):
    return jnp.concatenate([ref[pl.ds(c, n_rows, stride=SLAB), :] for c in range(SLAB)], axis=1)


def _rows_to_slab(ref, val, n_rows, c0):
    for c in range(val.shape[1] // 128):
        ref[pl.ds(c0 + c, n_rows, stride=SLAB), :] = val[:, c * 128:(c + 1) * 128]


def _expert_row_gather(pos_ref, y_hbm, gbuf, sem, tile, slot, nt, start):
    def body(t, carry):
        for k in range(TOP_K):
            p = pl.multiple_of(pos_ref[k * nt + tile * TM + t] * SLAB, SLAB)
            cp = pltpu.make_async_copy(y_hbm.at[pl.ds(p, SLAB), :],
                                       gbuf.at[slot, k, pl.ds(pl.multiple_of(t * SLAB, SLAB), SLAB), :],
                                       sem.at[slot])
            if start:
                cp.start()
            else:
                cp.wait()
        return carry
    lax.fori_loop(0, TM, body, 0, unroll=8)


def _moe_combine(pos_ref, y_hbm, wt_ref, gbuf, sem, nt, tile0):
    i = pl.program_id(0)
    slot = i % 2

    @pl.when(i == 0)
    def _():
        _expert_row_gather(pos_ref, y_hbm, gbuf, sem, tile0, 0, nt, True)

    @pl.when(i + 1 < pl.num_programs(0))
    def _():
        _expert_row_gather(pos_ref, y_hbm, gbuf, sem, tile0 + i + 1, 1 - slot, nt, True)

    _expert_row_gather(pos_ref, y_hbm, gbuf, sem, tile0 + i, slot, nt, False)
    wt = wt_ref[...]
    return (wt[:, 0:1] * _slab_to_rows(gbuf.at[slot, 0], TM) + wt[:, 1:2] * _slab_to_rows(gbuf.at[slot, 1], TM))


_COMBINE_SCRATCH = [pltpu.VMEM((2, TOP_K, TM * SLAB, 128), F32), pltpu.SemaphoreType.DMA((2,))]


INPROJ_NC = 512


def _inproj_kernel(*refs, combine, nt):
    if combine:
        pos_ref, x_ref, y_hbm, wt_ref, modp_ref, g_ref, mod_ref, w_ref, xo_ref, p_ref, gbuf, sem = refs
        x = x_ref[...] + modp_ref[0][5:6] * _moe_combine(pos_ref, y_hbm, wt_ref, gbuf, sem, nt, 0)
        xo_ref[...] = x
    else:
        x_ref, g_ref, mod_ref, w_ref, p_ref = refs
        x = x_ref[...]
    m = mod_ref[0]
    h = _rms_mod(x, g_ref[...], m[0:1], m[1:2]).astype(BF16)
    n = w_ref.shape[1]
    for j in range(0, n, INPROJ_NC):
        p_ref[:, j:j + INPROJ_NC] = _dot(h, w_ref[:, j:j + INPROJ_NC])


def inproj(x, g, mods_l, w_bf16, n_ctx_blocks, moe=None, mods_prev=None):
    nt, d = x.shape
    n = w_bf16.shape[1]
    nblk = nt // TM
    row = pl.BlockSpec((TM, d), lambda i, *_: (i, 0))
    g_spec = pl.BlockSpec((1, d), lambda i, *_: (0, 0))
    w_spec = pl.BlockSpec((d, n), lambda i, *_: (0, 0), pipeline_mode=pl.Buffered(1))
    p_spec = pl.BlockSpec((TM, n), lambda i, *_: (i, 0))
    p_shape = jax.ShapeDtypeStruct((nt, n), F32)
    if moe is None:
        return None, pl.pallas_call(
            functools.partial(_inproj_kernel, combine=False, nt=nt),
            grid=(nblk,),
            in_specs=[row, g_spec, _mod_spec(n_ctx_blocks), w_spec],
            out_specs=p_spec,
            out_shape=p_shape,
            compiler_params=_cparams(),
            name="inproj",
        )(x, g.reshape(1, d), mods_l, w_bf16)
    y_sorted, pos, wts_t = moe
    grid_spec = pltpu.PrefetchScalarGridSpec(
        num_scalar_prefetch=1,
        grid=(nblk,),
        in_specs=[row, pl.BlockSpec(memory_space=pl.ANY), pl.BlockSpec((TM, 8), lambda i, *_: (i, 0)),
                  _mod_spec(n_ctx_blocks), g_spec, _mod_spec(n_ctx_blocks), w_spec],
        out_specs=[row, p_spec],
        scratch_shapes=_COMBINE_SCRATCH,
    )
    return pl.pallas_call(
        functools.partial(_inproj_kernel, combine=True, nt=nt),
        grid_spec=grid_spec,
        out_shape=[jax.ShapeDtypeStruct((nt, d), F32), p_shape],
        compiler_params=_cparams(),
        name="combine_inproj",
    )(pos, x, y_sorted, wts_t, mods_prev, g.reshape(1, d), mods_l, w_bf16)


CONV_RC = 64


def _layer_norm(x, g, b):
    mu = jnp.mean(x, axis=-1, keepdims=True)
    xc = x - mu
    var = jnp.mean(xc * xc, axis=-1, keepdims=True)
    return xc * lax.rsqrt(var + EPS) * g + b


def _even_kernel(u_ref, v_ref, a_ref, g_ref, ap_ref, gp_ref, an_ref, gn_ref,
                 lng_ref, lnb_ref, ws_ref, bs_ref, cw_ref, cb_ref, clg_ref, clb_ref,
                 ya_ref, yb_ref, hpad_ref, cacc_ref, *, n_ctx_blocks, n_blocks):
    i = pl.program_id(0)
    for c in range(TM // CHUNK):
        rows = slice(c * CHUNK, (c + 1) * CHUNK)
        vn = _layer_norm(jax.nn.gelu(v_ref[rows, :]), lng_ref[...], lnb_ref[...]).astype(BF16)
        for grp in range(A_GROUPS):
            cols = slice(grp * CHUNK, (grp + 1) * CHUNK)
            mixed = _dot(ws_ref[grp], vn[:, cols]) + bs_ref[grp]
            ya_ref[rows, cols] = (jax.nn.gelu(u_ref[rows, cols]) * mixed).astype(BF16)
    first = jnp.logical_or(i == 0, i == n_ctx_blocks)
    last = jnp.logical_or(i == n_ctx_blocks - 1, i == n_blocks - 1)
    hpad_ref[0:CONV_HALO, :] = jnp.where(first, 0.0, ap_ref[...] * jax.nn.sigmoid(gp_ref[...]))
    hpad_ref[CONV_HALO:CONV_HALO + TM, :] = a_ref[...] * jax.nn.sigmoid(g_ref[...])
    hpad_ref[CONV_HALO + TM:, :] = jnp.where(last, 0.0, an_ref[...] * jax.nn.sigmoid(gn_ref[...]))
    off = CONV_HALO - CONV_WIDTH // 2
    for cc in range(B_WIDTH // 128):
        cols = slice(cc * 128, (cc + 1) * 128)
        for rc in range(TM // CONV_RC):
            acc = jnp.zeros((CONV_RC, 128), F32)
            for k in range(CONV_WIDTH):
                r0 = rc * CONV_RC + k + off
                acc = acc + cw_ref[k:k + 1, cols] * hpad_ref[r0:r0 + CONV_RC, cols]
            cacc_ref[rc * CONV_RC:(rc + 1) * CONV_RC, cols] = acc
    hc = _layer_norm(cacc_ref[...] + cb_ref[...], clg_ref[...], clb_ref[...])
    yb_ref[...] = (hc * jax.nn.sigmoid(hc)).astype(BF16)


def even_mixer(p, ln_g, ln_b, ws_bf16, bs_full, conv_w, conv_b, cln_g, cln_b, n_ctx_blocks):
    nt = p.shape[0]
    nblk = nt // TM
    hb = TM // CONV_HALO
    last_h = nt // CONV_HALO - 1
    col = lambda j: pl.BlockSpec((TM, 1024), lambda i: (i, j))
    prev = lambda j: pl.BlockSpec((CONV_HALO, 1024), lambda i: (jnp.maximum(i * hb - 1, 0), j))
    nxt = lambda j: pl.BlockSpec((CONV_HALO, 1024), lambda i: (jnp.minimum((i + 1) * hb, last_h), j))
    vec = pl.BlockSpec((1, 1024), lambda i: (0, 0))
    out = pl.BlockSpec((TM, 1024), lambda i: (i, 0))
    return pl.pallas_call(
        functools.partial(_even_kernel, n_ctx_blocks=n_ctx_blocks, n_blocks=nblk),
        grid=(nblk,),
        in_specs=[col(0), col(1), col(2), col(3), prev(2), prev(3), nxt(2), nxt(3),
                  vec, vec,
                  pl.BlockSpec((A_GROUPS, CHUNK, CHUNK), lambda i: (0, 0, 0)),
                  pl.BlockSpec((A_GROUPS, CHUNK, CHUNK), lambda i: (0, 0, 0)),
                  pl.BlockSpec((CONV_WIDTH, 1024), lambda i: (0, 0)),
                  vec, vec, vec],
        out_specs=[out, out],
        out_shape=[jax.ShapeDtypeStruct((nt, 1024), BF16)] * 2,
        scratch_shapes=[pltpu.VMEM((TM + 2 * CONV_HALO, 1024), F32), pltpu.VMEM((TM, 1024), F32)],
        compiler_params=_cparams(),
        name="even_mixer",
    )(p, p, p, p, p, p, p, p, ln_g.reshape(1, -1), ln_b.reshape(1, -1), ws_bf16, bs_full,
      conv_w, conv_b.reshape(1, -1), cln_g.reshape(1, -1), cln_b.reshape(1, -1))


def _second_max(a0, a1, a2, a3):
    m01, n01 = jnp.maximum(a0, a1), jnp.minimum(a0, a1)
    m23, n23 = jnp.maximum(a2, a3), jnp.minimum(a2, a3)
    return jnp.maximum(m01, m23), jnp.maximum(jnp.minimum(m01, m23), jnp.maximum(n01, n23))


def _route_tile(logits, b_col):
    aff = jax.nn.sigmoid(logits)
    biased = aff + b_col
    row = lambda m, e: m[e:e + 1, :]
    g_sel = best = None
    for g in range(N_EXPERT_GROUPS):
        top1, top2 = _second_max(*[row(biased, EXPERTS_PER_GROUP * g + j) for j in range(EXPERTS_PER_GROUP)])
        score = top1 + top2
        if g == 0:
            g_sel, best = jnp.zeros(score.shape, jnp.int32), score
        else:
            upd = score > best
            g_sel, best = jnp.where(upd, g, g_sel), jnp.where(upd, score, best)

    def in_group(m, j):
        out = row(m, (N_EXPERT_GROUPS - 1) * EXPERTS_PER_GROUP + j)
        for g in range(N_EXPERT_GROUPS - 2, -1, -1):
            out = jnp.where(g_sel == g, row(m, EXPERTS_PER_GROUP * g + j), out)
        return out

    v = [in_group(biased, j) for j in range(EXPERTS_PER_GROUP)]
    a = [in_group(aff, j) for j in range(EXPERTS_PER_GROUP)]
    i1, b1, w1 = jnp.zeros(g_sel.shape, jnp.int32), v[0], a[0]
    for j in range(1, EXPERTS_PER_GROUP):
        upd = v[j] > b1
        i1, b1, w1 = jnp.where(upd, j, i1), jnp.where(upd, v[j], b1), jnp.where(upd, a[j], w1)
    first = i1 == 0
    i2, b2, w2 = jnp.where(first, 1, 0), jnp.where(first, v[1], v[0]), jnp.where(first, a[1], a[0])
    for j in range(1, EXPERTS_PER_GROUP):
        upd = jnp.logical_and(i1 != j, v[j] > b2)
        i2, b2, w2 = jnp.where(upd, j, i2), jnp.where(upd, v[j], b2), jnp.where(upd, a[j], w2)
    den = w1 + w2
    return EXPERTS_PER_GROUP * g_sel + i1, EXPERTS_PER_GROUP * g_sel + i2, w1 / den, w2 / den


def _rows8(r0, r1):
    sub = lax.broadcasted_iota(jnp.int32, (8, r0.shape[1]), 0)
    return jnp.where(sub == 0, r0, jnp.where(sub == 1, r1, jnp.zeros_like(r0)))


def _outproj_kernel(ya_ref, yb_ref, wa_ref, wb_ref, x_ref, mod_ref, g_ref, wr_ref, br_ref,
                    xo_ref, f_ref, e_ref, w_ref, r_ref, cnt_ref, carry_ref):
    i = pl.program_id(0)
    m = mod_ref[0]
    y = _dot(ya_ref[...], wa_ref[...]) + _dot(yb_ref[...], wb_ref[...])
    x = x_ref[...] + m[2:3] * y
    xo_ref[...] = x
    f = _rms_mod(x, g_ref[...], m[3:4], m[4:5])
    _rows_to_slab(f_ref, f, TM, 0)
    f_hi = f.astype(BF16)
    f_lo = (f - f_hi.astype(F32)).astype(BF16)
    wr = wr_ref[...]
    w_hi = wr.astype(BF16)
    w_lo = (wr - w_hi.astype(F32)).astype(BF16)
    logits = _dot_nt(w_hi, f_hi) + _dot_nt(w_lo, f_hi) + _dot_nt(w_hi, f_lo)
    e0, e1, w0, w1 = _route_tile(logits, br_ref[:, 0:1])

    @pl.when(i == 0)
    def _():
        carry_ref[...] = jnp.zeros(carry_ref.shape, F32)

    sub = lax.broadcasted_iota(jnp.int32, (N_EXPERTS, TM), 0)
    hot0, hot1 = sub == e0, sub == e1
    member = jnp.where(jnp.logical_or(hot0, hot1), 1.0, 0.0)
    tri = jnp.where(lax.broadcasted_iota(jnp.int32, (TM, TM), 0) < lax.broadcasted_iota(jnp.int32, (TM, TM), 1),
                    1.0, 0.0).astype(BF16)
    before = _dot(member.astype(BF16), tri) + carry_ref[:, 0:1]
    r0 = jnp.sum(jnp.where(hot0, before, 0.0), axis=0, keepdims=True)
    r1 = jnp.sum(jnp.where(hot1, before, 0.0), axis=0, keepdims=True)
    carry_ref[...] = carry_ref[...] + jnp.sum(member, axis=1, keepdims=True)
    e_ref[...] = _rows8(e0, e1)
    w_ref[...] = _rows8(w0, w1)
    r_ref[...] = _rows8(r0.astype(jnp.int32), r1.astype(jnp.int32))
    cnt_ref[...] = carry_ref[...]


def outproj(ya, yb, wa, wb, x, mods_l, g_ffn, w_router_t, b_router_col, n_ctx_blocks):
    nt, d = x.shape
    nblk = nt // TM
    half = pl.BlockSpec((TM, 1024), lambda i: (i, 0))
    wsp = pl.BlockSpec((1024, d), lambda i: (0, 0), pipeline_mode=pl.Buffered(1))
    row = pl.BlockSpec((TM, d), lambda i: (i, 0))
    r8 = pl.BlockSpec((8, TM), lambda i: (0, i))
    cnt = pl.BlockSpec((N_EXPERTS, 128), lambda i: (0, 0))
    return pl.pallas_call(
        _outproj_kernel,
        grid=(nblk,),
        in_specs=[half, half, wsp, wsp, row, _mod_spec(n_ctx_blocks),
                  pl.BlockSpec((1, d), lambda i: (0, 0)),
                  pl.BlockSpec((N_EXPERTS, d), lambda i: (0, 0)), cnt],
        out_specs=[row, pl.BlockSpec((TM * SLAB, 128), lambda i: (i, 0)), r8, r8, r8, cnt],
        out_shape=[jax.ShapeDtypeStruct((nt, d), F32), jax.ShapeDtypeStruct((nt * SLAB, 128), F32),
                   jax.ShapeDtypeStruct((8, nt), jnp.int32), jax.ShapeDtypeStruct((8, nt), F32),
                   jax.ShapeDtypeStruct((8, nt), jnp.int32), jax.ShapeDtypeStruct((N_EXPERTS, 128), F32)],
        scratch_shapes=[pltpu.VMEM((N_EXPERTS, 128), F32)],
        compiler_params=_cparams(),
        name="outproj",
    )(ya, yb, wa, wb, x, mods_l, g_ffn.reshape(1, d), w_router_t, b_router_col)


N_LOC = 3 * ATTN_BLOCK


def _rope(x, cos, sin):
    lane = lax.broadcasted_iota(jnp.int32, x.shape, 1)
    swapped = jnp.where(lane % 64 < 32, pltpu.roll(x, 96, axis=1), pltpu.roll(x, 32, axis=1))
    return x * cos + swapped * sin


def _attn_kernel(sink_ref, q_ref, kp_ref, kc_ref, kn_ref, vp_ref, vc_ref, vn_ref, kx_ref, vx_ref,
                 cosp_ref, cosc_ref, cosn_ref, sinp_ref, sinc_ref, sinn_ref, o_ref,
                 qs_ref, kbuf_ref, vbuf_ref, *, n_ctx_blocks, n_blocks, n_ctx):
    i = pl.program_id(0)
    nkeys = N_LOC + n_ctx
    nq = Q_PER_KV * ATTN_BLOCK
    is_lat = jnp.where(i >= n_ctx_blocks, 1, 0)
    prev_ok = jnp.where(i - 1 >= n_ctx_blocks, is_lat, 0)
    next_ok = jnp.where(i + 1 <= n_blocks - 1, is_lat, 0)
    qi = lax.broadcasted_iota(jnp.int32, (nq, nkeys), 0) & (ATTN_BLOCK - 1)
    kj = lax.broadcasted_iota(jnp.int32, (nq, nkeys), 1)
    rel = kj - ATTN_BLOCK - qi
    blk_ok = jnp.where(kj < ATTN_BLOCK, prev_ok, jnp.where(kj < 2 * ATTN_BLOCK, is_lat, next_ok))
    rel = jnp.where(blk_ok > 0, rel, WINDOW + 1)
    valid = jnp.logical_or(kj >= N_LOC, jnp.logical_and(rel >= -WINDOW, rel <= WINDOW))
    rowh = lax.broadcasted_iota(jnp.int32, (nq, 1), 0) // ATTN_BLOCK
    for h in range(N_KV_HEADS):
        hc = slice(h * HEAD_DIM, (h + 1) * HEAD_DIM)
        kbuf_ref[0:ATTN_BLOCK, :] = _rope(kp_ref[:, hc], cosp_ref[...], sinp_ref[...]).astype(BF16)
        kbuf_ref[ATTN_BLOCK:2 * ATTN_BLOCK, :] = _rope(kc_ref[:, hc], cosc_ref[...], sinc_ref[...]).astype(BF16)
        kbuf_ref[2 * ATTN_BLOCK:N_LOC, :] = _rope(kn_ref[:, hc], cosn_ref[...], sinn_ref[...]).astype(BF16)
        kbuf_ref[N_LOC:, :] = kx_ref[:, hc].astype(BF16)
        vbuf_ref[0:ATTN_BLOCK, :] = vp_ref[:, hc].astype(BF16)
        vbuf_ref[ATTN_BLOCK:2 * ATTN_BLOCK, :] = vc_ref[:, hc].astype(BF16)
        vbuf_ref[2 * ATTN_BLOCK:N_LOC, :] = vn_ref[:, hc].astype(BF16)
        vbuf_ref[N_LOC:, :] = vx_ref[:, hc].astype(BF16)
        sink = jnp.zeros((Q_PER_KV * ATTN_BLOCK, 1), F32)
        for gq in range(Q_PER_KV):
            head = h * Q_PER_KV + gq
            qc = slice(head * HEAD_DIM, (head + 1) * HEAD_DIM)
            qs_ref[gq * ATTN_BLOCK:(gq + 1) * ATTN_BLOCK, :] = _rope(
                q_ref[:, qc], cosc_ref[...], sinc_ref[...]).astype(BF16)
            sink = jnp.where(rowh == gq, sink_ref[head], sink)
        s = _dot_nt(qs_ref[...], kbuf_ref[...]) * ATTN_SCALE
        s = jnp.where(valid, s, NEG_INF)
        mx = jnp.maximum(jnp.max(s, axis=-1, keepdims=True), sink)
        p = jnp.exp(s - mx)
        den = jnp.sum(p, axis=-1, keepdims=True) + jnp.exp(sink - mx)
        o = _dot(p.astype(BF16), vbuf_ref[...]) / den
        for gq in range(Q_PER_KV):
            head = h * Q_PER_KV + gq
            o_ref[:, head * HEAD_DIM:(head + 1) * HEAD_DIM] = o[gq * ATTN_BLOCK:(gq + 1) * ATTN_BLOCK].astype(BF16)


def attention(p, sink, cos_t, sin_t, n_ctx):
    nt = p.shape[0]
    nblk = nt // ATTN_BLOCK
    ncb = n_ctx // ATTN_BLOCK
    kcol, vcol = 2048 // KV_WIDTH, 2048 // KV_WIDTH + 1
    pm = lambda i: jnp.maximum(i - 1, 0)
    nx = lambda i: jnp.minimum(i + 1, nblk - 1)
    kv = lambda f, c: pl.BlockSpec((ATTN_BLOCK, KV_WIDTH), lambda i, s: (f(i), c))
    tab = lambda f: pl.BlockSpec((ATTN_BLOCK, HEAD_DIM), lambda i, s: (f(i), 0))
    same = lambda i: i
    grid_spec = pltpu.PrefetchScalarGridSpec(
        num_scalar_prefetch=1,
        grid=(nblk,),
        in_specs=[pl.BlockSpec((ATTN_BLOCK, C_WIDTH), lambda i, s: (i, 0)),
                  kv(pm, kcol), kv(same, kcol), kv(nx, kcol), kv(pm, vcol), kv(same, vcol), kv(nx, vcol),
                  pl.BlockSpec((n_ctx, KV_WIDTH), lambda i, s: (0, kcol)),
                  pl.BlockSpec((n_ctx, KV_WIDTH), lambda i, s: (0, vcol)),
                  tab(pm), tab(same), tab(nx), tab(pm), tab(same), tab(nx)],
        out_specs=pl.BlockSpec((ATTN_BLOCK, C_WIDTH), lambda i, s: (i, 0)),
        scratch_shapes=[pltpu.VMEM((Q_PER_KV * ATTN_BLOCK, HEAD_DIM), BF16),
                        pltpu.VMEM((N_LOC + n_ctx, HEAD_DIM), BF16),
                        pltpu.VMEM((N_LOC + n_ctx, HEAD_DIM), BF16)],
    )
    return pl.pallas_call(
        functools.partial(_attn_kernel, n_ctx_blocks=ncb, n_blocks=nblk, n_ctx=n_ctx),
        grid_spec=grid_spec,
        out_shape=jax.ShapeDtypeStruct((nt, C_WIDTH), BF16),
        compiler_params=_cparams(),
        name="attention",
    )(sink, p, p, p, p, p, p, p, p, p, cos_t, cos_t, cos_t, sin_t, sin_t, sin_t)


def rope_tables(n_ctx, n_lat):
    rows = n_lat // GRID_W
    row = jnp.broadcast_to(jnp.arange(rows, dtype=F32)[:, None], (rows, GRID_W)).reshape(-1)
    col = jnp.broadcast_to(jnp.arange(GRID_W, dtype=F32)[None, :], (rows, GRID_W)).reshape(-1)
    half = HEAD_DIM // 2
    inv_freq = ROPE_BASE ** (-jnp.arange(0, half, 2, dtype=F32) / half)
    ang_r = row[:, None] * inv_freq
    ang_c = col[:, None] * inv_freq
    cos = jnp.concatenate([jnp.cos(ang_r), jnp.cos(ang_r), jnp.cos(ang_c), jnp.cos(ang_c)], axis=-1)
    sin = jnp.concatenate([-jnp.sin(ang_r), jnp.sin(ang_r), -jnp.sin(ang_c), jnp.sin(ang_c)], axis=-1)
    cos = jnp.concatenate([jnp.ones((n_ctx, HEAD_DIM), F32), cos], axis=0)
    sin = jnp.concatenate([jnp.zeros((n_ctx, HEAD_DIM), F32), sin], axis=0)
    return cos, sin


S_LH = S_CHUNK * S_GROUP_CH
S_HALF = S_GROUP_BATCH * 128


def _s5_kernel(u_ref, t_ref, ws_ref, wo_ref, lam_ref, y_ref, s_ref, hf_ref, hr_ref, *, n_chunks, n_ctx_chunks):
    for g in range(S_GROUP_BATCH):
        s = _dot(u_ref[g].astype(BF16), ws_ref[g])
        s_ref[:, g * 128:(g + 1) * 128] = s[:, 0:128]
        s_ref[:, S_HALF + g * 128:S_HALF + (g + 1) * 128] = s[:, 128:256]
    hf_ref[...] = jnp.zeros(hf_ref.shape, F32)
    hr_ref[...] = jnp.zeros(hr_ref.shape, F32)
    lam = lam_ref[0]
    lam_re, lam_im = lam[:, :S_HALF], lam[:, S_HALF:]
    is_fwd = lax.broadcasted_iota(jnp.int32, (1, 2 * S_HALF), 1) % 128 < S_STATE

    def step(k, state):
        st_re, st_im = state
        cf = k
        cr = jnp.where(k < n_ctx_chunks, n_ctx_chunks - 1 - k, n_chunks - 1 - (k - n_ctx_chunks))
        st = jnp.concatenate([st_re, st_im], axis=1)
        hf_ref[pl.ds(cf, 1), :] = st
        hr_ref[pl.ds(cr, 1), :] = st
        s_in = jnp.where(is_fwd, s_ref[pl.ds(cf, 1), :], s_ref[pl.ds(cr, 1), :])
        new_re = lam_re * st_re - lam_im * st_im + s_in[:, :S_HALF]
        new_im = lam_re * st_im + lam_im * st_re + s_in[:, S_HALF:]
        return new_re, new_im

    zero = jnp.zeros((1, S_HALF), F32)
    lax.fori_loop(0, n_chunks, step, (zero, zero))
    hin = jnp.where(is_fwd, hf_ref[...], hr_ref[...]).astype(BF16)
    for g in range(S_GROUP_BATCH):
        hin_g = jnp.concatenate([hin[:, g * 128:(g + 1) * 128],
                                 hin[:, S_HALF + g * 128:S_HALF + (g + 1) * 128]], axis=1)
        y_ref[g] = _dot(u_ref[g].astype(BF16), t_ref[g]) + _dot(hin_g, wo_ref[g])


def s5_scan(u_g, t_m, w_s, w_o, lam_rows, n_chunks, n_ctx_chunks):
    g, ncp, lh = u_g.shape
    nb = g // S_GROUP_BATCH
    blk = lambda a, b: pl.BlockSpec((S_GROUP_BATCH, a, b), lambda i: (i, 0, 0))
    return pl.pallas_call(
        functools.partial(_s5_kernel, n_chunks=n_chunks, n_ctx_chunks=n_ctx_chunks),
        grid=(nb,),
        in_specs=[blk(ncp, lh), blk(lh, lh), blk(lh, 256), blk(256, lh),
                  pl.BlockSpec((1, 1, 2 * S_HALF), lambda i: (i, 0, 0))],
        out_specs=blk(ncp, lh),
        out_shape=jax.ShapeDtypeStruct((g, ncp, lh), F32),
        scratch_shapes=[pltpu.VMEM((ncp, 2 * S_HALF), F32)] * 3,
        compiler_params=_cparams(),
        name="s5_scan",
    )(u_g, t_m, w_s, w_o, lam_rows)


def s5_weights(lam_re, lam_im, log_dt, b_re, b_im, c_re, c_im):
    hp = lax.Precision.HIGHEST
    L = S_CHUNK
    lam = lax.complex(lam_re.astype(F32), lam_im.astype(F32))
    dt = jnp.exp(log_dt.astype(F32))[..., None]
    lam_bar = jnp.exp(lam * dt)
    b_bar = ((lam_bar - 1.0) / lam)[..., None] * lax.complex(b_re.astype(F32), b_im.astype(F32))
    c = lax.complex(c_re.astype(F32), c_im.astype(F32))
    taus = jnp.arange(L + 1, dtype=F32)
    pw = jnp.exp((lam * dt)[None] * taus[:, None, None, None])

    def real_einsum(spec, a, b):
        return (jnp.einsum(spec, a.real, b.real, precision=hp) - jnp.einsum(spec, a.imag, b.imag, precision=hp))

    cb = pw[:L, :, :, None, :] * c[None]
    kern = real_einsum('tdgop,dgpi->tdgoi', cb, b_bar)
    kf, kr = kern[:, 0], kern[:, 1]
    kfull = jnp.concatenate([kr[L - 1:0:-1], (kf[0] + kr[0])[None], kf[1:]], axis=0)
    idx = jnp.arange(L)[None, :] - jnp.arange(L)[:, None] + (L - 1)
    t_m = kfull[idx]
    t_m = jnp.transpose(t_m, (2, 0, 4, 1, 3)).reshape(S_GROUPS, S_LH, S_LH)
    pf = pw[L - 1::-1][:L, 0]
    pr = pw[:L, 1]
    wsf = pf[:, :, :, None] * b_bar[0][None]
    wsr = pr[:, :, :, None] * b_bar[1][None]
    w_s = jnp.stack([wsf.real, wsr.real, wsf.imag, wsr.imag], axis=0)
    w_s = jnp.transpose(w_s, (2, 1, 4, 0, 3)).reshape(S_GROUPS, S_LH, 4 * S_STATE)
    of = pw[1:L + 1, 0][:, :, None, :] * c[0][None]
    orv = pw[L:0:-1, 1][:, :, None, :] * c[1][None]
    w_o = jnp.stack([of.real, orv.real, -of.imag, -orv.imag], axis=0)
    w_o = jnp.transpose(w_o, (2, 0, 4, 1, 3)).reshape(S_GROUPS, 4 * S_STATE, S_LH)
    lam_l = pw[L]
    nb = S_GROUPS // S_GROUP_BATCH
    lre = jnp.concatenate([lam_l[0].real, lam_l[1].real], axis=-1).reshape(nb, 1, S_HALF)
    lim = jnp.concatenate([lam_l[0].imag, lam_l[1].imag], axis=-1).reshape(nb, 1, S_HALF)
    return t_m.astype(BF16), w_s.astype(BF16), w_o.astype(BF16), jnp.concatenate([lre, lim], axis=-1)


def _glu_kernel(y_ref, u_ref, d_ref, w_ref, b_ref, o_ref):
    z = jax.nn.gelu(y_ref[...] + d_ref[...] * u_ref[...])
    gate = jax.nn.sigmoid(_dot(z.astype(BF16), w_ref[...]) + b_ref[...])
    o_ref[...] = (z * gate).astype(BF16)


def s5_glu(y_ssm, p, d_skip, glu_w_bf16, glu_b):
    nt = y_ssm.shape[0]
    vec = pl.BlockSpec((1, S_WIDTH), lambda i: (0, 0))
    return pl.pallas_call(
        _glu_kernel,
        grid=(nt // TM,),
        in_specs=[pl.BlockSpec((TM, S_WIDTH), lambda i: (i, 0)),
                  pl.BlockSpec((TM, S_WIDTH), lambda i: (i, 1)),
                  vec, pl.BlockSpec((S_WIDTH, S_WIDTH), lambda i: (0, 0)), vec],
        out_specs=pl.BlockSpec((TM, S_WIDTH), lambda i: (i, 0)),
        out_shape=jax.ShapeDtypeStruct((nt, S_WIDTH), BF16),
        compiler_params=_cparams(),
        name="s5_glu",
    )(y_ssm, p, d_skip.reshape(1, -1), glu_w_bf16, glu_b.reshape(1, -1))


MOE_FCHUNKS = ((0, 512), (512, 512), (1024, 384))
MOE_YC = 512


def _dispatch_kernel(pos_ref, cnt_ref, pad_ref, start_ref, nu_ref, f_ref, xs_hbm, zero_ref, sem, zsem, *,
                     nt, n_blocks):
    i = pl.program_id(0)
    blk_rows = MOE_BM * SLAB

    def token_copies(t, start):
        src = f_ref.at[pl.ds(pl.multiple_of(t * SLAB, SLAB), SLAB), :]
        for k in range(TOP_K):
            p = pl.multiple_of(pos_ref[k * nt + i * TM + t] * SLAB, SLAB)
            cp = pltpu.make_async_copy(src, xs_hbm.at[pl.ds(p, SLAB), :], sem)
            if start:
                cp.start()
            else:
                cp.wait()

    def pad_copies(start):
        for e in range(N_EXPERTS):
            def body(r, carry):
                p = pl.multiple_of((start_ref[e] + r) * SLAB, SLAB)
                cp = pltpu.make_async_copy(zero_ref.at[pl.ds(0, SLAB), :], xs_hbm.at[pl.ds(p, SLAB), :], zsem)
                if start:
                    cp.start()
                else:
                    cp.wait()
                return carry
            lax.fori_loop(cnt_ref[e], pad_ref[e], body, 0)

        def tail(blk, carry):
            p = pl.multiple_of(blk * blk_rows, blk_rows)
            cp = pltpu.make_async_copy(zero_ref, xs_hbm.at[pl.ds(p, blk_rows), :], zsem)
            if start:
                cp.start()
            else:
                cp.wait()
            return carry
        lax.fori_loop(nu_ref[0], n_blocks, tail, 0)

    @pl.when(i == 0)
    def _():
        zero_ref[...] = jnp.zeros(zero_ref.shape, F32)
        pad_copies(True)

    def start_body(t, carry):
        token_copies(t, True)
        return carry

    def wait_body(t, carry):
        token_copies(t, False)
        return carry

    lax.fori_loop(0, TM, start_body, 0, unroll=8)
    lax.fori_loop(0, TM, wait_body, 0, unroll=8)

    @pl.when(i == 0)
    def _():
        pad_copies(False)


def moe_dispatch(f_slab, pos, counts, padded, pad_start, n_used, n_blocks):
    nt = f_slab.shape[0] // SLAB
    grid_spec = pltpu.PrefetchScalarGridSpec(
        num_scalar_prefetch=5,
        grid=(nt // TM,),
        in_specs=[pl.BlockSpec((TM * SLAB, 128), lambda i, *_: (i, 0))],
        out_specs=pl.BlockSpec(memory_space=pl.ANY),
        scratch_shapes=[pltpu.VMEM((MOE_BM * SLAB, 128), F32), pltpu.SemaphoreType.DMA, pltpu.SemaphoreType.DMA],
    )
    return pl.pallas_call(
        functools.partial(_dispatch_kernel, nt=nt, n_blocks=n_blocks),
        grid_spec=grid_spec,
        out_shape=jax.ShapeDtypeStruct((n_blocks * MOE_BM * SLAB, 128), F32),
        compiler_params=_cparams(),
        name="moe_dispatch",
    )(pos, counts, padded, pad_start, n_used, f_slab)


def _moe_kernel(be_ref, nu_ref, xs_ref, wg_ref, wu_ref, wd_ref, y_ref, hbuf):
    @pl.when(pl.program_id(0) >= nu_ref[0])
    def _():
        y_ref[...] = jnp.zeros(y_ref.shape, F32)

    @pl.when(pl.program_id(0) < nu_ref[0])
    def _():
        x = _slab_to_rows(xs_ref, MOE_BM).astype(BF16)
        for f0, fw in MOE_FCHUNKS:
            hg = _dot(x, wg_ref[0, :, f0:f0 + fw])
            hu = _dot(x, wu_ref[0, :, f0:f0 + fw])
            hbuf[:, f0:f0 + fw] = (hg * jax.nn.sigmoid(hg) * hu).astype(BF16)
        for c0 in range(0, D_MODEL, MOE_YC):
            _rows_to_slab(y_ref, _dot(hbuf[...], wd_ref[0, :, c0:c0 + MOE_YC]), MOE_BM, c0 // 128)


def moe_experts(x_sorted, block_e, n_used, wg, wu, wd):
    n_blocks = block_e.shape[0]
    d, fexp = wg.shape[1], wg.shape[2]
    blk = lambda b, be, nu: (jnp.minimum(b, nu[0] - 1), 0)
    wsel = lambda b, be, nu: (be[jnp.minimum(b, nu[0] - 1)], 0, 0)
    grid_spec = pltpu.PrefetchScalarGridSpec(
        num_scalar_prefetch=2,
        grid=(n_blocks,),
        in_specs=[pl.BlockSpec((MOE_BM * SLAB, 128), blk),
                  pl.BlockSpec((1, d, fexp), wsel), pl.BlockSpec((1, d, fexp), wsel),
                  pl.BlockSpec((1, fexp, d), wsel)],
        out_specs=pl.BlockSpec((MOE_BM * SLAB, 128), lambda b, be, nu: (b, 0)),
        scratch_shapes=[pltpu.VMEM((MOE_BM, fexp), BF16)],
    )
    return pl.pallas_call(
        _moe_kernel,
        grid_spec=grid_spec,
        out_shape=jax.ShapeDtypeStruct(x_sorted.shape, F32),
        compiler_params=_cparams(),
        name="moe_experts",
    )(block_e, n_used, x_sorted, wg, wu, wd)


def moe_layout(eidx, rank, cnt):
    nt = eidx.shape[1]
    n_blocks = -(-(nt * TOP_K) // MOE_BM) + N_EXPERTS
    counts = cnt[:, 0].astype(jnp.int32)
    padded = (counts + MOE_BM - 1) // MOE_BM * MOE_BM
    pad_end = jnp.cumsum(padded)
    pad_start = pad_end - padded
    hot = eidx[:TOP_K, :, None] == jnp.arange(N_EXPERTS, dtype=jnp.int32)
    pos = (rank[:TOP_K] + jnp.sum(jnp.where(hot, pad_start, 0), axis=-1)).reshape(-1)
    blk_start = jnp.arange(n_blocks, dtype=jnp.int32) * MOE_BM
    block_e = jnp.minimum(jnp.sum(blk_start[:, None] >= pad_end[None, :], axis=1), N_EXPERTS - 1).astype(jnp.int32)
    n_used = (pad_end[-1:] // MOE_BM).astype(jnp.int32)
    return pos, counts, padded, pad_start, block_e, n_used


def _final_kernel(pos_ref, x_ref, y_hbm, wt_ref, mod_ref, g_ref, out_ref, gbuf, sem, *, nt, tile0):
    x = x_ref[...] + mod_ref[0][5:6] * _moe_combine(pos_ref, y_hbm, wt_ref, gbuf, sem, nt, tile0)
    ms = jnp.mean(x * x, axis=-1, keepdims=True)
    out_ref[...] = x * lax.rsqrt(ms + EPS) * g_ref[...]


def final_norm(x, moe, mods_l, g_final, n_ctx_blocks):
    nt, d = x.shape
    nlat = nt // TM - n_ctx_blocks
    y_sorted, pos, wts_t = moe
    grid_spec = pltpu.PrefetchScalarGridSpec(
        num_scalar_prefetch=1,
        grid=(nlat,),
        in_specs=[pl.BlockSpec((TM, d), lambda i, *_: (i + n_ctx_blocks, 0)),
                  pl.BlockSpec(memory_space=pl.ANY),
                  pl.BlockSpec((TM, 8), lambda i, *_: (i + n_ctx_blocks, 0)),
                  pl.BlockSpec((1, 6, d), lambda i, *_: (0, 0, 0)),
                  pl.BlockSpec((1, d), lambda i, *_: (0, 0))],
        out_specs=pl.BlockSpec((TM, d), lambda i, *_: (i, 0)),
        scratch_shapes=_COMBINE_SCRATCH,
    )
    return pl.pallas_call(
        functools.partial(_final_kernel, nt=nt, tile0=n_ctx_blocks),
        grid_spec=grid_spec,
        out_shape=jax.ShapeDtypeStruct((nlat * TM, d), F32),
        compiler_params=_cparams(),
        name="final_norm",
    )(pos, x, y_sorted, wts_t, mods_l, g_final.reshape(1, d))


def kernel(x, c, ctx, c_ctx, w_mod, b_mod, g_mix, g_ffn, w_in_even, w_out_even, sgu_ln_g, sgu_ln_b, sgu_w, sgu_b, conv_w, conv_b, conv_ln_g, conv_ln_b, w_in_odd, w_out_odd, attn_sink, ssm_lam_re, ssm_lam_im, ssm_log_dt, ssm_b_re, ssm_b_im, ssm_c_re, ssm_c_im, ssm_d, glu_w, glu_b, w_router, b_router, w_gate, w_up, w_down, g_final):
    bsz, n_lat, d = x.shape
    n_ctx = ctx.shape[1]
    assert bsz == 1 and d == D_MODEL and n_ctx % TM == 0 and n_lat % TM == 0
    nt = n_ctx + n_lat
    ncb = n_ctx // TM
    n_chunks = nt // S_CHUNK
    ncp = -(-n_chunks // 16) * 16

    xs = jnp.concatenate([ctx[0], x[0]], axis=0)
    cond8 = jnp.concatenate([c, c_ctx[None, :], jnp.zeros((6, d), F32)], axis=0)
    mods = adaln_all(cond8, w_mod, b_mod)[:, :2].reshape(DEPTH, 2, 6, d)
    cos_t, sin_t = rope_tables(n_ctx, n_lat)
    w_router_t = w_router.T
    b_router_col = jnp.broadcast_to(b_router.astype(F32)[:, None], (N_EXPERTS, 128))

    moe = None
    for l in range(DEPTH):
        j = l // 2
        odd = l % 2 == 1
        if odd:
            wi = w_in_odd[j]
            w_in = jnp.concatenate([wi[:, :C_WIDTH], wi[:, C_WIDTH + 2 * KV_WIDTH:],
                                    wi[:, C_WIDTH:C_WIDTH + 2 * KV_WIDTH]], axis=1).astype(BF16)
            w_out = w_out_odd[j].astype(BF16)
        else:
            w_in = w_in_even[j].astype(BF16)
            w_out = w_out_even[j].astype(BF16)
        x_new, p = inproj(xs, g_mix[l], mods[l], w_in, ncb, moe, mods[l - 1] if l > 0 else None)
        if x_new is not None:
            xs = x_new
        if odd:
            ya = attention(p, attn_sink[j], cos_t, sin_t, n_ctx)
            t_m, w_s, w_o, lam_rows = s5_weights(ssm_lam_re[j], ssm_lam_im[j], ssm_log_dt[j], ssm_b_re[j],
                                                 ssm_b_im[j], ssm_c_re[j], ssm_c_im[j])
            u = p[:, C_WIDTH:C_WIDTH + S_WIDTH].reshape(n_chunks, S_CHUNK, S_GROUPS, S_GROUP_CH)
            u_g = jnp.transpose(u, (2, 0, 1, 3)).reshape(S_GROUPS, n_chunks, S_LH)
            u_g = jnp.pad(u_g, ((0, 0), (0, ncp - n_chunks), (0, 0)))
            y_g = s5_scan(u_g, t_m, w_s, w_o, lam_rows, n_chunks, n_ctx // S_CHUNK)
            y_ssm = jnp.transpose(y_g[:, :n_chunks].reshape(S_GROUPS, n_chunks, S_CHUNK, S_GROUP_CH),
                                  (1, 2, 0, 3)).reshape(nt, S_WIDTH)
            yb = s5_glu(y_ssm, p, ssm_d[j], glu_w[j].astype(BF16), glu_b[j])
        else:
            bs_full = jnp.broadcast_to(sgu_b[j][:, :, None], (A_GROUPS, CHUNK, CHUNK)).astype(F32)
            ya, yb = even_mixer(p, sgu_ln_g[j], sgu_ln_b[j], sgu_w[j].astype(BF16), bs_full,
                                conv_w[j], conv_b[j], conv_ln_g[j], conv_ln_b[j], ncb)
        xs, f_slab, eidx, wts, rank, cnt = outproj(ya, yb, w_out[:1024], w_out[1024:], xs, mods[l], g_ffn[l],
                                                   w_router_t, b_router_col, ncb)
        pos, counts, padded, pad_start, block_e, n_used = moe_layout(eidx, rank, cnt)
        x_sorted = moe_dispatch(f_slab, pos, counts, padded, pad_start, n_used, block_e.shape[0])
        y_sorted = moe_experts(x_sorted, block_e, n_used, w_gate[l].astype(BF16), w_up[l].astype(BF16),
                               w_down[l].astype(BF16))
        moe = (y_sorted, pos, wts.T)
    out = final_norm(xs, moe, mods[DEPTH - 1], g_final, ncb)
    return out.reshape(bsz, n_lat, d)
```

```python
import functools
import math

import jax
import jax.numpy as jnp
from jax import lax
from jax.experimental import pallas as pl
from jax.experimental.pallas import tpu as pltpu

F32 = jnp.float32
BF16 = jnp.bfloat16

D_MODEL = 2048
DEPTH = 4
GRID_W = 64
EPS = 1e-6
NEG_INF = -1e30

A_WIDTH = 1024
A_GROUPS = 8
CHUNK = 128
B_WIDTH = 1024
CONV_WIDTH = 31
CONV_HALO = 16

HEAD_DIM = 128
N_Q_HEADS = 8
N_KV_HEADS = 2
Q_PER_KV = 4
C_WIDTH = 1024
KV_WIDTH = 256
WINDOW = 128
ATTN_BLOCK = 128
ROPE_BASE = 10000.0
ATTN_SCALE = HEAD_DIM ** -0.5
S_WIDTH = 1024
S_GROUP_CH = 16
S_GROUPS = 64
S_STATE = 64
S_CHUNK = 32
S_GROUP_BATCH = 8
ODD_IN = C_WIDTH + 2 * KV_WIDTH + S_WIDTH

N_EXPERTS = 16
N_EXPERT_GROUPS = 4
EXPERTS_PER_GROUP = 4
TOP_K = 2
D_EXPERT = 1408
MOE_BM = 256

TM = 256
VMEM_LIMIT = 56 * 1024 * 1024


def _cparams(n_axes=1, vmem=VMEM_LIMIT):
    return pltpu.CompilerParams(dimension_semantics=("arbitrary",) * n_axes, vmem_limit_bytes=vmem)


def _dot(a, b):
    return jnp.dot(a, b, preferred_element_type=F32)


def _dot_nt(a, b):
    return lax.dot_general(a, b, (((1,), (1,)), ((), ())), preferred_element_type=F32)


ADALN_TN = 1024


def _adaln_kernel(cond_ref, w_ref, b_ref, o_ref):
    c = cond_ref[...]
    s = (c * jax.nn.sigmoid(c)).astype(BF16)
    o_ref[0] = _dot(s, w_ref[0].astype(BF16)) + b_ref[0]


def adaln_all(cond8, w_mod, b_mod):
    depth, d, n6 = w_mod.shape
    return pl.pallas_call(
        _adaln_kernel,
        grid=(depth, n6 // ADALN_TN),
        in_specs=[
            pl.BlockSpec((8, d), lambda l, j: (0, 0)),
            pl.BlockSpec((1, d, ADALN_TN), lambda l, j: (l, 0, j)),
            pl.BlockSpec((1, 1, ADALN_TN), lambda l, j: (l, 0, j)),
        ],
        out_specs=pl.BlockSpec((1, 8, ADALN_TN), lambda l, j: (l, 0, j)),
        out_shape=jax.ShapeDtypeStruct((depth, 8, n6), F32),
        compiler_params=_cparams(2),
        name="adaln",
    )(cond8, w_mod, b_mod.reshape(depth, 1, n6))


def _mod_spec(n_ctx_blocks):
    return pl.BlockSpec((1, 6, D_MODEL), lambda i, *_: (jnp.where(i < n_ctx_blocks, 1, 0), 0, 0))


def _rms_mod(x, g, shift, scale):
    ms = jnp.mean(x * x, axis=-1, keepdims=True)
    y = x * lax.rsqrt(ms + EPS) * g
    return y * (1.0 + scale) + shift


SLAB = D_MODEL // 128


def _slab_to_rows(ref, n_rows):
    return jnp.concatenate([ref[pl.ds(c, n_rows, stride=SLAB), :] for c in range(SLAB)], axis=1)


def _rows_to_slab(ref, val, n_rows, c0):
    for c in range(val.shape[1] // 128):
        ref[pl.ds(c0 + c, n_rows, stride=SLAB), :] = val[:, c * 128:(c + 1) * 128]


def _expert_row_gather(pos_ref, y_hbm, gbuf, sem, tile, slot, nt, start):
    def body(t, carry):
        for k in range(TOP_K):
            p = pl.multiple_of(pos_ref[k * nt + tile * TM + t] * SLAB, SLAB)
            cp = pltpu.make_async_copy(y_hbm.at[pl.ds(p, SLAB), :],
                                       gbuf.at[slot, k, pl.ds(pl.multiple_of(t * SLAB, SLAB), SLAB), :],
                                       sem.at[slot])
            if start:
                cp.start()
            else:
                cp.wait()
        return carry
    lax.fori_loop(0, TM, body, 0, unroll=8)


def _moe_combine(pos_ref, y_hbm, wt_ref, gbuf, sem, nt, tile0):
    i = pl.program_id(0)
    slot = i % 2

    @pl.when(i == 0)
    def _():
        _expert_row_gather(pos_ref, y_hbm, gbuf, sem, tile0, 0, nt, True)

    @pl.when(i + 1 < pl.num_programs(0))
    def _():
        _expert_row_gather(pos_ref, y_hbm, gbuf, sem, tile0 + i + 1, 1 - slot, nt, True)

    _expert_row_gather(pos_ref, y_hbm, gbuf, sem, tile0 + i, slot, nt, False)
    wt = wt_ref[...]
    return (wt[:, 0:1] * _slab_to_rows(gbuf.at[slot, 0], TM) + wt[:, 1:2] * _slab_to_rows(gbuf.at[slot, 1], TM))


_COMBINE_SCRATCH = [pltpu.VMEM((2, TOP_K, TM * SLAB, 128), F32), pltpu.SemaphoreType.DMA((2,))]


INPROJ_NC = 512


def _inproj_kernel(*refs, combine, nt):
    if combine:
        pos_ref, x_ref, y_hbm, wt_ref, modp_ref, g_ref, mod_ref, w_ref, xo_ref, p_ref, gbuf, sem = refs
        x = x_ref[...] + modp_ref[0][5:6] * _moe_combine(pos_ref, y_hbm, wt_ref, gbuf, sem, nt, 0)
        xo_ref[...] = x
    else:
        x_ref, g_ref, mod_ref, w_ref, p_ref = refs
        x = x_ref[...]
    m = mod_ref[0]
    h = _rms_mod(x, g_ref[...], m[0:1], m[1:2]).astype(BF16)
    n = w_ref.shape[1]
    for j in range(0, n, INPROJ_NC):
        p_ref[:, j:j + INPROJ_NC] = _dot(h, w_ref[:, j:j + INPROJ_NC])


def inproj(x, g, mods_l, w_bf16, n_ctx_blocks, moe=None, mods_prev=None):
    nt, d = x.shape
    n = w_bf16.shape[1]
    nblk = nt // TM
    row = pl.BlockSpec((TM, d), lambda i, *_: (i, 0))
    g_spec = pl.BlockSpec((1, d), lambda i, *_: (0, 0))
    w_spec = pl.BlockSpec((d, n), lambda i, *_: (0, 0), pipeline_mode=pl.Buffered(1))
    p_spec = pl.BlockSpec((TM, n), lambda i, *_: (i, 0))
    p_shape = jax.ShapeDtypeStruct((nt, n), F32)
    if moe is None:
        return None, pl.pallas_call(
            functools.partial(_inproj_kernel, combine=False, nt=nt),
            grid=(nblk,),
            in_specs=[row, g_spec, _mod_spec(n_ctx_blocks), w_spec],
            out_specs=p_spec,
            out_shape=p_shape,
            compiler_params=_cparams(),
            name="inproj",
        )(x, g.reshape(1, d), mods_l, w_bf16)
    y_sorted, pos, wts_t = moe
    grid_spec = pltpu.PrefetchScalarGridSpec(
        num_scalar_prefetch=1,
        grid=(nblk,),
        in_specs=[row, pl.BlockSpec(memory_space=pl.ANY), pl.BlockSpec((TM, 8), lambda i, *_: (i, 0)),
                  _mod_spec(n_ctx_blocks), g_spec, _mod_spec(n_ctx_blocks), w_spec],
        out_specs=[row, p_spec],
        scratch_shapes=_COMBINE_SCRATCH,
    )
    return pl.pallas_call(
        functools.partial(_inproj_kernel, combine=True, nt=nt),
        grid_spec=grid_spec,
        out_shape=[jax.ShapeDtypeStruct((nt, d), F32), p_shape],
        compiler_params=_cparams(),
        name="combine_inproj",
    )(pos, x, y_sorted, wts_t, mods_prev, g.reshape(1, d), mods_l, w_bf16)


CONV_RC = 64


def _layer_norm(x, g, b):
    mu = jnp.mean(x, axis=-1, keepdims=True)
    xc = x - mu
    var = jnp.mean(xc * xc, axis=-1, keepdims=True)
    return xc * lax.rsqrt(var + EPS) * g + b


def _even_kernel(u_ref, v_ref, a_ref, g_ref, ap_ref, gp_ref, an_ref, gn_ref,
                 lng_ref, lnb_ref, ws_ref, bs_ref, cw_ref, cb_ref, clg_ref, clb_ref,
                 ya_ref, yb_ref, hpad_ref, cacc_ref, *, n_ctx_blocks, n_blocks):
    i = pl.program_id(0)
    for c in range(TM // CHUNK):
        rows = slice(c * CHUNK, (c + 1) * CHUNK)
        vn = _layer_norm(jax.nn.gelu(v_ref[rows, :]), lng_ref[...], lnb_ref[...]).astype(BF16)
        for grp in range(A_GROUPS):
            cols = slice(grp * CHUNK, (grp + 1) * CHUNK)
            mixed = _dot(ws_ref[grp], vn[:, cols]) + bs_ref[grp]
            ya_ref[rows, cols] = (jax.nn.gelu(u_ref[rows, cols]) * mixed).astype(BF16)
    first = jnp.logical_or(i == 0, i == n_ctx_blocks)
    last = jnp.logical_or(i == n_ctx_blocks - 1, i == n_blocks - 1)
    hpad_ref[0:CONV_HALO, :] = jnp.where(first, 0.0, ap_ref[...] * jax.nn.sigmoid(gp_ref[...]))
    hpad_ref[CONV_HALO:CONV_HALO + TM, :] = a_ref[...] * jax.nn.sigmoid(g_ref[...])
    hpad_ref[CONV_HALO + TM:, :] = jnp.where(last, 0.0, an_ref[...] * jax.nn.sigmoid(gn_ref[...]))
    off = CONV_HALO - CONV_WIDTH // 2
    for cc in range(B_WIDTH // 128):
        cols = slice(cc * 128, (cc + 1) * 128)
        for rc in range(TM // CONV_RC):
            acc = jnp.zeros((CONV_RC, 128), F32)
            for k in range(CONV_WIDTH):
                r0 = rc * CONV_RC + k + off
                acc = acc + cw_ref[k:k + 1, cols] * hpad_ref[r0:r0 + CONV_RC, cols]
            cacc_ref[rc * CONV_RC:(rc + 1) * CONV_RC, cols] = acc
    hc = _layer_norm(cacc_ref[...] + cb_ref[...], clg_ref[...], clb_ref[...])
    yb_ref[...] = (hc * jax.nn.sigmoid(hc)).astype(BF16)


def even_mixer(p, ln_g, ln_b, ws_bf16, bs_full, conv_w, conv_b, cln_g, cln_b, n_ctx_blocks):
    nt = p.shape[0]
    nblk = nt // TM
    hb = TM // CONV_HALO
    last_h = nt // CONV_HALO - 1
    col = lambda j: pl.BlockSpec((TM, 1024), lambda i: (i, j))
    prev = lambda j: pl.BlockSpec((CONV_HALO, 1024), lambda i: (jnp.maximum(i * hb - 1, 0), j))
    nxt = lambda j: pl.BlockSpec((CONV_HALO, 1024), lambda i: (jnp.minimum((i + 1) * hb, last_h), j))
    vec = pl.BlockSpec((1, 1024), lambda i: (0, 0))
    out = pl.BlockSpec((TM, 1024), lambda i: (i, 0))
    return pl.pallas_call(
        functools.partial(_even_kernel, n_ctx_blocks=n_ctx_blocks, n_blocks=nblk),
        grid=(nblk,),
        in_specs=[col(0), col(1), col(2), col(3), prev(2), prev(3), nxt(2), nxt(3),
                  vec, vec,
                  pl.BlockSpec((A_GROUPS, CHUNK, CHUNK), lambda i: (0, 0, 0)),
                  pl.BlockSpec((A_GROUPS, CHUNK, CHUNK), lambda i: (0, 0, 0)),
                  pl.BlockSpec((CONV_WIDTH, 1024), lambda i: (0, 0)),
                  vec, vec, vec],
        out_specs=[out, out],
        out_shape=[jax.ShapeDtypeStruct((nt, 1024), BF16)] * 2,
        scratch_shapes=[pltpu.VMEM((TM + 2 * CONV_HALO, 1024), F32), pltpu.VMEM((TM, 1024), F32)],
        compiler_params=_cparams(),
        name="even_mixer",
    )(p, p, p, p, p, p, p, p, ln_g.reshape(1, -1), ln_b.reshape(1, -1), ws_bf16, bs_full,
      conv_w, conv_b.reshape(1, -1), cln_g.reshape(1, -1), cln_b.reshape(1, -1))


def _second_max(a0, a1, a2, a3):
    m01, n01 = jnp.maximum(a0, a1), jnp.minimum(a0, a1)
    m23, n23 = jnp.maximum(a2, a3), jnp.minimum(a2, a3)
    return jnp.maximum(m01, m23), jnp.maximum(jnp.minimum(m01, m23), jnp.maximum(n01, n23))


def _route_tile(logits, b_col):
    aff = jax.nn.sigmoid(logits)
    biased = aff + b_col
    row = lambda m, e: m[e:e + 1, :]
    g_sel = best = None
    for g in range(N_EXPERT_GROUPS):
        top1, top2 = _second_max(*[row(biased, EXPERTS_PER_GROUP * g + j) for j in range(EXPERTS_PER_GROUP)])
        score = top1 + top2
        if g == 0:
            g_sel, best = jnp.zeros(score.shape, jnp.int32), score
        else:
            upd = score > best
            g_sel, best = jnp.where(upd, g, g_sel), jnp.where(upd, score, best)

    def in_group(m, j):
        out = row(m, (N_EXPERT_GROUPS - 1) * EXPERTS_PER_GROUP + j)
        for g in range(N_EXPERT_GROUPS - 2, -1, -1):
            out = jnp.where(g_sel == g, row(m, EXPERTS_PER_GROUP * g + j), out)
        return out

    v = [in_group(biased, j) for j in range(EXPERTS_PER_GROUP)]
    a = [in_group(aff, j) for j in range(EXPERTS_PER_GROUP)]
    i1, b1, w1 = jnp.zeros(g_sel.shape, jnp.int32), v[0], a[0]
    for j in range(1, EXPERTS_PER_GROUP):
        upd = v[j] > b1
        i1, b1, w1 = jnp.where(upd, j, i1), jnp.where(upd, v[j], b1), jnp.where(upd, a[j], w1)
    first = i1 == 0
    i2, b2, w2 = jnp.where(first, 1, 0), jnp.where(first, v[1], v[0]), jnp.where(first, a[1], a[0])
    for j in range(1, EXPERTS_PER_GROUP):
        upd = jnp.logical_and(i1 != j, v[j] > b2)
        i2, b2, w2 = jnp.where(upd, j, i2), jnp.where(upd, v[j], b2), jnp.where(upd, a[j], w2)
    den = w1 + w2
    return EXPERTS_PER_GROUP * g_sel + i1, EXPERTS_PER_GROUP * g_sel + i2, w1 / den, w2 / den


def _rows8(r0, r1):
    sub = lax.broadcasted_iota(jnp.int32, (8, r0.shape[1]), 0)
    return jnp.where(sub == 0, r0, jnp.where(sub == 1, r1, jnp.zeros_like(r0)))


def _outproj_kernel(ya_ref, yb_ref, wa_ref, wb_ref, x_ref, mod_ref, g_ref, wr_ref, br_ref,
                    xo_ref, f_ref, e_ref, w_ref, r_ref, cnt_ref, carry_ref):
    i = pl.program_id(0)
    m = mod_ref[0]
    y = _dot(ya_ref[...], wa_ref[...]) + _dot(yb_ref[...], wb_ref[...])
    x = x_ref[...] + m[2:3] * y
    xo_ref[...] = x
    f = _rms_mod(x, g_ref[...], m[3:4], m[4:5])
    _rows_to_slab(f_ref, f, TM, 0)
    f_hi = f.astype(BF16)
    f_lo = (f - f_hi.astype(F32)).astype(BF16)
    wr = wr_ref[...]
    w_hi = wr.astype(BF16)
    w_lo = (wr - w_hi.astype(F32)).astype(BF16)
    logits = _dot_nt(w_hi, f_hi) + _dot_nt(w_lo, f_hi) + _dot_nt(w_hi, f_lo)
    e0, e1, w0, w1 = _route_tile(logits, br_ref[:, 0:1])

    @pl.when(i == 0)
    def _():
        carry_ref[...] = jnp.zeros(carry_ref.shape, F32)

    sub = lax.broadcasted_iota(jnp.int32, (N_EXPERTS, TM), 0)
    hot0, hot1 = sub == e0, sub == e1
    member = jnp.where(jnp.logical_or(hot0, hot1), 1.0, 0.0)
    tri = jnp.where(lax.broadcasted_iota(jnp.int32, (TM, TM), 0) < lax.broadcasted_iota(jnp.int32, (TM, TM), 1),
                    1.0, 0.0).astype(BF16)
    before = _dot(member.astype(BF16), tri) + carry_ref[:, 0:1]
    r0 = jnp.sum(jnp.where(hot0, before, 0.0), axis=0, keepdims=True)
    r1 = jnp.sum(jnp.where(hot1, before, 0.0), axis=0, keepdims=True)
    carry_ref[...] = carry_ref[...] + jnp.sum(member, axis=1, keepdims=True)
    e_ref[...] = _rows8(e0, e1)
    w_ref[...] = _rows8(w0, w1)
    r_ref[...] = _rows8(r0.astype(jnp.int32), r1.astype(jnp.int32))
    cnt_ref[...] = carry_ref[...]


def outproj(ya, yb, wa, wb, x, mods_l, g_ffn, w_router_t, b_router_col, n_ctx_blocks):
    nt, d = x.shape
    nblk = nt // TM
    half = pl.BlockSpec((TM, 1024), lambda i: (i, 0))
    wsp = pl.BlockSpec((1024, d), lambda i: (0, 0), pipeline_mode=pl.Buffered(1))
    row = pl.BlockSpec((TM, d), lambda i: (i, 0))
    r8 = pl.BlockSpec((8, TM), lambda i: (0, i))
    cnt = pl.BlockSpec((N_EXPERTS, 128), lambda i: (0, 0))
    return pl.pallas_call(
        _outproj_kernel,
        grid=(nblk,),
        in_specs=[half, half, wsp, wsp, row, _mod_spec(n_ctx_blocks),
                  pl.BlockSpec((1, d), lambda i: (0, 0)),
                  pl.BlockSpec((N_EXPERTS, d), lambda i: (0, 0)), cnt],
        out_specs=[row, pl.BlockSpec((TM * SLAB, 128), lambda i: (i, 0)), r8, r8, r8, cnt],
        out_shape=[jax.ShapeDtypeStruct((nt, d), F32), jax.ShapeDtypeStruct((nt * SLAB, 128), F32),
                   jax.ShapeDtypeStruct((8, nt), jnp.int32), jax.ShapeDtypeStruct((8, nt), F32),
                   jax.ShapeDtypeStruct((8, nt), jnp.int32), jax.ShapeDtypeStruct((N_EXPERTS, 128), F32)],
        scratch_shapes=[pltpu.VMEM((N_EXPERTS, 128), F32)],
        compiler_params=_cparams(),
        name="outproj",
    )(ya, yb, wa, wb, x, mods_l, g_ffn.reshape(1, d), w_router_t, b_router_col)


N_LOC = 3 * ATTN_BLOCK


def _rope(x, cos, sin):
    lane = lax.broadcasted_iota(jnp.int32, x.shape, 1)
    swapped = jnp.where(lane % 64 < 32, pltpu.roll(x, 96, axis=1), pltpu.roll(x, 32, axis=1))
    return x * cos + swapped * sin


def _attn_kernel(sink_ref, q_ref, kp_ref, kc_ref, kn_ref, vp_ref, vc_ref, vn_ref, kx_ref, vx_ref,
                 cosp_ref, cosc_ref, cosn_ref, sinp_ref, sinc_ref, sinn_ref, o_ref,
                 qs_ref, kbuf_ref, vbuf_ref, *, n_ctx_blocks, n_blocks, n_ctx):
    i = pl.program_id(0)
    nkeys = N_LOC + n_ctx
    nq = Q_PER_KV * ATTN_BLOCK
    is_lat = jnp.where(i >= n_ctx_blocks, 1, 0)
    prev_ok = jnp.where(i - 1 >= n_ctx_blocks, is_lat, 0)
    next_ok = jnp.where(i + 1 <= n_blocks - 1, is_lat, 0)
    qi = lax.broadcasted_iota(jnp.int32, (nq, nkeys), 0) & (ATTN_BLOCK - 1)
    kj = lax.broadcasted_iota(jnp.int32, (nq, nkeys), 1)
    rel = kj - ATTN_BLOCK - qi
    blk_ok = jnp.where(kj < ATTN_BLOCK, prev_ok, jnp.where(kj < 2 * ATTN_BLOCK, is_lat, next_ok))
    rel = jnp.where(blk_ok > 0, rel, WINDOW + 1)
    valid = jnp.logical_or(kj >= N_LOC, jnp.logical_and(rel >= -WINDOW, rel <= WINDOW))
    rowh = lax.broadcasted_iota(jnp.int32, (nq, 1), 0) // ATTN_BLOCK
    for h in range(N_KV_HEADS):
        hc = slice(h * HEAD_DIM, (h + 1) * HEAD_DIM)
        kbuf_ref[0:ATTN_BLOCK, :] = _rope(kp_ref[:, hc], cosp_ref[...], sinp_ref[...]).astype(BF16)
        kbuf_ref[ATTN_BLOCK:2 * ATTN_BLOCK, :] = _rope(kc_ref[:, hc], cosc_ref[...], sinc_ref[...]).astype(BF16)
        kbuf_ref[2 * ATTN_BLOCK:N_LOC, :] = _rope(kn_ref[:, hc], cosn_ref[...], sinn_ref[...]).astype(BF16)
        kbuf_ref[N_LOC:, :] = kx_ref[:, hc].astype(BF16)
        vbuf_ref[0:ATTN_BLOCK, :] = vp_ref[:, hc].astype(BF16)
        vbuf_ref[ATTN_BLOCK:2 * ATTN_BLOCK, :] = vc_ref[:, hc].astype(BF16)
        vbuf_ref[2 * ATTN_BLOCK:N_LOC, :] = vn_ref[:, hc].astype(BF16)
        vbuf_ref[N_LOC:, :] = vx_ref[:, hc].astype(BF16)
        sink = jnp.zeros((Q_PER_KV * ATTN_BLOCK, 1), F32)
        for gq in range(Q_PER_KV):
            head = h * Q_PER_KV + gq
            qc = slice(head * HEAD_DIM, (head + 1) * HEAD_DIM)
            qs_ref[gq * ATTN_BLOCK:(gq + 1) * ATTN_BLOCK, :] = _rope(
                q_ref[:, qc], cosc_ref[...], sinc_ref[...]).astype(BF16)
            sink = jnp.where(rowh == gq, sink_ref[head], sink)
        s = _dot_nt(qs_ref[...], kbuf_ref[...]) * ATTN_SCALE
        s = jnp.where(valid, s, NEG_INF)
        mx = jnp.maximum(jnp.max(s, axis=-1, keepdims=True), sink)
        p = jnp.exp(s - mx)
        den = jnp.sum(p, axis=-1, keepdims=True) + jnp.exp(sink - mx)
        o = _dot(p.astype(BF16), vbuf_ref[...]) / den
        for gq in range(Q_PER_KV):
            head = h * Q_PER_KV + gq
            o_ref[:, head * HEAD_DIM:(head + 1) * HEAD_DIM] = o[gq * ATTN_BLOCK:(gq + 1) * ATTN_BLOCK].astype(BF16)


def attention(p, sink, cos_t, sin_t, n_ctx):
    nt = p.shape[0]
    nblk = nt // ATTN_BLOCK
    ncb = n_ctx // ATTN_BLOCK
    kcol, vcol = 2048 // KV_WIDTH, 2048 // KV_WIDTH + 1
    pm = lambda i: jnp.maximum(i - 1, 0)
    nx = lambda i: jnp.minimum(i + 1, nblk - 1)
    kv = lambda f, c: pl.BlockSpec((ATTN_BLOCK, KV_WIDTH), lambda i, s: (f(i), c))
    tab = lambda f: pl.BlockSpec((ATTN_BLOCK, HEAD_DIM), lambda i, s: (f(i), 0))
    same = lambda i: i
    grid_spec = pltpu.PrefetchScalarGridSpec(
        num_scalar_prefetch=1,
        grid=(nblk,),
        in_specs=[pl.BlockSpec((ATTN_BLOCK, C_WIDTH), lambda i, s: (i, 0)),
                  kv(pm, kcol), kv(same, kcol), kv(nx, kcol), kv(pm, vcol), kv(same, vcol), kv(nx, vcol),
                  pl.BlockSpec((n_ctx, KV_WIDTH), lambda i, s: (0, kcol)),
                  pl.BlockSpec((n_ctx, KV_WIDTH), lambda i, s: (0, vcol)),
                  tab(pm), tab(same), tab(nx), tab(pm), tab(same), tab(nx)],
        out_specs=pl.BlockSpec((ATTN_BLOCK, C_WIDTH), lambda i, s: (i, 0)),
        scratch_shapes=[pltpu.VMEM((Q_PER_KV * ATTN_BLOCK, HEAD_DIM), BF16),
                        pltpu.VMEM((N_LOC + n_ctx, HEAD_DIM), BF16),
                        pltpu.VMEM((N_LOC + n_ctx, HEAD_DIM), BF16)],
    )
    return pl.pallas_call(
        functools.partial(_attn_kernel, n_ctx_blocks=ncb, n_blocks=nblk, n_ctx=n_ctx),
        grid_spec=grid_spec,
        out_shape=jax.ShapeDtypeStruct((nt, C_WIDTH), BF16),
        compiler_params=_cparams(),
        name="attention",
    )(sink, p, p, p, p, p, p, p, p, p, cos_t, cos_t, cos_t, sin_t, sin_t, sin_t)


def rope_tables(n_ctx, n_lat):
    rows = n_lat // GRID_W
    row = jnp.broadcast_to(jnp.arange(rows, dtype=F32)[:, None], (rows, GRID_W)).reshape(-1)
    col = jnp.broadcast_to(jnp.arange(GRID_W, dtype=F32)[None, :], (rows, GRID_W)).reshape(-1)
    half = HEAD_DIM // 2
    inv_freq = ROPE_BASE ** (-jnp.arange(0, half, 2, dtype=F32) / half)
    ang_r = row[:, None] * inv_freq
    ang_c = col[:, None] * inv_freq
    cos = jnp.concatenate([jnp.cos(ang_r), jnp.cos(ang_r), jnp.cos(ang_c), jnp.cos(ang_c)], axis=-1)
    sin = jnp.concatenate([-jnp.sin(ang_r), jnp.sin(ang_r), -jnp.sin(ang_c), jnp.sin(ang_c)], axis=-1)
    cos = jnp.concatenate([jnp.ones((n_ctx, HEAD_DIM), F32), cos], axis=0)
    sin = jnp.concatenate([jnp.zeros((n_ctx, HEAD_DIM), F32), sin], axis=0)
    return cos, sin


S_LH = S_CHUNK * S_GROUP_CH
S_HALF = S_GROUP_BATCH * 128


S_PW_ROWS = 4 * S_CHUNK


def _dot3_nt(x, a):
    x_hi, a_hi = x.astype(BF16), a.astype(BF16)
    x_lo, a_lo = (x - x_hi.astype(F32)).astype(BF16), (a - a_hi.astype(F32)).astype(BF16)
    return _dot_nt(x_hi, a_hi) + _dot_nt(x_lo, a_hi) + _dot_nt(x_hi, a_lo)


def _s5_kernel(u_ref, pw_ref, bc_ref, lam_ref, y_ref, s_ref, hf_ref, hr_ref,
               ws_ref, wo_ref, are_ref, aim_ref, t_ref, ug_ref, yg_ref, *, n_chunks, n_ctx_chunks):
    L = S_CHUNK
    gpt = 128 // S_GROUP_CH
    ug_ref[:, n_chunks:, :] = jnp.zeros((S_GROUP_BATCH, ug_ref.shape[1] - n_chunks, S_LH), F32)
    for s_tok in range(L):
        x = u_ref[pl.ds(s_tok, n_chunks, stride=L), :]
        dst = s_tok % gpt
        for g in range(S_GROUP_BATCH):
            k = (dst - g) % gpt
            r = x if k == 0 else pltpu.roll(x, k * S_GROUP_CH, axis=1)
            ug_ref[g, 0:n_chunks, s_tok * S_GROUP_CH:(s_tok + 1) * S_GROUP_CH] = (
                r[:, dst * S_GROUP_CH:(dst + 1) * S_GROUP_CH])

    def scaled_rows(g, x_re, x_im, row0, n_rows, emit):
        def body(r, carry):
            p_re = pw_ref[g, 0, pl.ds(row0 + r, 1), :]
            p_im = pw_ref[g, 1, pl.ds(row0 + r, 1), :]
            emit(pl.ds(pl.multiple_of(r * S_GROUP_CH, S_GROUP_CH), S_GROUP_CH),
                 x_re * p_re - x_im * p_im, x_re * p_im + x_im * p_re)
            return carry
        lax.fori_loop(0, n_rows, body, 0, unroll=4)

    def emit_ws(rows, re, im):
        ws_ref[rows, 0:128] = re.astype(BF16)
        ws_ref[rows, 128:256] = im.astype(BF16)

    def emit_wo(rows, re, im):
        wo_ref[rows, 0:128] = re.astype(BF16)
        wo_ref[rows, 128:256] = (-im).astype(BF16)

    def emit_a(rows, re, im):
        are_ref[rows, :] = re
        aim_ref[rows, :] = im

    for g in range(S_GROUP_BATCH):
        scaled_rows(g, bc_ref[g, 0], bc_ref[g, 1], 0, L, emit_ws)
        s = _dot(ug_ref[g].astype(BF16), ws_ref[...])
        s_ref[:, g * 128:(g + 1) * 128] = s[:, 0:128]
        s_ref[:, S_HALF + g * 128:S_HALF + (g + 1) * 128] = s[:, 128:256]
    hf_ref[...] = jnp.zeros(hf_ref.shape, F32)
    hr_ref[...] = jnp.zeros(hr_ref.shape, F32)
    lam = lam_ref[0]
    lam_re, lam_im = lam[:, :S_HALF], lam[:, S_HALF:]
    is_fwd = lax.broadcasted_iota(jnp.int32, (1, 2 * S_HALF), 1) % 128 < S_STATE

    def step(k, state):
        st_re, st_im = state
        cf = k
        cr = jnp.where(k < n_ctx_chunks, n_ctx_chunks - 1 - k, n_chunks - 1 - (k - n_ctx_chunks))
        st = jnp.concatenate([st_re, st_im], axis=1)
        hf_ref[pl.ds(cf, 1), :] = st
        hr_ref[pl.ds(cr, 1), :] = st
        s_in = jnp.where(is_fwd, s_ref[pl.ds(cf, 1), :], s_ref[pl.ds(cr, 1), :])
        new_re = lam_re * st_re - lam_im * st_im + s_in[:, :S_HALF]
        new_im = lam_re * st_im + lam_im * st_re + s_in[:, S_HALF:]
        return new_re, new_im

    zero = jnp.zeros((1, S_HALF), F32)
    lax.fori_loop(0, n_chunks, step, (zero, zero))
    hin = jnp.where(is_fwd, hf_ref[...], hr_ref[...]).astype(BF16)
    kwidth = 2 * L * S_GROUP_CH
    for g in range(S_GROUP_BATCH):
        hin_g = jnp.concatenate([hin[:, g * 128:(g + 1) * 128],
                                 hin[:, S_HALF + g * 128:S_HALF + (g + 1) * 128]], axis=1)
        c_re, c_im = bc_ref[g, 2], bc_ref[g, 3]
        scaled_rows(g, c_re, c_im, L, L, emit_wo)
        scaled_rows(g, c_re, c_im, 2 * L, 2 * L, emit_a)
        kern = _dot3_nt(bc_ref[g, 0], are_ref[...]) - _dot3_nt(bc_ref[g, 1], aim_ref[...])
        for s_tok in range(L):
            off = (L - 1 - s_tok) * S_GROUP_CH
            win = kern if off == 0 else pltpu.roll(kern, kwidth - off, axis=1)
            t_ref[s_tok * S_GROUP_CH:(s_tok + 1) * S_GROUP_CH, :] = win[:, :S_LH].astype(BF16)
        yg_ref[g] = _dot(ug_ref[g].astype(BF16), t_ref[...]) + _dot_nt(hin_g, wo_ref[...])
    lane_group = lax.broadcasted_iota(jnp.int32, (n_chunks, 128), 1) // S_GROUP_CH
    for s_tok in range(L):
        dst = s_tok % gpt
        cols = slice((s_tok // gpt) * 128, (s_tok // gpt + 1) * 128)
        z = None
        for g in range(S_GROUP_BATCH):
            piece = yg_ref[g, 0:n_chunks, cols]
            k = (g - dst) % gpt
            r = piece if k == 0 else pltpu.roll(piece, k * S_GROUP_CH, axis=1)
            z = r if z is None else jnp.where(lane_group == g, r, z)
        y_ref[pl.ds(s_tok, n_chunks, stride=L), :] = z


def s5_scan(p, pw_tab, bc_tab, lam_rows, n_chunks, n_ctx_chunks):
    nt = p.shape[0]
    assert S_GROUP_BATCH * S_GROUP_CH == 128 and nt == n_chunks * S_CHUNK
    ncp = -(-n_chunks // 16) * 16
    lh = S_LH
    return pl.pallas_call(
        functools.partial(_s5_kernel, n_chunks=n_chunks, n_ctx_chunks=n_ctx_chunks),
        grid=(S_GROUPS // S_GROUP_BATCH,),
        in_specs=[pl.BlockSpec((nt, 128), lambda i: (0, C_WIDTH // 128 + i)),
                  pl.BlockSpec((S_GROUP_BATCH, 2, S_PW_ROWS, 128), lambda i: (i, 0, 0, 0)),
                  pl.BlockSpec((S_GROUP_BATCH, 4, S_GROUP_CH, 128), lambda i: (i, 0, 0, 0)),
                  pl.BlockSpec((1, 1, 2 * S_HALF), lambda i: (i, 0, 0))],
        out_specs=pl.BlockSpec((nt, 128), lambda i: (0, i)),
        out_shape=jax.ShapeDtypeStruct((nt, S_WIDTH), F32),
        scratch_shapes=[pltpu.VMEM((ncp, 2 * S_HALF), F32)] * 3 + [
            pltpu.VMEM((lh, 256), BF16), pltpu.VMEM((lh, 256), BF16),
            pltpu.VMEM((2 * lh, 128), F32), pltpu.VMEM((2 * lh, 128), F32), pltpu.VMEM((lh, lh), BF16),
            pltpu.VMEM((S_GROUP_BATCH, ncp, lh), F32), pltpu.VMEM((S_GROUP_BATCH, ncp, lh), F32)],
        compiler_params=_cparams(),
        name="s5_scan",
    )(p, pw_tab, bc_tab, lam_rows)


def s5_weights(lam_re, lam_im, log_dt, b_re, b_im, c_re, c_im):
    L = S_CHUNK
    lam = lax.complex(lam_re.astype(F32), lam_im.astype(F32))
    lam_dt = lam * jnp.exp(log_dt.astype(F32))[..., None]
    lam_bar = jnp.exp(lam_dt)
    b_bar = ((lam_bar - 1.0) / lam)[..., None] * lax.complex(b_re.astype(F32), b_im.astype(F32))
    c = lax.complex(c_re.astype(F32), c_im.astype(F32))
    n = jnp.arange(L, dtype=F32)
    lag = jnp.arange(2 * L, dtype=F32) - (L - 1)
    expo = jnp.concatenate([jnp.stack([L - 1 - n, n], axis=-1), jnp.stack([n + 1, L - n], axis=-1),
                            jnp.stack([lag, -lag], axis=-1)], axis=0)
    live = jnp.logical_and(expo >= 0, (jnp.arange(4 * L) < 4 * L - 1)[:, None])
    pw = jnp.where(live[:, :, None, None], jnp.exp(lam_dt[None] * jnp.maximum(expo, 0.0)[:, :, None, None]), 0.0)
    pw = jnp.transpose(pw, (2, 0, 1, 3)).reshape(S_GROUPS, S_PW_ROWS, 2 * S_STATE)
    pw_tab = jnp.stack([pw.real, pw.imag], axis=1)
    bt = jnp.transpose(b_bar, (1, 3, 0, 2)).reshape(S_GROUPS, S_GROUP_CH, 2 * S_STATE)
    ct = jnp.transpose(c, (1, 2, 0, 3)).reshape(S_GROUPS, S_GROUP_CH, 2 * S_STATE)
    bc_tab = jnp.stack([bt.real, bt.imag, ct.real, ct.imag], axis=1)
    lam_l = jnp.exp(lam_dt * L)
    nb = S_GROUPS // S_GROUP_BATCH
    lre = jnp.concatenate([lam_l[0].real, lam_l[1].real], axis=-1).reshape(nb, 1, S_HALF)
    lim = jnp.concatenate([lam_l[0].imag, lam_l[1].imag], axis=-1).reshape(nb, 1, S_HALF)
    return pw_tab, bc_tab, jnp.concatenate([lre, lim], axis=-1)


def _glu_kernel(y_ref, u_ref, d_ref, w_ref, b_ref, o_ref):
    z = jax.nn.gelu(y_ref[...] + d_ref[...] * u_ref[...])
    gate = jax.nn.sigmoid(_dot(z.astype(BF16), w_ref[...]) + b_ref[...])
    o_ref[...] = (z * gate).astype(BF16)


def s5_glu(y_ssm, p, d_skip, glu_w_bf16, glu_b):
    nt = y_ssm.shape[0]
    vec = pl.BlockSpec((1, S_WIDTH), lambda i: (0, 0))
    return pl.pallas_call(
        _glu_kernel,
        grid=(nt // TM,),
        in_specs=[pl.BlockSpec((TM, S_WIDTH), lambda i: (i, 0)),
                  pl.BlockSpec((TM, S_WIDTH), lambda i: (i, 1)),
                  vec, pl.BlockSpec((S_WIDTH, S_WIDTH), lambda i: (0, 0)), vec],
        out_specs=pl.BlockSpec((TM, S_WIDTH), lambda i: (i, 0)),
        out_shape=jax.ShapeDtypeStruct((nt, S_WIDTH), BF16),
        compiler_params=_cparams(),
        name="s5_glu",
    )(y_ssm, p, d_skip.reshape(1, -1), glu_w_bf16, glu_b.reshape(1, -1))


MOE_FCHUNKS = ((0, 512), (512, 512), (1024, 384))
MOE_YC = 512


def _dispatch_kernel(pos_ref, cnt_ref, pad_ref, start_ref, nu_ref, f_ref, xs_hbm, zero_ref, sem, zsem, *,
                     nt, n_blocks):
    i = pl.program_id(0)
    blk_rows = MOE_BM * SLAB

    def token_copies(t, start):
        src = f_ref.at[pl.ds(pl.multiple_of(t * SLAB, SLAB), SLAB), :]
        for k in range(TOP_K):
            p = pl.multiple_of(pos_ref[k * nt + i * TM + t] * SLAB, SLAB)
            cp = pltpu.make_async_copy(src, xs_hbm.at[pl.ds(p, SLAB), :], sem)
            if start:
                cp.start()
            else:
                cp.wait()

    def pad_copies(start):
        for e in range(N_EXPERTS):
            def body(r, carry):
                p = pl.multiple_of((start_ref[e] + r) * SLAB, SLAB)
                cp = pltpu.make_async_copy(zero_ref.at[pl.ds(0, SLAB), :], xs_hbm.at[pl.ds(p, SLAB), :], zsem)
                if start:
                    cp.start()
                else:
                    cp.wait()
                return carry
            lax.fori_loop(cnt_ref[e], pad_ref[e], body, 0)

        def tail(blk, carry):
            p = pl.multiple_of(blk * blk_rows, blk_rows)
            cp = pltpu.make_async_copy(zero_ref, xs_hbm.at[pl.ds(p, blk_rows), :], zsem)
            if start:
                cp.start()
            else:
                cp.wait()
            return carry
        lax.fori_loop(nu_ref[0], n_blocks, tail, 0)

    @pl.when(i == 0)
    def _():
        zero_ref[...] = jnp.zeros(zero_ref.shape, F32)
        pad_copies(True)

    def start_body(t, carry):
        token_copies(t, True)
        return carry

    def wait_body(t, carry):
        token_copies(t, False)
        return carry

    lax.fori_loop(0, TM, start_body, 0, unroll=8)
    lax.fori_loop(0, TM, wait_body, 0, unroll=8)

    @pl.when(i == 0)
    def _():
        pad_copies(False)


def moe_dispatch(f_slab, pos, counts, padded, pad_start, n_used, n_blocks):
    nt = f_slab.shape[0] // SLAB
    grid_spec = pltpu.PrefetchScalarGridSpec(
        num_scalar_prefetch=5,
        grid=(nt // TM,),
        in_specs=[pl.BlockSpec((TM * SLAB, 128), lambda i, *_: (i, 0))],
        out_specs=pl.BlockSpec(memory_space=pl.ANY),
        scratch_shapes=[pltpu.VMEM((MOE_BM * SLAB, 128), F32), pltpu.SemaphoreType.DMA, pltpu.SemaphoreType.DMA],
    )
    return pl.pallas_call(
        functools.partial(_dispatch_kernel, nt=nt, n_blocks=n_blocks),
        grid_spec=grid_spec,
        out_shape=jax.ShapeDtypeStruct((n_blocks * MOE_BM * SLAB, 128), F32),
        compiler_params=_cparams(),
        name="moe_dispatch",
    )(pos, counts, padded, pad_start, n_used, f_slab)


def _moe_kernel(be_ref, nu_ref, xs_ref, wg_ref, wu_ref, wd_ref, y_ref, hbuf):
    @pl.when(pl.program_id(0) >= nu_ref[0])
    def _():
        y_ref[...] = jnp.zeros(y_ref.shape, F32)

    @pl.when(pl.program_id(0) < nu_ref[0])
    def _():
        x = _slab_to_rows(xs_ref, MOE_BM).astype(BF16)
        for f0, fw in MOE_FCHUNKS:
            hg = _dot(x, wg_ref[0, 0, :, f0:f0 + fw])
            hu = _dot(x, wu_ref[0, 0, :, f0:f0 + fw])
            hbuf[:, f0:f0 + fw] = (hg * jax.nn.sigmoid(hg) * hu).astype(BF16)
        for c0 in range(0, D_MODEL, MOE_YC):
            _rows_to_slab(y_ref, _dot(hbuf[...], wd_ref[0, 0, :, c0:c0 + MOE_YC]), MOE_BM, c0 // 128)


def moe_experts(x_sorted, block_e, n_used, wg, wu, wd, layer):
    n_blocks = block_e.shape[0]
    d, fexp = wg.shape[2], wg.shape[3]
    blk = lambda b, be, nu: (jnp.minimum(b, nu[0] - 1), 0)
    wsel = lambda b, be, nu: (layer, be[jnp.minimum(b, nu[0] - 1)], 0, 0)
    grid_spec = pltpu.PrefetchScalarGridSpec(
        num_scalar_prefetch=2,
        grid=(n_blocks,),
        in_specs=[pl.BlockSpec((MOE_BM * SLAB, 128), blk),
                  pl.BlockSpec((1, 1, d, fexp), wsel), pl.BlockSpec((1, 1, d, fexp), wsel),
                  pl.BlockSpec((1, 1, fexp, d), wsel)],
        out_specs=pl.BlockSpec((MOE_BM * SLAB, 128), lambda b, be, nu: (b, 0)),
        scratch_shapes=[pltpu.VMEM((MOE_BM, fexp), BF16)],
    )
    return pl.pallas_call(
        _moe_kernel,
        grid_spec=grid_spec,
        out_shape=jax.ShapeDtypeStruct(x_sorted.shape, F32),
        compiler_params=_cparams(),
        name="moe_experts",
    )(block_e, n_used, x_sorted, wg, wu, wd)


def moe_layout(eidx, rank, cnt):
    nt = eidx.shape[1]
    n_blocks = -(-(nt * TOP_K) // MOE_BM) + N_EXPERTS
    counts = cnt[:, 0].astype(jnp.int32)
    padded = (counts + MOE_BM - 1) // MOE_BM * MOE_BM
    pad_end = jnp.cumsum(padded)
    pad_start = pad_end - padded
    hot = eidx[:TOP_K, :, None] == jnp.arange(N_EXPERTS, dtype=jnp.int32)
    pos = (rank[:TOP_K] + jnp.sum(jnp.where(hot, pad_start, 0), axis=-1)).reshape(-1)
    blk_start = jnp.arange(n_blocks, dtype=jnp.int32) * MOE_BM
    block_e = jnp.minimum(jnp.sum(blk_start[:, None] >= pad_end[None, :], axis=1), N_EXPERTS - 1).astype(jnp.int32)
    n_used = (pad_end[-1:] // MOE_BM).astype(jnp.int32)
    return pos, counts, padded, pad_start, block_e, n_used


def _final_kernel(pos_ref, x_ref, y_hbm, wt_ref, mod_ref, g_ref, out_ref, gbuf, sem, *, nt, tile0):
    x = x_ref[...] + mod_ref[0][5:6] * _moe_combine(pos_ref, y_hbm, wt_ref, gbuf, sem, nt, tile0)
    ms = jnp.mean(x * x, axis=-1, keepdims=True)
    out_ref[...] = x * lax.rsqrt(ms + EPS) * g_ref[...]


def final_norm(x, moe, mods_l, g_final, n_ctx_blocks):
    nt, d = x.shape
    nlat = nt // TM - n_ctx_blocks
    y_sorted, pos, wts_t = moe
    grid_spec = pltpu.PrefetchScalarGridSpec(
        num_scalar_prefetch=1,
        grid=(nlat,),
        in_specs=[pl.BlockSpec((TM, d), lambda i, *_: (i + n_ctx_blocks, 0)),
                  pl.BlockSpec(memory_space=pl.ANY),
                  pl.BlockSpec((TM, 8), lambda i, *_: (i + n_ctx_blocks, 0)),
                  pl.BlockSpec((1, 6, d), lambda i, *_: (0, 0, 0)),
                  pl.BlockSpec((1, d), lambda i, *_: (0, 0))],
        out_specs=pl.BlockSpec((TM, d), lambda i, *_: (i, 0)),
        scratch_shapes=_COMBINE_SCRATCH,
    )
    return pl.pallas_call(
        functools.partial(_final_kernel, nt=nt, tile0=n_ctx_blocks),
        grid_spec=grid_spec,
        out_shape=jax.ShapeDtypeStruct((nlat * TM, d), F32),
        compiler_params=_cparams(),
        name="final_norm",
    )(pos, x, y_sorted, wts_t, mods_l, g_final.reshape(1, d))


def kernel(x, c, ctx, c_ctx, w_mod, b_mod, g_mix, g_ffn, w_in_even, w_out_even, sgu_ln_g, sgu_ln_b, sgu_w, sgu_b, conv_w, conv_b, conv_ln_g, conv_ln_b, w_in_odd, w_out_odd, attn_sink, ssm_lam_re, ssm_lam_im, ssm_log_dt, ssm_b_re, ssm_b_im, ssm_c_re, ssm_c_im, ssm_d, glu_w, glu_b, w_router, b_router, w_gate, w_up, w_down, g_final):
    bsz, n_lat, d = x.shape
    n_ctx = ctx.shape[1]
    assert bsz == 1 and d == D_MODEL and n_ctx % TM == 0 and n_lat % TM == 0
    nt = n_ctx + n_lat
    ncb = n_ctx // TM
    n_chunks = nt // S_CHUNK
    ncp = -(-n_chunks // 16) * 16

    xs = jnp.concatenate([ctx[0], x[0]], axis=0)
    cond8 = jnp.concatenate([c, c_ctx[None, :], jnp.zeros((6, d), F32)], axis=0)
    mods = adaln_all(cond8, w_mod, b_mod)[:, :2].reshape(DEPTH, 2, 6, d)
    cos_t, sin_t = rope_tables(n_ctx, n_lat)
    w_router_t = w_router.T
    b_router_col = jnp.broadcast_to(b_router.astype(F32)[:, None], (N_EXPERTS, 128))
    w_gate_bf, w_up_bf, w_down_bf = w_gate.astype(BF16), w_up.astype(BF16), w_down.astype(BF16)

    moe = None
    for l in range(DEPTH):
        j = l // 2
        odd = l % 2 == 1
        if odd:
            wi = w_in_odd[j]
            w_in = jnp.concatenate([wi[:, :C_WIDTH], wi[:, C_WIDTH + 2 * KV_WIDTH:],
                                    wi[:, C_WIDTH:C_WIDTH + 2 * KV_WIDTH]], axis=1).astype(BF16)
            w_out = w_out_odd[j].astype(BF16)
        else:
            w_in = w_in_even[j].astype(BF16)
            w_out = w_out_even[j].astype(BF16)
        x_new, p = inproj(xs, g_mix[l], mods[l], w_in, ncb, moe, mods[l - 1] if l > 0 else None)
        if x_new is not None:
            xs = x_new
        if odd:
            ya = attention(p, attn_sink[j], cos_t, sin_t, n_ctx)
            pw_tab, bc_tab, lam_rows = s5_weights(ssm_lam_re[j], ssm_lam_im[j], ssm_log_dt[j], ssm_b_re[j],
                                                  ssm_b_im[j], ssm_c_re[j], ssm_c_im[j])
            y_ssm = s5_scan(p, pw_tab, bc_tab, lam_rows, n_chunks, n_ctx // S_CHUNK)
            yb = s5_glu(y_ssm, p, ssm_d[j], glu_w[j].astype(BF16), glu_b[j])
        else:
            bs_full = jnp.broadcast_to(sgu_b[j][:, :, None], (A_GROUPS, CHUNK, CHUNK)).astype(F32)
            ya, yb = even_mixer(p, sgu_ln_g[j], sgu_ln_b[j], sgu_w[j].astype(BF16), bs_full,
                                conv_w[j], conv_b[j], conv_ln_g[j], conv_ln_b[j], ncb)
        xs, f_slab, eidx, wts, rank, cnt = outproj(ya, yb, w_out[:1024], w_out[1024:], xs, mods[l], g_ffn[l],
                                                   w_router_t, b_router_col, ncb)
        pos, counts, padded, pad_start, block_e, n_used = moe_layout(eidx, rank, cnt)
        x_sorted = moe_dispatch(f_slab, pos, counts, padded, pad_start, n_used, block_e.shape[0])
        y_sorted = moe_experts(x_sorted, block_e, n_used, w_gate_bf, w_up_bf, w_down_bf, l)
        moe = (y_sorted, pos, wts.T)
    out = final_norm(xs, moe, mods[DEPTH - 1], g_final, ncb)
    return out.reshape(bsz, n_lat, d)
```

```python
import functools
import math

import jax
import jax.numpy as jnp
from jax import lax
from jax.experimental import pallas as pl
from jax.experimental.pallas import tpu as pltpu

F32 = jnp.float32
BF16 = jnp.bfloat16

D_MODEL = 2048
DEPTH = 4
GRID_W = 64
EPS = 1e-6
NEG_INF = -1e30

A_WIDTH = 1024
A_GROUPS = 8
CHUNK = 128
B_WIDTH = 1024
CONV_WIDTH = 31
CONV_HALO = 16

HEAD_DIM = 128
N_Q_HEADS = 8
N_KV_HEADS = 2
Q_PER_KV = 4
C_WIDTH = 1024
KV_WIDTH = 256
WINDOW = 128
ATTN_BLOCK = 128
ROPE_BASE = 10000.0
ATTN_SCALE = HEAD_DIM ** -0.5
S_WIDTH = 1024
S_GROUP_CH = 16
S_GROUPS = 64
S_STATE = 64
S_CHUNK = 32
S_GROUP_BATCH = 8
ODD_IN = C_WIDTH + 2 * KV_WIDTH + S_WIDTH

N_EXPERTS = 16
N_EXPERT_GROUPS = 4
EXPERTS_PER_GROUP = 4
TOP_K = 2
D_EXPERT = 1408
MOE_BM = 256

TM = 256
VMEM_LIMIT = 56 * 1024 * 1024


def _cparams(n_axes=1, vmem=VMEM_LIMIT):
    return pltpu.CompilerParams(dimension_semantics=("arbitrary",) * n_axes, vmem_limit_bytes=vmem)


def _dot(a, b):
    return jnp.dot(a, b, preferred_element_type=F32)


def _dot_nt(a, b):
    return lax.dot_general(a, b, (((1,), (1,)), ((), ())), preferred_element_type=F32)


ADALN_TN = 1024


def _adaln_kernel(cond_ref, w_ref, b_ref, o_ref):
    c = cond_ref[...]
    s = (c * jax.nn.sigmoid(c)).astype(BF16)
    o_ref[0] = _dot(s, w_ref[0].astype(BF16)) + b_ref[0]


def adaln_all(cond8, w_mod, b_mod):
    depth, d, n6 = w_mod.shape
    return pl.pallas_call(
        _adaln_kernel,
        grid=(depth, n6 // ADALN_TN),
        in_specs=[
            pl.BlockSpec((8, d), lambda l, j: (0, 0)),
            pl.BlockSpec((1, d, ADALN_TN), lambda l, j: (l, 0, j)),
            pl.BlockSpec((1, 1, ADALN_TN), lambda l, j: (l, 0, j)),
        ],
        out_specs=pl.BlockSpec((1, 8, ADALN_TN), lambda l, j: (l, 0, j)),
        out_shape=jax.ShapeDtypeStruct((depth, 8, n6), F32),
        compiler_params=_cparams(2),
        name="adaln",
    )(cond8, w_mod, b_mod.reshape(depth, 1, n6))


def _mod_spec(n_ctx_blocks):
    return pl.BlockSpec((1, 6, D_MODEL), lambda i, *_: (jnp.where(i < n_ctx_blocks, 1, 0), 0, 0))


def _rms_mod(x, g, shift, scale):
    ms = jnp.mean(x * x, axis=-1, keepdims=True)
    y = x * lax.rsqrt(ms + EPS) * g
    return y * (1.0 + scale) + shift


SLAB_DATA = D_MODEL // 128
SLAB = SLAB_DATA + 4


def _slab_to_rows(ref, n_rows):
    return jnp.concatenate([ref[pl.ds(c, n_rows, stride=SLAB), :] for c in range(SLAB_DATA)], axis=1)


def _rows_to_slab(ref, val, n_rows, c0):
    for c in range(val.shape[1] // 128):
        ref[pl.ds(c0 + c, n_rows, stride=SLAB), :] = val[:, c * 128:(c + 1) * 128]


def _zero_slab_padding(ref, n_rows):
    for c in range(SLAB_DATA, SLAB):
        ref[pl.ds(c, n_rows, stride=SLAB), :] = jnp.zeros((n_rows, 128), F32)


def _expert_row_gather(pos_ref, y_hbm, gbuf, sem, tile, slot, nt, start):
    if not start:
        for k in range(TOP_K):
            pltpu.make_async_copy(y_hbm.at[pl.ds(0, TM * SLAB), :], gbuf.at[slot, k], sem.at[slot]).wait()
        return

    def body(t, carry):
        for k in range(TOP_K):
            p = pl.multiple_of(pos_ref[k * nt + tile * TM + t] * SLAB, SLAB)
            pltpu.make_async_copy(y_hbm.at[pl.ds(p, SLAB), :],
                                  gbuf.at[slot, k, pl.ds(pl.multiple_of(t * SLAB, SLAB), SLAB), :],
                                  sem.at[slot]).start()
        return carry
    lax.fori_loop(0, TM, body, 0, unroll=8)


def _moe_combine(pos_ref, y_hbm, wt_ref, gbuf, sem, nt, tile0):
    i = pl.program_id(0)
    slot = i % 2

    @pl.when(i == 0)
    def _():
        _expert_row_gather(pos_ref, y_hbm, gbuf, sem, tile0, 0, nt, True)

    @pl.when(i + 1 < pl.num_programs(0))
    def _():
        _expert_row_gather(pos_ref, y_hbm, gbuf, sem, tile0 + i + 1, 1 - slot, nt, True)

    _expert_row_gather(pos_ref, y_hbm, gbuf, sem, tile0 + i, slot, nt, False)
    wt = wt_ref[...]
    return (wt[:, 0:1] * _slab_to_rows(gbuf.at[slot, 0], TM) + wt[:, 1:2] * _slab_to_rows(gbuf.at[slot, 1], TM))


_COMBINE_SCRATCH = [pltpu.VMEM((2, TOP_K, TM * SLAB, 128), F32), pltpu.SemaphoreType.DMA((2,))]


INPROJ_NC = 512


def _inproj_kernel(*refs, combine, nt):
    if combine:
        pos_ref, x_ref, y_hbm, wt_ref, modp_ref, g_ref, mod_ref, w_ref, xo_ref, p_ref, gbuf, sem = refs
        x = x_ref[...] + modp_ref[0][5:6] * _moe_combine(pos_ref, y_hbm, wt_ref, gbuf, sem, nt, 0)
        xo_ref[...] = x
    else:
        x_ref, g_ref, mod_ref, w_ref, p_ref = refs
        x = x_ref[...]
    m = mod_ref[0]
    h = _rms_mod(x, g_ref[...], m[0:1], m[1:2]).astype(BF16)
    n = w_ref.shape[1]
    for j in range(0, n, INPROJ_NC):
        p_ref[:, j:j + INPROJ_NC] = _dot(h, w_ref[:, j:j + INPROJ_NC])


def inproj(x, g, mods_l, w_bf16, n_ctx_blocks, moe=None, mods_prev=None):
    nt, d = x.shape
    n = w_bf16.shape[1]
    nblk = nt // TM
    row = pl.BlockSpec((TM, d), lambda i, *_: (i, 0))
    g_spec = pl.BlockSpec((1, d), lambda i, *_: (0, 0))
    w_spec = pl.BlockSpec((d, n), lambda i, *_: (0, 0), pipeline_mode=pl.Buffered(1))
    p_spec = pl.BlockSpec((TM, n), lambda i, *_: (i, 0))
    p_shape = jax.ShapeDtypeStruct((nt, n), F32)
    if moe is None:
        return None, pl.pallas_call(
            functools.partial(_inproj_kernel, combine=False, nt=nt),
            grid=(nblk,),
            in_specs=[row, g_spec, _mod_spec(n_ctx_blocks), w_spec],
            out_specs=p_spec,
            out_shape=p_shape,
            compiler_params=_cparams(),
            name="inproj",
        )(x, g.reshape(1, d), mods_l, w_bf16)
    y_sorted, pos, wts_t = moe
    grid_spec = pltpu.PrefetchScalarGridSpec(
        num_scalar_prefetch=1,
        grid=(nblk,),
        in_specs=[row, pl.BlockSpec(memory_space=pl.ANY), pl.BlockSpec((TM, 8), lambda i, *_: (i, 0)),
                  _mod_spec(n_ctx_blocks), g_spec, _mod_spec(n_ctx_blocks), w_spec],
        out_specs=[row, p_spec],
        scratch_shapes=_COMBINE_SCRATCH,
    )
    return pl.pallas_call(
        functools.partial(_inproj_kernel, combine=True, nt=nt),
        grid_spec=grid_spec,
        out_shape=[jax.ShapeDtypeStruct((nt, d), F32), p_shape],
        compiler_params=_cparams(),
        name="combine_inproj",
    )(pos, x, y_sorted, wts_t, mods_prev, g.reshape(1, d), mods_l, w_bf16)


CONV_RC = 64


def _layer_norm(x, g, b):
    mu = jnp.mean(x, axis=-1, keepdims=True)
    xc = x - mu
    var = jnp.mean(xc * xc, axis=-1, keepdims=True)
    return xc * lax.rsqrt(var + EPS) * g + b


def _even_kernel(u_ref, v_ref, a_ref, g_ref, ap_ref, gp_ref, an_ref, gn_ref,
                 lng_ref, lnb_ref, ws_ref, bs_ref, cw_ref, cb_ref, clg_ref, clb_ref,
                 ya_ref, yb_ref, hpad_ref, cacc_ref, shift_ref, *, n_ctx_blocks, n_blocks):
    i = pl.program_id(0)
    for c in range(TM // CHUNK):
        rows = slice(c * CHUNK, (c + 1) * CHUNK)
        vn = _layer_norm(jax.nn.gelu(v_ref[rows, :]), lng_ref[...], lnb_ref[...]).astype(BF16)
        for grp in range(A_GROUPS):
            cols = slice(grp * CHUNK, (grp + 1) * CHUNK)
            mixed = _dot(ws_ref[grp], vn[:, cols]) + bs_ref[grp]
            ya_ref[rows, cols] = (jax.nn.gelu(u_ref[rows, cols]) * mixed).astype(BF16)
    first = jnp.logical_or(i == 0, i == n_ctx_blocks)
    last = jnp.logical_or(i == n_ctx_blocks - 1, i == n_blocks - 1)
    hpad_ref[0:CONV_HALO, :] = jnp.where(first, 0.0, ap_ref[...] * jax.nn.sigmoid(gp_ref[...]))
    hpad_ref[CONV_HALO:CONV_HALO + TM, :] = a_ref[...] * jax.nn.sigmoid(g_ref[...])
    hpad_ref[CONV_HALO + TM:, :] = jnp.where(last, 0.0, an_ref[...] * jax.nn.sigmoid(gn_ref[...]))
    off = CONV_HALO - CONV_WIDTH // 2
    n_sh = shift_ref.shape[1]
    for cc in range(B_WIDTH // 128):
        cols = slice(cc * 128, (cc + 1) * 128)
        for j in range(1, 8):
            shift_ref[j - 1, :, cols] = hpad_ref[j:j + n_sh, cols]
    for cc in range(B_WIDTH // 128):
        cols = slice(cc * 128, (cc + 1) * 128)
        for rc in range(TM // CONV_RC):
            acc = jnp.zeros((CONV_RC, 128), F32)
            for k in range(CONV_WIDTH):
                q, j = divmod(k + off, 8)
                r0 = rc * CONV_RC + 8 * q
                tap = hpad_ref[r0:r0 + CONV_RC, cols] if j == 0 else shift_ref[j - 1, r0:r0 + CONV_RC, cols]
                acc = acc + cw_ref[k:k + 1, cols] * tap
            cacc_ref[rc * CONV_RC:(rc + 1) * CONV_RC, cols] = acc
    hc = _layer_norm(cacc_ref[...] + cb_ref[...], clg_ref[...], clb_ref[...])
    yb_ref[...] = (hc * jax.nn.sigmoid(hc)).astype(BF16)


def even_mixer(p, ln_g, ln_b, ws_bf16, bs_full, conv_w, conv_b, cln_g, cln_b, n_ctx_blocks):
    nt = p.shape[0]
    nblk = nt // TM
    hb = TM // CONV_HALO
    last_h = nt // CONV_HALO - 1
    col = lambda j: pl.BlockSpec((TM, 1024), lambda i: (i, j))
    prev = lambda j: pl.BlockSpec((CONV_HALO, 1024), lambda i: (jnp.maximum(i * hb - 1, 0), j))
    nxt = lambda j: pl.BlockSpec((CONV_HALO, 1024), lambda i: (jnp.minimum((i + 1) * hb, last_h), j))
    vec = pl.BlockSpec((1, 1024), lambda i: (0, 0))
    out = pl.BlockSpec((TM, 1024), lambda i: (i, 0))
    return pl.pallas_call(
        functools.partial(_even_kernel, n_ctx_blocks=n_ctx_blocks, n_blocks=nblk),
        grid=(nblk,),
        in_specs=[col(0), col(1), col(2), col(3), prev(2), prev(3), nxt(2), nxt(3),
                  vec, vec,
                  pl.BlockSpec((A_GROUPS, CHUNK, CHUNK), lambda i: (0, 0, 0)),
                  pl.BlockSpec((A_GROUPS, CHUNK, CHUNK), lambda i: (0, 0, 0)),
                  pl.BlockSpec((CONV_WIDTH, 1024), lambda i: (0, 0)),
                  vec, vec, vec],
        out_specs=[out, out],
        out_shape=[jax.ShapeDtypeStruct((nt, 1024), BF16)] * 2,
        scratch_shapes=[pltpu.VMEM((TM + 2 * CONV_HALO, 1024), F32), pltpu.VMEM((TM, 1024), F32),
                        pltpu.VMEM((7, TM + 2 * CONV_HALO - 8, 1024), F32)],
        compiler_params=_cparams(),
        name="even_mixer",
    )(p, p, p, p, p, p, p, p, ln_g.reshape(1, -1), ln_b.reshape(1, -1), ws_bf16, bs_full,
      conv_w, conv_b.reshape(1, -1), cln_g.reshape(1, -1), cln_b.reshape(1, -1))


def _second_max(a0, a1, a2, a3):
    m01, n01 = jnp.maximum(a0, a1), jnp.minimum(a0, a1)
    m23, n23 = jnp.maximum(a2, a3), jnp.minimum(a2, a3)
    return jnp.maximum(m01, m23), jnp.maximum(jnp.minimum(m01, m23), jnp.maximum(n01, n23))


def _route_tile(logits, b_col):
    aff = jax.nn.sigmoid(logits)
    biased = aff + b_col
    row = lambda m, e: m[e:e + 1, :]
    g_sel = best = None
    for g in range(N_EXPERT_GROUPS):
        top1, top2 = _second_max(*[row(biased, EXPERTS_PER_GROUP * g + j) for j in range(EXPERTS_PER_GROUP)])
        score = top1 + top2
        if g == 0:
            g_sel, best = jnp.zeros(score.shape, jnp.int32), score
        else:
            upd = score > best
            g_sel, best = jnp.where(upd, g, g_sel), jnp.where(upd, score, best)

    def in_group(m, j):
        out = row(m, (N_EXPERT_GROUPS - 1) * EXPERTS_PER_GROUP + j)
        for g in range(N_EXPERT_GROUPS - 2, -1, -1):
            out = jnp.where(g_sel == g, row(m, EXPERTS_PER_GROUP * g + j), out)
        return out

    v = [in_group(biased, j) for j in range(EXPERTS_PER_GROUP)]
    a = [in_group(aff, j) for j in range(EXPERTS_PER_GROUP)]
    i1, b1, w1 = jnp.zeros(g_sel.shape, jnp.int32), v[0], a[0]
    for j in range(1, EXPERTS_PER_GROUP):
        upd = v[j] > b1
        i1, b1, w1 = jnp.where(upd, j, i1), jnp.where(upd, v[j], b1), jnp.where(upd, a[j], w1)
    first = i1 == 0
    i2, b2, w2 = jnp.where(first, 1, 0), jnp.where(first, v[1], v[0]), jnp.where(first, a[1], a[0])
    for j in range(1, EXPERTS_PER_GROUP):
        upd = jnp.logical_and(i1 != j, v[j] > b2)
        i2, b2, w2 = jnp.where(upd, j, i2), jnp.where(upd, v[j], b2), jnp.where(upd, a[j], w2)
    den = w1 + w2
    return EXPERTS_PER_GROUP * g_sel + i1, EXPERTS_PER_GROUP * g_sel + i2, w1 / den, w2 / den


def _rows8(r0, r1):
    sub = lax.broadcasted_iota(jnp.int32, (8, r0.shape[1]), 0)
    return jnp.where(sub == 0, r0, jnp.where(sub == 1, r1, jnp.zeros_like(r0)))


def _outproj_kernel(ya_ref, yb_ref, wa_ref, wb_ref, x_ref, mod_ref, g_ref, wr_ref, br_ref, tri_ref,
                    xo_ref, f_ref, e_ref, w_ref, r_ref, cnt_ref, carry_ref):
    i = pl.program_id(0)
    m = mod_ref[0]
    y = _dot(ya_ref[...], wa_ref[...]) + _dot(yb_ref[...], wb_ref[...])
    x = x_ref[...] + m[2:3] * y
    xo_ref[...] = x
    f = _rms_mod(x, g_ref[...], m[3:4], m[4:5])
    _rows_to_slab(f_ref, f, TM, 0)
    _zero_slab_padding(f_ref, TM)
    f_hi = f.astype(BF16)
    f_lo = (f - f_hi.astype(F32)).astype(BF16)
    wr = wr_ref[...]
    w_hi = wr.astype(BF16)
    w_lo = (wr - w_hi.astype(F32)).astype(BF16)
    logits = _dot_nt(w_hi, f_hi) + _dot_nt(w_lo, f_hi) + _dot_nt(w_hi, f_lo)
    e0, e1, w0, w1 = _route_tile(logits, br_ref[:, 0:1])

    @pl.when(i == 0)
    def _():
        carry_ref[...] = jnp.zeros(carry_ref.shape, F32)

    sub = lax.broadcasted_iota(jnp.int32, (N_EXPERTS, TM), 0)
    hot0, hot1 = sub == e0, sub == e1
    member = jnp.where(jnp.logical_or(hot0, hot1), 1.0, 0.0)
    before = _dot(member.astype(BF16), tri_ref[...]) + carry_ref[:, 0:1]
    r0 = jnp.sum(jnp.where(hot0, before, 0.0), axis=0, keepdims=True)
    r1 = jnp.sum(jnp.where(hot1, before, 0.0), axis=0, keepdims=True)
    carry_ref[...] = carry_ref[...] + jnp.sum(member, axis=1, keepdims=True)
    e_ref[...] = _rows8(e0, e1)
    w_ref[...] = _rows8(w0, w1)
    r_ref[...] = _rows8(r0.astype(jnp.int32), r1.astype(jnp.int32))
    cnt_ref[...] = carry_ref[...]


def outproj(ya, yb, wa, wb, x, mods_l, g_ffn, w_router_t, b_router_col, n_ctx_blocks):
    nt, d = x.shape
    nblk = nt // TM
    half = pl.BlockSpec((TM, 1024), lambda i: (i, 0))
    wsp = pl.BlockSpec((1024, d), lambda i: (0, 0), pipeline_mode=pl.Buffered(1))
    row = pl.BlockSpec((TM, d), lambda i: (i, 0))
    r8 = pl.BlockSpec((8, TM), lambda i: (0, i))
    cnt = pl.BlockSpec((N_EXPERTS, 128), lambda i: (0, 0))
    tri = (jnp.arange(TM)[:, None] < jnp.arange(TM)[None, :]).astype(BF16)
    return pl.pallas_call(
        _outproj_kernel,
        grid=(nblk,),
        in_specs=[half, half, wsp, wsp, row, _mod_spec(n_ctx_blocks),
                  pl.BlockSpec((1, d), lambda i: (0, 0)),
                  pl.BlockSpec((N_EXPERTS, d), lambda i: (0, 0)), cnt,
                  pl.BlockSpec((TM, TM), lambda i: (0, 0))],
        out_specs=[row, pl.BlockSpec((TM * SLAB, 128), lambda i: (i, 0)), r8, r8, r8, cnt],
        out_shape=[jax.ShapeDtypeStruct((nt, d), F32), jax.ShapeDtypeStruct((nt * SLAB, 128), F32),
                   jax.ShapeDtypeStruct((8, nt), jnp.int32), jax.ShapeDtypeStruct((8, nt), F32),
                   jax.ShapeDtypeStruct((8, nt), jnp.int32), jax.ShapeDtypeStruct((N_EXPERTS, 128), F32)],
        scratch_shapes=[pltpu.VMEM((N_EXPERTS, 128), F32)],
        compiler_params=_cparams(),
        name="outproj",
    )(ya, yb, wa, wb, x, mods_l, g_ffn.reshape(1, d), w_router_t, b_router_col, tri)


N_LOC = 3 * ATTN_BLOCK


def _rope(x, cos, sin):
    lane = lax.broadcasted_iota(jnp.int32, x.shape, 1)
    swapped = jnp.where(lane % 64 < 32, pltpu.roll(x, 96, axis=1), pltpu.roll(x, 32, axis=1))
    return x * cos + swapped * sin


def _attn_kernel(sink_ref, q_ref, kp_ref, kc_ref, kn_ref, vp_ref, vc_ref, vn_ref, kx_ref, vx_ref,
                 cosp_ref, cosc_ref, cosn_ref, sinp_ref, sinc_ref, sinn_ref, o_ref,
                 qs_ref, kbuf_ref, vbuf_ref, *, n_ctx_blocks, n_blocks, n_ctx):
    i = pl.program_id(0)
    nkeys = N_LOC + n_ctx
    nq = Q_PER_KV * ATTN_BLOCK
    is_lat = jnp.where(i >= n_ctx_blocks, 1, 0)
    prev_ok = jnp.where(i - 1 >= n_ctx_blocks, is_lat, 0)
    next_ok = jnp.where(i + 1 <= n_blocks - 1, is_lat, 0)
    qi = lax.broadcasted_iota(jnp.int32, (nq, nkeys), 0) & (ATTN_BLOCK - 1)
    kj = lax.broadcasted_iota(jnp.int32, (nq, nkeys), 1)
    rel = kj - ATTN_BLOCK - qi
    blk_ok = jnp.where(kj < ATTN_BLOCK, prev_ok, jnp.where(kj < 2 * ATTN_BLOCK, is_lat, next_ok))
    rel = jnp.where(blk_ok > 0, rel, WINDOW + 1)
    valid = jnp.logical_or(kj >= N_LOC, jnp.logical_and(rel >= -WINDOW, rel <= WINDOW))
    rowh = lax.broadcasted_iota(jnp.int32, (nq, 1), 0) // ATTN_BLOCK
    for h in range(N_KV_HEADS):
        hc = slice(h * HEAD_DIM, (h + 1) * HEAD_DIM)
        kbuf_ref[0:ATTN_BLOCK, :] = _rope(kp_ref[:, hc], cosp_ref[...], sinp_ref[...]).astype(BF16)
        kbuf_ref[ATTN_BLOCK:2 * ATTN_BLOCK, :] = _rope(kc_ref[:, hc], cosc_ref[...], sinc_ref[...]).astype(BF16)
        kbuf_ref[2 * ATTN_BLOCK:N_LOC, :] = _rope(kn_ref[:, hc], cosn_ref[...], sinn_ref[...]).astype(BF16)
        kbuf_ref[N_LOC:, :] = kx_ref[:, hc].astype(BF16)
        vbuf_ref[0:ATTN_BLOCK, :] = vp_ref[:, hc].astype(BF16)
        vbuf_ref[ATTN_BLOCK:2 * ATTN_BLOCK, :] = vc_ref[:, hc].astype(BF16)
        vbuf_ref[2 * ATTN_BLOCK:N_LOC, :] = vn_ref[:, hc].astype(BF16)
        vbuf_ref[N_LOC:, :] = vx_ref[:, hc].astype(BF16)
        sink = jnp.zeros((Q_PER_KV * ATTN_BLOCK, 1), F32)
        for gq in range(Q_PER_KV):
            head = h * Q_PER_KV + gq
            qc = slice(head * HEAD_DIM, (head + 1) * HEAD_DIM)
            qs_ref[gq * ATTN_BLOCK:(gq + 1) * ATTN_BLOCK, :] = _rope(
                q_ref[:, qc], cosc_ref[...], sinc_ref[...]).astype(BF16)
            sink = jnp.where(rowh == gq, sink_ref[head], sink)
        s = _dot_nt(qs_ref[...], kbuf_ref[...]) * ATTN_SCALE
        s = jnp.where(valid, s, NEG_INF)
        mx = jnp.maximum(jnp.max(s, axis=-1, keepdims=True), sink)
        p = jnp.exp(s - mx)
        den = jnp.sum(p, axis=-1, keepdims=True) + jnp.exp(sink - mx)
        o = _dot(p.astype(BF16), vbuf_ref[...]) / den
        for gq in range(Q_PER_KV):
            head = h * Q_PER_KV + gq
            o_ref[:, head * HEAD_DIM:(head + 1) * HEAD_DIM] = o[gq * ATTN_BLOCK:(gq + 1) * ATTN_BLOCK].astype(BF16)


def attention(p, sink, cos_t, sin_t, n_ctx):
    nt = p.shape[0]
    nblk = nt // ATTN_BLOCK
    ncb = n_ctx // ATTN_BLOCK
    kcol, vcol = 2048 // KV_WIDTH, 2048 // KV_WIDTH + 1
    pm = lambda i: jnp.maximum(i - 1, 0)
    nx = lambda i: jnp.minimum(i + 1, nblk - 1)
    kv = lambda f, c: pl.BlockSpec((ATTN_BLOCK, KV_WIDTH), lambda i, s: (f(i), c))
    tab = lambda f: pl.BlockSpec((ATTN_BLOCK, HEAD_DIM), lambda i, s: (f(i), 0))
    same = lambda i: i
    grid_spec = pltpu.PrefetchScalarGridSpec(
        num_scalar_prefetch=1,
        grid=(nblk,),
        in_specs=[pl.BlockSpec((ATTN_BLOCK, C_WIDTH), lambda i, s: (i, 0)),
                  kv(pm, kcol), kv(same, kcol), kv(nx, kcol), kv(pm, vcol), kv(same, vcol), kv(nx, vcol),
                  pl.BlockSpec((n_ctx, KV_WIDTH), lambda i, s: (0, kcol)),
                  pl.BlockSpec((n_ctx, KV_WIDTH), lambda i, s: (0, vcol)),
                  tab(pm), tab(same), tab(nx), tab(pm), tab(same), tab(nx)],
        out_specs=pl.BlockSpec((ATTN_BLOCK, C_WIDTH), lambda i, s: (i, 0)),
        scratch_shapes=[pltpu.VMEM((Q_PER_KV * ATTN_BLOCK, HEAD_DIM), BF16),
                        pltpu.VMEM((N_LOC + n_ctx, HEAD_DIM), BF16),
                        pltpu.VMEM((N_LOC + n_ctx, HEAD_DIM), BF16)],
    )
    return pl.pallas_call(
        functools.partial(_attn_kernel, n_ctx_blocks=ncb, n_blocks=nblk, n_ctx=n_ctx),
        grid_spec=grid_spec,
        out_shape=jax.ShapeDtypeStruct((nt, C_WIDTH), BF16),
        compiler_params=_cparams(),
        name="attention",
    )(sink, p, p, p, p, p, p, p, p, p, cos_t, cos_t, cos_t, sin_t, sin_t, sin_t)


def rope_tables(n_ctx, n_lat):
    rows = n_lat // GRID_W
    row = jnp.broadcast_to(jnp.arange(rows, dtype=F32)[:, None], (rows, GRID_W)).reshape(-1)
    col = jnp.broadcast_to(jnp.arange(GRID_W, dtype=F32)[None, :], (rows, GRID_W)).reshape(-1)
    half = HEAD_DIM // 2
    inv_freq = ROPE_BASE ** (-jnp.arange(0, half, 2, dtype=F32) / half)
    ang_r = row[:, None] * inv_freq
    ang_c = col[:, None] * inv_freq
    cos = jnp.concatenate([jnp.cos(ang_r), jnp.cos(ang_r), jnp.cos(ang_c), jnp.cos(ang_c)], axis=-1)
    sin = jnp.concatenate([-jnp.sin(ang_r), jnp.sin(ang_r), -jnp.sin(ang_c), jnp.sin(ang_c)], axis=-1)
    cos = jnp.concatenate([jnp.ones((n_ctx, HEAD_DIM), F32), cos], axis=0)
    sin = jnp.concatenate([jnp.zeros((n_ctx, HEAD_DIM), F32), sin], axis=0)
    return cos, sin


S_LH = S_CHUNK * S_GROUP_CH
S_HALF = S_GROUP_BATCH * 128


S_PW_ROWS = 4 * S_CHUNK
S_PITCH = S_CHUNK + 8


def _dot3_nt(x, a):
    x_hi, a_hi = x.astype(BF16), a.astype(BF16)
    x_lo, a_lo = (x - x_hi.astype(F32)).astype(BF16), (a - a_hi.astype(F32)).astype(BF16)
    return _dot_nt(x_hi, a_hi) + _dot_nt(x_lo, a_hi) + _dot_nt(x_hi, a_lo)


def _s5_kernel(u_ref, pw_ref, bc_ref, lam_ref, y_ref, s_ref, hf_ref, hr_ref,
               ws_ref, wo_ref, are_ref, aim_ref, t_ref, ug_ref, yg_ref, upad_ref, ypad_ref, *,
               n_chunks, n_ctx_chunks):
    L = S_CHUNK
    gpt = 128 // S_GROUP_CH
    ug_ref[:, n_chunks:, :] = jnp.zeros((S_GROUP_BATCH, ug_ref.shape[1] - n_chunks, S_LH), F32)

    def spread(c, carry):
        upad_ref[pl.ds(pl.multiple_of(c * S_PITCH, 8), L), :] = u_ref[pl.ds(pl.multiple_of(c * L, L), L), :]
        return carry
    lax.fori_loop(0, n_chunks, spread, 0, unroll=8)
    for s_tok in range(L):
        x = upad_ref[pl.ds(s_tok, n_chunks, stride=S_PITCH), :]
        dst = s_tok % gpt
        for g in range(S_GROUP_BATCH):
            k = (dst - g) % gpt
            r = x if k == 0 else pltpu.roll(x, k * S_GROUP_CH, axis=1)
            ug_ref[g, 0:n_chunks, s_tok * S_GROUP_CH:(s_tok + 1) * S_GROUP_CH] = (
                r[:, dst * S_GROUP_CH:(dst + 1) * S_GROUP_CH])

    def scaled_rows(g, x_re, x_im, row0, n_rows, emit):
        def body(r, carry):
            p_re = pw_ref[g, 0, pl.ds(row0 + r, 1), :]
            p_im = pw_ref[g, 1, pl.ds(row0 + r, 1), :]
            emit(pl.ds(pl.multiple_of(r * S_GROUP_CH, S_GROUP_CH), S_GROUP_CH),
                 x_re * p_re - x_im * p_im, x_re * p_im + x_im * p_re)
            return carry
        lax.fori_loop(0, n_rows, body, 0, unroll=4)

    def emit_ws(rows, re, im):
        ws_ref[rows, 0:128] = re.astype(BF16)
        ws_ref[rows, 128:256] = im.astype(BF16)

    def emit_wo(rows, re, im):
        wo_ref[rows, 0:128] = re.astype(BF16)
        wo_ref[rows, 128:256] = (-im).astype(BF16)

    def emit_a(rows, re, im):
        are_ref[rows, :] = re
        aim_ref[rows, :] = im

    for g in range(S_GROUP_BATCH):
        scaled_rows(g, bc_ref[g, 0], bc_ref[g, 1], 0, L, emit_ws)
        s = _dot(ug_ref[g].astype(BF16), ws_ref[...])
        s_ref[:, g * 128:(g + 1) * 128] = s[:, 0:128]
        s_ref[:, S_HALF + g * 128:S_HALF + (g + 1) * 128] = s[:, 128:256]
    hf_ref[...] = jnp.zeros(hf_ref.shape, F32)
    hr_ref[...] = jnp.zeros(hr_ref.shape, F32)
    lam = lam_ref[0]
    lam_re, lam_im = lam[:, :S_HALF], lam[:, S_HALF:]
    is_fwd = lax.broadcasted_iota(jnp.int32, (1, 2 * S_HALF), 1) % 128 < S_STATE

    def step(k, state):
        st_re, st_im = state
        cf = k
        cr = jnp.where(k < n_ctx_chunks, n_ctx_chunks - 1 - k, n_chunks - 1 - (k - n_ctx_chunks))
        st = jnp.concatenate([st_re, st_im], axis=1)
        hf_ref[pl.ds(cf, 1), :] = st
        hr_ref[pl.ds(cr, 1), :] = st
        s_in = jnp.where(is_fwd, s_ref[pl.ds(cf, 1), :], s_ref[pl.ds(cr, 1), :])
        new_re = lam_re * st_re - lam_im * st_im + s_in[:, :S_HALF]
        new_im = lam_re * st_im + lam_im * st_re + s_in[:, S_HALF:]
        return new_re, new_im

    zero = jnp.zeros((1, S_HALF), F32)
    lax.fori_loop(0, n_chunks, step, (zero, zero))
    hin = jnp.where(is_fwd, hf_ref[...], hr_ref[...]).astype(BF16)
    kwidth = 2 * L * S_GROUP_CH
    for g in range(S_GROUP_BATCH):
        hin_g = jnp.concatenate([hin[:, g * 128:(g + 1) * 128],
                                 hin[:, S_HALF + g * 128:S_HALF + (g + 1) * 128]], axis=1)
        c_re, c_im = bc_ref[g, 2], bc_ref[g, 3]
        scaled_rows(g, c_re, c_im, L, L, emit_wo)
        scaled_rows(g, c_re, c_im, 2 * L, 2 * L, emit_a)
        kern = _dot3_nt(bc_ref[g, 0], are_ref[...]) - _dot3_nt(bc_ref[g, 1], aim_ref[...])
        for s_tok in range(L):
            off = (L - 1 - s_tok) * S_GROUP_CH
            win = kern if off == 0 else pltpu.roll(kern, kwidth - off, axis=1)
            t_ref[s_tok * S_GROUP_CH:(s_tok + 1) * S_GROUP_CH, :] = win[:, :S_LH].astype(BF16)
        yg_ref[g] = _dot(ug_ref[g].astype(BF16), t_ref[...]) + _dot_nt(hin_g, wo_ref[...])
    lane_group = lax.broadcasted_iota(jnp.int32, (n_chunks, 128), 1) // S_GROUP_CH
    for s_tok in range(L):
        dst = s_tok % gpt
        cols = slice((s_tok // gpt) * 128, (s_tok // gpt + 1) * 128)
        z = None
        for g in range(S_GROUP_BATCH):
            piece = yg_ref[g, 0:n_chunks, cols]
            k = (g - dst) % gpt
            r = piece if k == 0 else pltpu.roll(piece, k * S_GROUP_CH, axis=1)
            z = r if z is None else jnp.where(lane_group == g, r, z)
        ypad_ref[pl.ds(s_tok, n_chunks, stride=S_PITCH), :] = z

    def pack(c, carry):
        y_ref[pl.ds(pl.multiple_of(c * L, L), L), :] = ypad_ref[pl.ds(pl.multiple_of(c * S_PITCH, 8), L), :]
        return carry
    lax.fori_loop(0, n_chunks, pack, 0, unroll=8)


def s5_scan(p, pw_tab, bc_tab, lam_rows, n_chunks, n_ctx_chunks):
    nt = p.shape[0]
    assert S_GROUP_BATCH * S_GROUP_CH == 128 and nt == n_chunks * S_CHUNK
    ncp = -(-n_chunks // 16) * 16
    lh = S_LH
    return pl.pallas_call(
        functools.partial(_s5_kernel, n_chunks=n_chunks, n_ctx_chunks=n_ctx_chunks),
        grid=(S_GROUPS // S_GROUP_BATCH,),
        in_specs=[pl.BlockSpec((nt, 128), lambda i: (0, C_WIDTH // 128 + i)),
                  pl.BlockSpec((S_GROUP_BATCH, 2, S_PW_ROWS, 128), lambda i: (i, 0, 0, 0)),
                  pl.BlockSpec((S_GROUP_BATCH, 4, S_GROUP_CH, 128), lambda i: (i, 0, 0, 0)),
                  pl.BlockSpec((1, 1, 2 * S_HALF), lambda i: (i, 0, 0))],
        out_specs=pl.BlockSpec((nt, 128), lambda i: (0, i)),
        out_shape=jax.ShapeDtypeStruct((nt, S_WIDTH), F32),
        scratch_shapes=[pltpu.VMEM((ncp, 2 * S_HALF), F32)] * 3 + [
            pltpu.VMEM((lh, 256), BF16), pltpu.VMEM((lh, 256), BF16),
            pltpu.VMEM((2 * lh, 128), F32), pltpu.VMEM((2 * lh, 128), F32), pltpu.VMEM((lh, lh), BF16),
            pltpu.VMEM((S_GROUP_BATCH, ncp, lh), F32), pltpu.VMEM((S_GROUP_BATCH, ncp, lh), F32),
            pltpu.VMEM((n_chunks * S_PITCH, 128), F32), pltpu.VMEM((n_chunks * S_PITCH, 128), F32)],
        compiler_params=_cparams(),
        name="s5_scan",
    )(p, pw_tab, bc_tab, lam_rows)


def s5_weights(lam_re, lam_im, log_dt, b_re, b_im, c_re, c_im):
    L = S_CHUNK
    lam = lax.complex(lam_re.astype(F32), lam_im.astype(F32))
    lam_dt = lam * jnp.exp(log_dt.astype(F32))[..., None]
    lam_bar = jnp.exp(lam_dt)
    b_bar = ((lam_bar - 1.0) / lam)[..., None] * lax.complex(b_re.astype(F32), b_im.astype(F32))
    c = lax.complex(c_re.astype(F32), c_im.astype(F32))
    n = jnp.arange(L, dtype=F32)
    lag = jnp.arange(2 * L, dtype=F32) - (L - 1)
    expo = jnp.concatenate([jnp.stack([L - 1 - n, n], axis=-1), jnp.stack([n + 1, L - n], axis=-1),
                            jnp.stack([lag, -lag], axis=-1)], axis=0)
    live = jnp.logical_and(expo >= 0, (jnp.arange(4 * L) < 4 * L - 1)[:, None])
    pw = jnp.where(live[:, :, None, None], jnp.exp(lam_dt[None] * jnp.maximum(expo, 0.0)[:, :, None, None]), 0.0)
    pw = jnp.transpose(pw, (2, 0, 1, 3)).reshape(S_GROUPS, S_PW_ROWS, 2 * S_STATE)
    pw_tab = jnp.stack([pw.real, pw.imag], axis=1)
    bt = jnp.transpose(b_bar, (1, 3, 0, 2)).reshape(S_GROUPS, S_GROUP_CH, 2 * S_STATE)
    ct = jnp.transpose(c, (1, 2, 0, 3)).reshape(S_GROUPS, S_GROUP_CH, 2 * S_STATE)
    bc_tab = jnp.stack([bt.real, bt.imag, ct.real, ct.imag], axis=1)
    lam_l = jnp.exp(lam_dt * L)
    nb = S_GROUPS // S_GROUP_BATCH
    lre = jnp.concatenate([lam_l[0].real, lam_l[1].real], axis=-1).reshape(nb, 1, S_HALF)
    lim = jnp.concatenate([lam_l[0].imag, lam_l[1].imag], axis=-1).reshape(nb, 1, S_HALF)
    return pw_tab, bc_tab, jnp.concatenate([lre, lim], axis=-1)


def _glu_kernel(y_ref, u_ref, d_ref, w_ref, b_ref, o_ref):
    z = jax.nn.gelu(y_ref[...] + d_ref[...] * u_ref[...])
    gate = jax.nn.sigmoid(_dot(z.astype(BF16), w_ref[...]) + b_ref[...])
    o_ref[...] = (z * gate).astype(BF16)


def s5_glu(y_ssm, p, d_skip, glu_w_bf16, glu_b):
    nt = y_ssm.shape[0]
    vec = pl.BlockSpec((1, S_WIDTH), lambda i: (0, 0))
    return pl.pallas_call(
        _glu_kernel,
        grid=(nt // TM,),
        in_specs=[pl.BlockSpec((TM, S_WIDTH), lambda i: (i, 0)),
                  pl.BlockSpec((TM, S_WIDTH), lambda i: (i, 1)),
                  vec, pl.BlockSpec((S_WIDTH, S_WIDTH), lambda i: (0, 0)), vec],
        out_specs=pl.BlockSpec((TM, S_WIDTH), lambda i: (i, 0)),
        out_shape=jax.ShapeDtypeStruct((nt, S_WIDTH), BF16),
        compiler_params=_cparams(),
        name="s5_glu",
    )(y_ssm, p, d_skip.reshape(1, -1), glu_w_bf16, glu_b.reshape(1, -1))


MOE_FCHUNKS = ((0, 512), (512, 512), (1024, 384))
MOE_YC = 512


def _dispatch_kernel(pos_ref, cnt_ref, pad_ref, start_ref, nu_ref, f_ref, xs_hbm, zero_ref, sem, zsem, *,
                     nt, n_blocks):
    i = pl.program_id(0)
    blk_rows = MOE_BM * SLAB

    def start_body(t, carry):
        src = f_ref.at[pl.ds(pl.multiple_of(t * SLAB, SLAB), SLAB), :]
        for k in range(TOP_K):
            p = pl.multiple_of(pos_ref[k * nt + i * TM + t] * SLAB, SLAB)
            pltpu.make_async_copy(src, xs_hbm.at[pl.ds(p, SLAB), :], sem).start()
        return carry

    def pad_copies(start):
        for e in range(N_EXPERTS):
            def body(r, carry):
                p = pl.multiple_of((start_ref[e] + r) * SLAB, SLAB)
                cp = pltpu.make_async_copy(zero_ref.at[pl.ds(0, SLAB), :], xs_hbm.at[pl.ds(p, SLAB), :], zsem)
                if start:
                    cp.start()
                else:
                    cp.wait()
                return carry
            lax.fori_loop(cnt_ref[e], pad_ref[e], body, 0)

        def tail(blk, carry):
            p = pl.multiple_of(blk * blk_rows, blk_rows)
            cp = pltpu.make_async_copy(zero_ref, xs_hbm.at[pl.ds(p, blk_rows), :], zsem)
            if start:
                cp.start()
            else:
                cp.wait()
            return carry
        lax.fori_loop(nu_ref[0], n_blocks, tail, 0)

    @pl.when(i == 0)
    def _():
        zero_ref[...] = jnp.zeros(zero_ref.shape, F32)
        pad_copies(True)

    lax.fori_loop(0, TM, start_body, 0, unroll=8)
    for _ in range(TOP_K):
        pltpu.make_async_copy(f_ref, xs_hbm.at[pl.ds(0, TM * SLAB), :], sem).wait()

    @pl.when(i == 0)
    def _():
        pad_copies(False)


def moe_dispatch(f_slab, pos, counts, padded, pad_start, n_used, n_blocks):
    nt = f_slab.shape[0] // SLAB
    grid_spec = pltpu.PrefetchScalarGridSpec(
        num_scalar_prefetch=5,
        grid=(nt // TM,),
        in_specs=[pl.BlockSpec((TM * SLAB, 128), lambda i, *_: (i, 0))],
        out_specs=pl.BlockSpec(memory_space=pl.ANY),
        scratch_shapes=[pltpu.VMEM((MOE_BM * SLAB, 128), F32), pltpu.SemaphoreType.DMA, pltpu.SemaphoreType.DMA],
    )
    return pl.pallas_call(
        functools.partial(_dispatch_kernel, nt=nt, n_blocks=n_blocks),
        grid_spec=grid_spec,
        out_shape=jax.ShapeDtypeStruct((n_blocks * MOE_BM * SLAB, 128), F32),
        compiler_params=_cparams(),
        name="moe_dispatch",
    )(pos, counts, padded, pad_start, n_used, f_slab)


def _moe_kernel(be_ref, nu_ref, xs_ref, wg_ref, wu_ref, wd_ref, y_ref, hbuf):
    @pl.when(pl.program_id(0) >= nu_ref[0])
    def _():
        y_ref[...] = jnp.zeros(y_ref.shape, F32)

    @pl.when(pl.program_id(0) < nu_ref[0])
    def _():
        x = _slab_to_rows(xs_ref, MOE_BM).astype(BF16)
        for f0, fw in MOE_FCHUNKS:
            hg = _dot(x, wg_ref[0, 0, :, f0:f0 + fw])
            hu = _dot(x, wu_ref[0, 0, :, f0:f0 + fw])
            hbuf[:, f0:f0 + fw] = (hg * jax.nn.sigmoid(hg) * hu).astype(BF16)
        for c0 in range(0, D_MODEL, MOE_YC):
            _rows_to_slab(y_ref, _dot(hbuf[...], wd_ref[0, 0, :, c0:c0 + MOE_YC]), MOE_BM, c0 // 128)
        _zero_slab_padding(y_ref, MOE_BM)


def moe_experts(x_sorted, block_e, n_used, wg, wu, wd, layer):
    n_blocks = block_e.shape[0]
    d, fexp = wg.shape[2], wg.shape[3]
    blk = lambda b, be, nu: (jnp.minimum(b, nu[0] - 1), 0)
    wsel = lambda b, be, nu: (layer, be[jnp.minimum(b, nu[0] - 1)], 0, 0)
    grid_spec = pltpu.PrefetchScalarGridSpec(
        num_scalar_prefetch=2,
        grid=(n_blocks,),
        in_specs=[pl.BlockSpec((MOE_BM * SLAB, 128), blk),
                  pl.BlockSpec((1, 1, d, fexp), wsel), pl.BlockSpec((1, 1, d, fexp), wsel),
                  pl.BlockSpec((1, 1, fexp, d), wsel)],
        out_specs=pl.BlockSpec((MOE_BM * SLAB, 128), lambda b, be, nu: (b, 0)),
        scratch_shapes=[pltpu.VMEM((MOE_BM, fexp), BF16)],
    )
    return pl.pallas_call(
        _moe_kernel,
        grid_spec=grid_spec,
        out_shape=jax.ShapeDtypeStruct(x_sorted.shape, F32),
        compiler_params=_cparams(),
        name="moe_experts",
    )(block_e, n_used, x_sorted, wg, wu, wd)


def moe_layout(eidx, rank, cnt):
    nt = eidx.shape[1]
    n_blocks = -(-(nt * TOP_K) // MOE_BM) + N_EXPERTS
    counts = cnt[:, 0].astype(jnp.int32)
    padded = (counts + MOE_BM - 1) // MOE_BM * MOE_BM
    pad_end = jnp.cumsum(padded)
    pad_start = pad_end - padded
    hot = eidx[:TOP_K, :, None] == jnp.arange(N_EXPERTS, dtype=jnp.int32)
    pos = (rank[:TOP_K] + jnp.sum(jnp.where(hot, pad_start, 0), axis=-1)).reshape(-1)
    blk_start = jnp.arange(n_blocks, dtype=jnp.int32) * MOE_BM
    block_e = jnp.minimum(jnp.sum(blk_start[:, None] >= pad_end[None, :], axis=1), N_EXPERTS - 1).astype(jnp.int32)
    n_used = (pad_end[-1:] // MOE_BM).astype(jnp.int32)
    return pos, counts, padded, pad_start, block_e, n_used


def _final_kernel(pos_ref, x_ref, y_hbm, wt_ref, mod_ref, g_ref, out_ref, gbuf, sem, *, nt, tile0):
    x = x_ref[...] + mod_ref[0][5:6] * _moe_combine(pos_ref, y_hbm, wt_ref, gbuf, sem, nt, tile0)
    ms = jnp.mean(x * x, axis=-1, keepdims=True)
    out_ref[...] = x * lax.rsqrt(ms + EPS) * g_ref[...]


def final_norm(x, moe, mods_l, g_final, n_ctx_blocks):
    nt, d = x.shape
    nlat = nt // TM - n_ctx_blocks
    y_sorted, pos, wts_t = moe
    grid_spec = pltpu.PrefetchScalarGridSpec(
        num_scalar_prefetch=1,
        grid=(nlat,),
        in_specs=[pl.BlockSpec((TM, d), lambda i, *_: (i + n_ctx_blocks, 0)),
                  pl.BlockSpec(memory_space=pl.ANY),
                  pl.BlockSpec((TM, 8), lambda i, *_: (i + n_ctx_blocks, 0)),
                  pl.BlockSpec((1, 6, d), lambda i, *_: (0, 0, 0)),
                  pl.BlockSpec((1, d), lambda i, *_: (0, 0))],
        out_specs=pl.BlockSpec((TM, d), lambda i, *_: (i, 0)),
        scratch_shapes=_COMBINE_SCRATCH,
    )
    return pl.pallas_call(
        functools.partial(_final_kernel, nt=nt, tile0=n_ctx_blocks),
        grid_spec=grid_spec,
        out_shape=jax.ShapeDtypeStruct((nlat * TM, d), F32),
        compiler_params=_cparams(),
        name="final_norm",
    )(pos, x, y_sorted, wts_t, mods_l, g_final.reshape(1, d))


def kernel(x, c, ctx, c_ctx, w_mod, b_mod, g_mix, g_ffn, w_in_even, w_out_even, sgu_ln_g, sgu_ln_b, sgu_w, sgu_b, conv_w, conv_b, conv_ln_g, conv_ln_b, w_in_odd, w_out_odd, attn_sink, ssm_lam_re, ssm_lam_im, ssm_log_dt, ssm_b_re, ssm_b_im, ssm_c_re, ssm_c_im, ssm_d, glu_w, glu_b, w_router, b_router, w_gate, w_up, w_down, g_final):
    bsz, n_lat, d = x.shape
    n_ctx = ctx.shape[1]
    assert bsz == 1 and d == D_MODEL and n_ctx % TM == 0 and n_lat % TM == 0
    nt = n_ctx + n_lat
    ncb = n_ctx // TM
    n_chunks = nt // S_CHUNK
    ncp = -(-n_chunks // 16) * 16

    xs = jnp.concatenate([ctx[0], x[0]], axis=0)
    cond8 = jnp.concatenate([c, c_ctx[None, :], jnp.zeros((6, d), F32)], axis=0)
    mods = adaln_all(cond8, w_mod, b_mod)[:, :2].reshape(DEPTH, 2, 6, d)
    cos_t, sin_t = rope_tables(n_ctx, n_lat)
    w_router_t = w_router.T
    b_router_col = jnp.broadcast_to(b_router.astype(F32)[:, None], (N_EXPERTS, 128))
    w_gate_bf, w_up_bf, w_down_bf = w_gate.astype(BF16), w_up.astype(BF16), w_down.astype(BF16)

    moe = None
    for l in range(DEPTH):
        j = l // 2
        odd = l % 2 == 1
        if odd:
            wi = w_in_odd[j]
            w_in = jnp.concatenate([wi[:, :C_WIDTH], wi[:, C_WIDTH + 2 * KV_WIDTH:],
                                    wi[:, C_WIDTH:C_WIDTH + 2 * KV_WIDTH]], axis=1).astype(BF16)
            w_out = w_out_odd[j].astype(BF16)
        else:
            w_in = w_in_even[j].astype(BF16)
            w_out = w_out_even[j].astype(BF16)
        x_new, p = inproj(xs, g_mix[l], mods[l], w_in, ncb, moe, mods[l - 1] if l > 0 else None)
        if x_new is not None:
            xs = x_new
        if odd:
            ya = attention(p, attn_sink[j], cos_t, sin_t, n_ctx)
            pw_tab, bc_tab, lam_rows = s5_weights(ssm_lam_re[j], ssm_lam_im[j], ssm_log_dt[j], ssm_b_re[j],
                                                  ssm_b_im[j], ssm_c_re[j], ssm_c_im[j])
            y_ssm = s5_scan(p, pw_tab, bc_tab, lam_rows, n_chunks, n_ctx // S_CHUNK)
            yb = s5_glu(y_ssm, p, ssm_d[j], glu_w[j].astype(BF16), glu_b[j])
        else:
            bs_full = jnp.broadcast_to(sgu_b[j][:, :, None], (A_GROUPS, CHUNK, CHUNK)).astype(F32)
            ya, yb = even_mixer(p, sgu_ln_g[j], sgu_ln_b[j], sgu_w[j].astype(BF16), bs_full,
                                conv_w[j], conv_b[j], conv_ln_g[j], conv_ln_b[j], ncb)
        xs, f_slab, eidx, wts, rank, cnt = outproj(ya, yb, w_out[:1024], w_out[1024:], xs, mods[l], g_ffn[l],
                                                   w_router_t, b_router_col, ncb)
        pos, counts, padded, pad_start, block_e, n_used = moe_layout(eidx, rank, cnt)
        x_sorted = moe_dispatch(f_slab, pos, counts, padded, pad_start, n_used, block_e.shape[0])
        y_sorted = moe_experts(x_sorted, block_e, n_used, w_gate_bf, w_up_bf, w_down_bf, l)
        moe = (y_sorted, pos, wts.T)
    out = final_norm(xs, moe, mods[DEPTH - 1], g_final, ncb)
    return out.reshape(bsz, n_lat, d)
```

```python
import functools
import math

import jax
import jax.numpy as jnp
from jax import lax
from jax.experimental import pallas as pl
from jax.experimental.pallas import tpu as pltpu

F32 = jnp.float32
BF16 = jnp.bfloat16

D_MODEL = 2048
DEPTH = 4
GRID_W = 64
EPS = 1e-6
NEG_INF = -1e30

A_WIDTH = 1024
A_GROUPS = 8
CHUNK = 128
B_WIDTH = 1024
CONV_WIDTH = 31
CONV_HALO = 16

HEAD_DIM = 128
N_Q_HEADS = 8
N_KV_HEADS = 2
Q_PER_KV = 4
C_WIDTH = 1024
KV_WIDTH = 256
WINDOW = 128
ATTN_BLOCK = 128
ROPE_BASE = 10000.0
ATTN_SCALE = HEAD_DIM ** -0.5
S_WIDTH = 1024
S_GROUP_CH = 16
S_GROUPS = 64
S_STATE = 64
S_CHUNK = 32
S_GROUP_BATCH = 8
ODD_IN = C_WIDTH + 2 * KV_WIDTH + S_WIDTH

N_EXPERTS = 16
N_EXPERT_GROUPS = 4
EXPERTS_PER_GROUP = 4
TOP_K = 2
D_EXPERT = 1408
MOE_BM = 256

TM = 256
VMEM_LIMIT = 56 * 1024 * 1024


def _cparams(n_axes=1, vmem=VMEM_LIMIT):
    return pltpu.CompilerParams(dimension_semantics=("arbitrary",) * n_axes, vmem_limit_bytes=vmem)


def _dot(a, b):
    return jnp.dot(a, b, preferred_element_type=F32)


def _dot_nt(a, b):
    return lax.dot_general(a, b, (((1,), (1,)), ((), ())), preferred_element_type=F32)


ADALN_TN = 1024


def _adaln_kernel(cond_ref, w_ref, b_ref, o_ref):
    c = cond_ref[...]
    s = (c * jax.nn.sigmoid(c)).astype(BF16)
    o_ref[0] = _dot(s, w_ref[0].astype(BF16)) + b_ref[0]


def adaln_all(cond8, w_mod, b_mod):
    depth, d, n6 = w_mod.shape
    return pl.pallas_call(
        _adaln_kernel,
        grid=(depth, n6 // ADALN_TN),
        in_specs=[
            pl.BlockSpec((8, d), lambda l, j: (0, 0)),
            pl.BlockSpec((1, d, ADALN_TN), lambda l, j: (l, 0, j)),
            pl.BlockSpec((1, 1, ADALN_TN), lambda l, j: (l, 0, j)),
        ],
        out_specs=pl.BlockSpec((1, 8, ADALN_TN), lambda l, j: (l, 0, j)),
        out_shape=jax.ShapeDtypeStruct((depth, 8, n6), F32),
        compiler_params=_cparams(2),
        name="adaln",
    )(cond8, w_mod, b_mod.reshape(depth, 1, n6))


def _with_casts(kernel_fn, casts, n_steps, n_lead, n_out):
    k = len(casts)
    in_specs, out_specs, out_shapes = [], [], []
    steps = 1 << (n_steps.bit_length() - 1)
    for w, layer in casts:
        rows = w.shape[0] // DEPTH
        assert rows % (steps * 16) == 0
        blk = (rows // steps, w.shape[1])
        in_specs.append(pl.BlockSpec(blk, lambda i, *_, base=layer * steps: (base + jnp.minimum(i, steps - 1), 0)))
        out_specs.append(pl.BlockSpec(blk, lambda i, *_: (jnp.minimum(i, steps - 1), 0)))
        out_shapes.append(jax.ShapeDtypeStruct((rows, w.shape[1]), BF16))

    def wrapped(*refs):
        lead, rest = refs[:n_lead], refs[n_lead:]
        cast_in, rest = rest[:k], rest[k:]
        outs, rest = rest[:n_out], rest[n_out:]
        cast_out, scratch = rest[:k], rest[k:]
        kernel_fn(*lead, *outs, *scratch)
        for src, dst in zip(cast_in, cast_out):
            dst[...] = src[...].astype(BF16)

    return wrapped, in_specs, out_specs, out_shapes, [w for w, _ in casts]


def _mod_spec(n_ctx_blocks):
    return pl.BlockSpec((1, 6, D_MODEL), lambda i, *_: (jnp.where(i < n_ctx_blocks, 1, 0), 0, 0))


def _rms_mod(x, g, shift, scale):
    ms = jnp.mean(x * x, axis=-1, keepdims=True)
    y = x * lax.rsqrt(ms + EPS) * g
    return y * (1.0 + scale) + shift


SLAB_DATA = D_MODEL // 128
SLAB = SLAB_DATA + 4


def _slab_to_rows(ref, n_rows):
    return jnp.concatenate([ref[pl.ds(c, n_rows, stride=SLAB), :] for c in range(SLAB_DATA)], axis=1)


def _rows_to_slab(ref, val, n_rows, c0):
    for c in range(val.shape[1] // 128):
        ref[pl.ds(c0 + c, n_rows, stride=SLAB), :] = val[:, c * 128:(c + 1) * 128]


def _zero_slab_padding(ref, n_rows):
    for c in range(SLAB_DATA, SLAB):
        ref[pl.ds(c, n_rows, stride=SLAB), :] = jnp.zeros((n_rows, 128), F32)


def _expert_row_gather(pos_ref, y_hbm, gbuf, sem, tile, slot, nt, start):
    if not start:
        for k in range(TOP_K):
            pltpu.make_async_copy(y_hbm.at[pl.ds(0, TM * SLAB), :], gbuf.at[slot, k], sem.at[slot]).wait()
        return

    def body(t, carry):
        for k in range(TOP_K):
            p = pl.multiple_of(pos_ref[k * nt + tile * TM + t] * SLAB, SLAB)
            pltpu.make_async_copy(y_hbm.at[pl.ds(p, SLAB), :],
                                  gbuf.at[slot, k, pl.ds(pl.multiple_of(t * SLAB, SLAB), SLAB), :],
                                  sem.at[slot]).start()
        return carry
    lax.fori_loop(0, TM, body, 0, unroll=8)


def _moe_combine(pos_ref, y_hbm, wt_ref, gbuf, sem, nt, tile0):
    i = pl.program_id(0)
    slot = i % 2

    @pl.when(i == 0)
    def _():
        _expert_row_gather(pos_ref, y_hbm, gbuf, sem, tile0, 0, nt, True)

    @pl.when(i + 1 < pl.num_programs(0))
    def _():
        _expert_row_gather(pos_ref, y_hbm, gbuf, sem, tile0 + i + 1, 1 - slot, nt, True)

    _expert_row_gather(pos_ref, y_hbm, gbuf, sem, tile0 + i, slot, nt, False)
    wt = wt_ref[...]
    return (wt[:, 0:1] * _slab_to_rows(gbuf.at[slot, 0], TM) + wt[:, 1:2] * _slab_to_rows(gbuf.at[slot, 1], TM))


_COMBINE_SCRATCH = [pltpu.VMEM((2, TOP_K, TM * SLAB, 128), F32), pltpu.SemaphoreType.DMA((2,))]


INPROJ_NC = 512


def _inproj_kernel(*refs, combine, nt):
    if combine:
        pos_ref, x_ref, y_hbm, wt_ref, modp_ref, g_ref, mod_ref, w_ref, xo_ref, p_ref, gbuf, sem = refs
        x = x_ref[...] + modp_ref[0][5:6] * _moe_combine(pos_ref, y_hbm, wt_ref, gbuf, sem, nt, 0)
        xo_ref[...] = x
    else:
        x_ref, g_ref, mod_ref, w_ref, p_ref = refs
        x = x_ref[...]
    m = mod_ref[0]
    h = _rms_mod(x, g_ref[...], m[0:1], m[1:2]).astype(BF16)
    n = w_ref.shape[1]
    for j in range(0, n, INPROJ_NC):
        p_ref[:, j:j + INPROJ_NC] = _dot(h, w_ref[:, j:j + INPROJ_NC])


def inproj(x, g, mods_l, w_bf16, n_ctx_blocks, moe=None, mods_prev=None, casts=()):
    nt, d = x.shape
    n = w_bf16.shape[1]
    nblk = nt // TM
    row = pl.BlockSpec((TM, d), lambda i, *_: (i, 0))
    g_spec = pl.BlockSpec((1, d), lambda i, *_: (0, 0))
    w_spec = pl.BlockSpec((d, n), lambda i, *_: (0, 0), pipeline_mode=pl.Buffered(1))
    p_spec = pl.BlockSpec((TM, n), lambda i, *_: (i, 0))
    p_shape = jax.ShapeDtypeStruct((nt, n), F32)
    if moe is None:
        kern, c_in, c_out, c_shape, c_args = _with_casts(functools.partial(_inproj_kernel, combine=False, nt=nt),
                                                 casts, nblk, 4, 1)
        res = pl.pallas_call(
            kern,
            grid=(nblk,),
            in_specs=[row, g_spec, _mod_spec(n_ctx_blocks), w_spec] + c_in,
            out_specs=[p_spec] + c_out,
            out_shape=[p_shape] + c_shape,
            compiler_params=_cparams(),
            name="inproj",
        )(x, g.reshape(1, d), mods_l, w_bf16, *c_args)
        return None, res[0], res[1:]
    assert not casts
    y_sorted, pos, wts_t = moe
    grid_spec = pltpu.PrefetchScalarGridSpec(
        num_scalar_prefetch=1,
        grid=(nblk,),
        in_specs=[row, pl.BlockSpec(memory_space=pl.ANY), pl.BlockSpec((TM, 8), lambda i, *_: (i, 0)),
                  _mod_spec(n_ctx_blocks), g_spec, _mod_spec(n_ctx_blocks), w_spec],
        out_specs=[row, p_spec],
        scratch_shapes=_COMBINE_SCRATCH,
    )
    x_new, p = pl.pallas_call(
        functools.partial(_inproj_kernel, combine=True, nt=nt),
        grid_spec=grid_spec,
        out_shape=[jax.ShapeDtypeStruct((nt, d), F32), p_shape],
        compiler_params=_cparams(),
        name="combine_inproj",
    )(pos, x, y_sorted, wts_t, mods_prev, g.reshape(1, d), mods_l, w_bf16)
    return x_new, p, []


CONV_RC = 64


def _layer_norm(x, g, b):
    mu = jnp.mean(x, axis=-1, keepdims=True)
    xc = x - mu
    var = jnp.mean(xc * xc, axis=-1, keepdims=True)
    return xc * lax.rsqrt(var + EPS) * g + b


def _even_kernel(u_ref, v_ref, a_ref, g_ref, ap_ref, gp_ref, an_ref, gn_ref,
                 lng_ref, lnb_ref, ws_ref, bs_ref, cw_ref, cb_ref, clg_ref, clb_ref,
                 ya_ref, yb_ref, hpad_ref, cacc_ref, shift_ref, *, n_ctx_blocks, n_blocks):
    i = pl.program_id(0)
    for c in range(TM // CHUNK):
        rows = slice(c * CHUNK, (c + 1) * CHUNK)
        vn = _layer_norm(jax.nn.gelu(v_ref[rows, :]), lng_ref[...], lnb_ref[...]).astype(BF16)
        for grp in range(A_GROUPS):
            cols = slice(grp * CHUNK, (grp + 1) * CHUNK)
            mixed = _dot(ws_ref[grp], vn[:, cols]) + bs_ref[grp]
            ya_ref[rows, cols] = (jax.nn.gelu(u_ref[rows, cols]) * mixed).astype(BF16)
    first = jnp.logical_or(i == 0, i == n_ctx_blocks)
    last = jnp.logical_or(i == n_ctx_blocks - 1, i == n_blocks - 1)
    hpad_ref[0:CONV_HALO, :] = jnp.where(first, 0.0, ap_ref[...] * jax.nn.sigmoid(gp_ref[...]))
    hpad_ref[CONV_HALO:CONV_HALO + TM, :] = a_ref[...] * jax.nn.sigmoid(g_ref[...])
    hpad_ref[CONV_HALO + TM:, :] = jnp.where(last, 0.0, an_ref[...] * jax.nn.sigmoid(gn_ref[...]))
    off = CONV_HALO - CONV_WIDTH // 2
    n_sh = shift_ref.shape[1]
    for cc in range(B_WIDTH // 128):
        cols = slice(cc * 128, (cc + 1) * 128)
        for j in range(1, 8):
            shift_ref[j - 1, :, cols] = hpad_ref[j:j + n_sh, cols]
    for cc in range(B_WIDTH // 128):
        cols = slice(cc * 128, (cc + 1) * 128)
        for rc in range(TM // CONV_RC):
            acc = jnp.zeros((CONV_RC, 128), F32)
            for k in range(CONV_WIDTH):
                q, j = divmod(k + off, 8)
                r0 = rc * CONV_RC + 8 * q
                tap = hpad_ref[r0:r0 + CONV_RC, cols] if j == 0 else shift_ref[j - 1, r0:r0 + CONV_RC, cols]
                acc = acc + cw_ref[k:k + 1, cols] * tap
            cacc_ref[rc * CONV_RC:(rc + 1) * CONV_RC, cols] = acc
    hc = _layer_norm(cacc_ref[...] + cb_ref[...], clg_ref[...], clb_ref[...])
    yb_ref[...] = (hc * jax.nn.sigmoid(hc)).astype(BF16)


def even_mixer(p, ln_g, ln_b, ws_bf16, bs_full, conv_w, conv_b, cln_g, cln_b, n_ctx_blocks, casts=()):
    nt = p.shape[0]
    nblk = nt // TM
    kern, c_in, c_out, c_shape, c_args = _with_casts(
        functools.partial(_even_kernel, n_ctx_blocks=n_ctx_blocks, n_blocks=nblk), casts, nblk, 16, 2)
    hb = TM // CONV_HALO
    last_h = nt // CONV_HALO - 1
    col = lambda j: pl.BlockSpec((TM, 1024), lambda i: (i, j))
    prev = lambda j: pl.BlockSpec((CONV_HALO, 1024), lambda i: (jnp.maximum(i * hb - 1, 0), j))
    nxt = lambda j: pl.BlockSpec((CONV_HALO, 1024), lambda i: (jnp.minimum((i + 1) * hb, last_h), j))
    vec = pl.BlockSpec((1, 1024), lambda i: (0, 0))
    out = pl.BlockSpec((TM, 1024), lambda i: (i, 0))
    res = pl.pallas_call(
        kern,
        grid=(nblk,),
        in_specs=[col(0), col(1), col(2), col(3), prev(2), prev(3), nxt(2), nxt(3),
                  vec, vec,
                  pl.BlockSpec((A_GROUPS, CHUNK, CHUNK), lambda i: (0, 0, 0)),
                  pl.BlockSpec((A_GROUPS, CHUNK, CHUNK), lambda i: (0, 0, 0)),
                  pl.BlockSpec((CONV_WIDTH, 1024), lambda i: (0, 0)),
                  vec, vec, vec] + c_in,
        out_specs=[out, out] + c_out,
        out_shape=[jax.ShapeDtypeStruct((nt, 1024), BF16)] * 2 + c_shape,
        scratch_shapes=[pltpu.VMEM((TM + 2 * CONV_HALO, 1024), F32), pltpu.VMEM((TM, 1024), F32),
                        pltpu.VMEM((7, TM + 2 * CONV_HALO - 8, 1024), F32)],
        compiler_params=_cparams(),
        name="even_mixer",
    )(p, p, p, p, p, p, p, p, ln_g.reshape(1, -1), ln_b.reshape(1, -1), ws_bf16, bs_full,
      conv_w, conv_b.reshape(1, -1), cln_g.reshape(1, -1), cln_b.reshape(1, -1), *c_args)
    return res[0], res[1], res[2:]


def _second_max(a0, a1, a2, a3):
    m01, n01 = jnp.maximum(a0, a1), jnp.minimum(a0, a1)
    m23, n23 = jnp.maximum(a2, a3), jnp.minimum(a2, a3)
    return jnp.maximum(m01, m23), jnp.maximum(jnp.minimum(m01, m23), jnp.maximum(n01, n23))


def _route_tile(logits, b_col):
    aff = jax.nn.sigmoid(logits)
    biased = aff + b_col
    row = lambda m, e: m[e:e + 1, :]
    g_sel = best = None
    for g in range(N_EXPERT_GROUPS):
        top1, top2 = _second_max(*[row(biased, EXPERTS_PER_GROUP * g + j) for j in range(EXPERTS_PER_GROUP)])
        score = top1 + top2
        if g == 0:
            g_sel, best = jnp.zeros(score.shape, jnp.int32), score
        else:
            upd = score > best
            g_sel, best = jnp.where(upd, g, g_sel), jnp.where(upd, score, best)

    def in_group(m, j):
        out = row(m, (N_EXPERT_GROUPS - 1) * EXPERTS_PER_GROUP + j)
        for g in range(N_EXPERT_GROUPS - 2, -1, -1):
            out = jnp.where(g_sel == g, row(m, EXPERTS_PER_GROUP * g + j), out)
        return out

    v = [in_group(biased, j) for j in range(EXPERTS_PER_GROUP)]
    a = [in_group(aff, j) for j in range(EXPERTS_PER_GROUP)]
    i1, b1, w1 = jnp.zeros(g_sel.shape, jnp.int32), v[0], a[0]
    for j in range(1, EXPERTS_PER_GROUP):
        upd = v[j] > b1
        i1, b1, w1 = jnp.where(upd, j, i1), jnp.where(upd, v[j], b1), jnp.where(upd, a[j], w1)
    first = i1 == 0
    i2, b2, w2 = jnp.where(first, 1, 0), jnp.where(first, v[1], v[0]), jnp.where(first, a[1], a[0])
    for j in range(1, EXPERTS_PER_GROUP):
        upd = jnp.logical_and(i1 != j, v[j] > b2)
        i2, b2, w2 = jnp.where(upd, j, i2), jnp.where(upd, v[j], b2), jnp.where(upd, a[j], w2)
    den = w1 + w2
    return EXPERTS_PER_GROUP * g_sel + i1, EXPERTS_PER_GROUP * g_sel + i2, w1 / den, w2 / den


def _rows8(r0, r1):
    sub = lax.broadcasted_iota(jnp.int32, (8, r0.shape[1]), 0)
    return jnp.where(sub == 0, r0, jnp.where(sub == 1, r1, jnp.zeros_like(r0)))


def _outproj_kernel(ya_ref, yb_ref, wa_ref, wb_ref, x_ref, mod_ref, g_ref, wr_ref, br_ref, tri_ref,
                    xo_ref, f_ref, e_ref, w_ref, r_ref, cnt_ref, carry_ref):
    i = pl.program_id(0)
    m = mod_ref[0]
    y = _dot(ya_ref[...], wa_ref[...]) + _dot(yb_ref[...], wb_ref[...])
    x = x_ref[...] + m[2:3] * y
    xo_ref[...] = x
    f = _rms_mod(x, g_ref[...], m[3:4], m[4:5])
    _rows_to_slab(f_ref, f, TM, 0)
    _zero_slab_padding(f_ref, TM)
    f_hi = f.astype(BF16)
    f_lo = (f - f_hi.astype(F32)).astype(BF16)
    wr = wr_ref[...]
    w_hi = wr.astype(BF16)
    w_lo = (wr - w_hi.astype(F32)).astype(BF16)
    logits = _dot_nt(w_hi, f_hi) + _dot_nt(w_lo, f_hi) + _dot_nt(w_hi, f_lo)
    e0, e1, w0, w1 = _route_tile(logits, br_ref[:, 0:1])

    @pl.when(i == 0)
    def _():
        carry_ref[...] = jnp.zeros(carry_ref.shape, F32)

    sub = lax.broadcasted_iota(jnp.int32, (N_EXPERTS, TM), 0)
    hot0, hot1 = sub == e0, sub == e1
    member = jnp.where(jnp.logical_or(hot0, hot1), 1.0, 0.0)
    before = _dot(member.astype(BF16), tri_ref[...]) + carry_ref[:, 0:1]
    r0 = jnp.sum(jnp.where(hot0, before, 0.0), axis=0, keepdims=True)
    r1 = jnp.sum(jnp.where(hot1, before, 0.0), axis=0, keepdims=True)
    carry_ref[...] = carry_ref[...] + jnp.sum(member, axis=1, keepdims=True)
    e_ref[...] = _rows8(e0, e1)
    w_ref[...] = _rows8(w0, w1)
    r_ref[...] = _rows8(r0.astype(jnp.int32), r1.astype(jnp.int32))
    cnt_ref[...] = carry_ref[...]


def outproj(ya, yb, wa, wb, x, mods_l, g_ffn, w_router_t, b_router_col, n_ctx_blocks, casts=()):
    nt, d = x.shape
    nblk = nt // TM
    kern, c_in, c_out, c_shape, c_args = _with_casts(_outproj_kernel, casts, nblk, 10, 6)
    half = pl.BlockSpec((TM, 1024), lambda i: (i, 0))
    wsp = pl.BlockSpec((1024, d), lambda i: (0, 0), pipeline_mode=pl.Buffered(1))
    row = pl.BlockSpec((TM, d), lambda i: (i, 0))
    r8 = pl.BlockSpec((8, TM), lambda i: (0, i))
    cnt = pl.BlockSpec((N_EXPERTS, 128), lambda i: (0, 0))
    tri = (jnp.arange(TM)[:, None] < jnp.arange(TM)[None, :]).astype(BF16)
    res = pl.pallas_call(
        kern,
        grid=(nblk,),
        in_specs=[half, half, wsp, wsp, row, _mod_spec(n_ctx_blocks),
                  pl.BlockSpec((1, d), lambda i: (0, 0)),
                  pl.BlockSpec((N_EXPERTS, d), lambda i: (0, 0)), cnt,
                  pl.BlockSpec((TM, TM), lambda i: (0, 0))] + c_in,
        out_specs=[row, pl.BlockSpec((TM * SLAB, 128), lambda i: (i, 0)), r8, r8, r8, cnt] + c_out,
        out_shape=[jax.ShapeDtypeStruct((nt, d), F32), jax.ShapeDtypeStruct((nt * SLAB, 128), F32),
                   jax.ShapeDtypeStruct((8, nt), jnp.int32), jax.ShapeDtypeStruct((8, nt), F32),
                   jax.ShapeDtypeStruct((8, nt), jnp.int32), jax.ShapeDtypeStruct((N_EXPERTS, 128), F32)] + c_shape,
        scratch_shapes=[pltpu.VMEM((N_EXPERTS, 128), F32)],
        compiler_params=_cparams(),
        name="outproj",
    )(ya, yb, wa, wb, x, mods_l, g_ffn.reshape(1, d), w_router_t, b_router_col, tri, *c_args)
    return res[:6], res[6:]


N_LOC = 3 * ATTN_BLOCK


def _rope(x, cos, sin):
    lane = lax.broadcasted_iota(jnp.int32, x.shape, 1)
    swapped = jnp.where(lane % 64 < 32, pltpu.roll(x, 96, axis=1), pltpu.roll(x, 32, axis=1))
    return x * cos + swapped * sin


def _attn_kernel(sink_ref, q_ref, kp_ref, kc_ref, kn_ref, vp_ref, vc_ref, vn_ref, kx_ref, vx_ref,
                 cosp_ref, cosc_ref, cosn_ref, sinp_ref, sinc_ref, sinn_ref, o_ref,
                 qs_ref, kbuf_ref, vbuf_ref, *, n_ctx_blocks, n_blocks, n_ctx):
    i = pl.program_id(0)
    nkeys = N_LOC + n_ctx
    nq = Q_PER_KV * ATTN_BLOCK
    is_lat = jnp.where(i >= n_ctx_blocks, 1, 0)
    prev_ok = jnp.where(i - 1 >= n_ctx_blocks, is_lat, 0)
    next_ok = jnp.where(i + 1 <= n_blocks - 1, is_lat, 0)
    qi = lax.broadcasted_iota(jnp.int32, (nq, nkeys), 0) & (ATTN_BLOCK - 1)
    kj = lax.broadcasted_iota(jnp.int32, (nq, nkeys), 1)
    rel = kj - ATTN_BLOCK - qi
    blk_ok = jnp.where(kj < ATTN_BLOCK, prev_ok, jnp.where(kj < 2 * ATTN_BLOCK, is_lat, next_ok))
    rel = jnp.where(blk_ok > 0, rel, WINDOW + 1)
    valid = jnp.logical_or(kj >= N_LOC, jnp.logical_and(rel >= -WINDOW, rel <= WINDOW))
    rowh = lax.broadcasted_iota(jnp.int32, (nq, 1), 0) // ATTN_BLOCK
    for h in range(N_KV_HEADS):
        hc = slice(h * HEAD_DIM, (h + 1) * HEAD_DIM)
        kbuf_ref[0:ATTN_BLOCK, :] = _rope(kp_ref[:, hc], cosp_ref[...], sinp_ref[...]).astype(BF16)
        kbuf_ref[ATTN_BLOCK:2 * ATTN_BLOCK, :] = _rope(kc_ref[:, hc], cosc_ref[...], sinc_ref[...]).astype(BF16)
        kbuf_ref[2 * ATTN_BLOCK:N_LOC, :] = _rope(kn_ref[:, hc], cosn_ref[...], sinn_ref[...]).astype(BF16)
        kbuf_ref[N_LOC:, :] = kx_ref[:, hc].astype(BF16)
        vbuf_ref[0:ATTN_BLOCK, :] = vp_ref[:, hc].astype(BF16)
        vbuf_ref[ATTN_BLOCK:2 * ATTN_BLOCK, :] = vc_ref[:, hc].astype(BF16)
        vbuf_ref[2 * ATTN_BLOCK:N_LOC, :] = vn_ref[:, hc].astype(BF16)
        vbuf_ref[N_LOC:, :] = vx_ref[:, hc].astype(BF16)
        sink = jnp.zeros((Q_PER_KV * ATTN_BLOCK, 1), F32)
        for gq in range(Q_PER_KV):
            head = h * Q_PER_KV + gq
            qc = slice(head * HEAD_DIM, (head + 1) * HEAD_DIM)
            qs_ref[gq * ATTN_BLOCK:(gq + 1) * ATTN_BLOCK, :] = _rope(
                q_ref[:, qc], cosc_ref[...], sinc_ref[...]).astype(BF16)
            sink = jnp.where(rowh == gq, sink_ref[head], sink)
        s = _dot_nt(qs_ref[...], kbuf_ref[...]) * ATTN_SCALE
        s = jnp.where(valid, s, NEG_INF)
        mx = jnp.maximum(jnp.max(s, axis=-1, keepdims=True), sink)
        p = jnp.exp(s - mx)
        den = jnp.sum(p, axis=-1, keepdims=True) + jnp.exp(sink - mx)
        o = _dot(p.astype(BF16), vbuf_ref[...]) / den
        for gq in range(Q_PER_KV):
            head = h * Q_PER_KV + gq
            o_ref[:, head * HEAD_DIM:(head + 1) * HEAD_DIM] = o[gq * ATTN_BLOCK:(gq + 1) * ATTN_BLOCK].astype(BF16)


def attention(p, sink, cos_t, sin_t, n_ctx, casts=()):
    nt = p.shape[0]
    nblk = nt // ATTN_BLOCK
    ncb = n_ctx // ATTN_BLOCK
    kern, c_in, c_out, c_shape, c_args = _with_casts(
        functools.partial(_attn_kernel, n_ctx_blocks=ncb, n_blocks=nblk, n_ctx=n_ctx), casts, nblk, 16, 1)
    kcol, vcol = 2048 // KV_WIDTH, 2048 // KV_WIDTH + 1
    pm = lambda i: jnp.maximum(i - 1, 0)
    nx = lambda i: jnp.minimum(i + 1, nblk - 1)
    kv = lambda f, c: pl.BlockSpec((ATTN_BLOCK, KV_WIDTH), lambda i, s: (f(i), c))
    tab = lambda f: pl.BlockSpec((ATTN_BLOCK, HEAD_DIM), lambda i, s: (f(i), 0))
    same = lambda i: i
    grid_spec = pltpu.PrefetchScalarGridSpec(
        num_scalar_prefetch=1,
        grid=(nblk,),
        in_specs=[pl.BlockSpec((ATTN_BLOCK, C_WIDTH), lambda i, s: (i, 0)),
                  kv(pm, kcol), kv(same, kcol), kv(nx, kcol), kv(pm, vcol), kv(same, vcol), kv(nx, vcol),
                  pl.BlockSpec((n_ctx, KV_WIDTH), lambda i, s: (0, kcol)),
                  pl.BlockSpec((n_ctx, KV_WIDTH), lambda i, s: (0, vcol)),
                  tab(pm), tab(same), tab(nx), tab(pm), tab(same), tab(nx)] + c_in,
        out_specs=[pl.BlockSpec((ATTN_BLOCK, C_WIDTH), lambda i, s: (i, 0))] + c_out,
        scratch_shapes=[pltpu.VMEM((Q_PER_KV * ATTN_BLOCK, HEAD_DIM), BF16),
                        pltpu.VMEM((N_LOC + n_ctx, HEAD_DIM), BF16),
                        pltpu.VMEM((N_LOC + n_ctx, HEAD_DIM), BF16)],
    )
    res = pl.pallas_call(
        kern,
        grid_spec=grid_spec,
        out_shape=[jax.ShapeDtypeStruct((nt, C_WIDTH), BF16)] + c_shape,
        compiler_params=_cparams(),
        name="attention",
    )(sink, p, p, p, p, p, p, p, p, p, cos_t, cos_t, cos_t, sin_t, sin_t, sin_t, *c_args)
    return res[0], res[1:]


def rope_tables(n_ctx, n_lat):
    rows = n_lat // GRID_W
    row = jnp.broadcast_to(jnp.arange(rows, dtype=F32)[:, None], (rows, GRID_W)).reshape(-1)
    col = jnp.broadcast_to(jnp.arange(GRID_W, dtype=F32)[None, :], (rows, GRID_W)).reshape(-1)
    half = HEAD_DIM // 2
    inv_freq = ROPE_BASE ** (-jnp.arange(0, half, 2, dtype=F32) / half)
    ang_r = row[:, None] * inv_freq
    ang_c = col[:, None] * inv_freq
    cos = jnp.concatenate([jnp.cos(ang_r), jnp.cos(ang_r), jnp.cos(ang_c), jnp.cos(ang_c)], axis=-1)
    sin = jnp.concatenate([-jnp.sin(ang_r), jnp.sin(ang_r), -jnp.sin(ang_c), jnp.sin(ang_c)], axis=-1)
    cos = jnp.concatenate([jnp.ones((n_ctx, HEAD_DIM), F32), cos], axis=0)
    sin = jnp.concatenate([jnp.zeros((n_ctx, HEAD_DIM), F32), sin], axis=0)
    return cos, sin


S_LH = S_CHUNK * S_GROUP_CH
S_HALF = S_GROUP_BATCH * 128


S_PW_ROWS = 4 * S_CHUNK
S_PITCH = S_CHUNK + 8


def _dot3_nt(x, a):
    x_hi, a_hi = x.astype(BF16), a.astype(BF16)
    x_lo, a_lo = (x - x_hi.astype(F32)).astype(BF16), (a - a_hi.astype(F32)).astype(BF16)
    return _dot_nt(x_hi, a_hi) + _dot_nt(x_lo, a_hi) + _dot_nt(x_hi, a_lo)


def _s5_kernel(u_ref, pw_ref, bc_ref, lam_ref, y_ref, s_ref, hf_ref, hr_ref,
               ws_ref, wo_ref, are_ref, aim_ref, t_ref, ug_ref, yg_ref, upad_ref, ypad_ref, *,
               n_chunks, n_ctx_chunks):
    L = S_CHUNK
    gpt = 128 // S_GROUP_CH
    ug_ref[:, n_chunks:, :] = jnp.zeros((S_GROUP_BATCH, ug_ref.shape[1] - n_chunks, S_LH), F32)

    def spread(c, carry):
        upad_ref[pl.ds(pl.multiple_of(c * S_PITCH, 8), L), :] = u_ref[pl.ds(pl.multiple_of(c * L, L), L), :]
        return carry
    lax.fori_loop(0, n_chunks, spread, 0, unroll=8)
    for s_tok in range(L):
        x = upad_ref[pl.ds(s_tok, n_chunks, stride=S_PITCH), :]
        dst = s_tok % gpt
        for g in range(S_GROUP_BATCH):
            k = (dst - g) % gpt
            r = x if k == 0 else pltpu.roll(x, k * S_GROUP_CH, axis=1)
            ug_ref[g, 0:n_chunks, s_tok * S_GROUP_CH:(s_tok + 1) * S_GROUP_CH] = (
                r[:, dst * S_GROUP_CH:(dst + 1) * S_GROUP_CH])

    def scaled_rows(g, x_re, x_im, row0, n_rows, emit):
        def body(r, carry):
            p_re = pw_ref[g, 0, pl.ds(row0 + r, 1), :]
            p_im = pw_ref[g, 1, pl.ds(row0 + r, 1), :]
            emit(pl.ds(pl.multiple_of(r * S_GROUP_CH, S_GROUP_CH), S_GROUP_CH),
                 x_re * p_re - x_im * p_im, x_re * p_im + x_im * p_re)
            return carry
        lax.fori_loop(0, n_rows, body, 0, unroll=4)

    def emit_ws(rows, re, im):
        ws_ref[rows, 0:128] = re.astype(BF16)
        ws_ref[rows, 128:256] = im.astype(BF16)

    def emit_wo(rows, re, im):
        wo_ref[rows, 0:128] = re.astype(BF16)
        wo_ref[rows, 128:256] = (-im).astype(BF16)

    def emit_a(rows, re, im):
        are_ref[rows, :] = re
        aim_ref[rows, :] = im

    for g in range(S_GROUP_BATCH):
        scaled_rows(g, bc_ref[g, 0], bc_ref[g, 1], 0, L, emit_ws)
        s = _dot(ug_ref[g].astype(BF16), ws_ref[...])
        s_ref[:, g * 128:(g + 1) * 128] = s[:, 0:128]
        s_ref[:, S_HALF + g * 128:S_HALF + (g + 1) * 128] = s[:, 128:256]
    hf_ref[...] = jnp.zeros(hf_ref.shape, F32)
    hr_ref[...] = jnp.zeros(hr_ref.shape, F32)
    lam = lam_ref[0]
    lam_re, lam_im = lam[:, :S_HALF], lam[:, S_HALF:]
    is_fwd = lax.broadcasted_iota(jnp.int32, (1, 2 * S_HALF), 1) % 128 < S_STATE

    def step(k, state):
        st_re, st_im = state
        cf = k
        cr = jnp.where(k < n_ctx_chunks, n_ctx_chunks - 1 - k, n_chunks - 1 - (k - n_ctx_chunks))
        st = jnp.concatenate([st_re, st_im], axis=1)
        hf_ref[pl.ds(cf, 1), :] = st
        hr_ref[pl.ds(cr, 1), :] = st
        s_in = jnp.where(is_fwd, s_ref[pl.ds(cf, 1), :], s_ref[pl.ds(cr, 1), :])
        new_re = lam_re * st_re - lam_im * st_im + s_in[:, :S_HALF]
        new_im = lam_re * st_im + lam_im * st_re + s_in[:, S_HALF:]
        return new_re, new_im

    zero = jnp.zeros((1, S_HALF), F32)
    lax.fori_loop(0, n_chunks, step, (zero, zero))
    hin = jnp.where(is_fwd, hf_ref[...], hr_ref[...]).astype(BF16)
    kwidth = 2 * L * S_GROUP_CH
    for g in range(S_GROUP_BATCH):
        hin_g = jnp.concatenate([hin[:, g * 128:(g + 1) * 128],
                                 hin[:, S_HALF + g * 128:S_HALF + (g + 1) * 128]], axis=1)
        c_re, c_im = bc_ref[g, 2], bc_ref[g, 3]
        scaled_rows(g, c_re, c_im, L, L, emit_wo)
        scaled_rows(g, c_re, c_im, 2 * L, 2 * L, emit_a)
        kern = _dot3_nt(bc_ref[g, 0], are_ref[...]) - _dot3_nt(bc_ref[g, 1], aim_ref[...])
        for s_tok in range(L):
            off = (L - 1 - s_tok) * S_GROUP_CH
            win = kern if off == 0 else pltpu.roll(kern, kwidth - off, axis=1)
            t_ref[s_tok * S_GROUP_CH:(s_tok + 1) * S_GROUP_CH, :] = win[:, :S_LH].astype(BF16)
        yg_ref[g] = _dot(ug_ref[g].astype(BF16), t_ref[...]) + _dot_nt(hin_g, wo_ref[...])
    lane_group = lax.broadcasted_iota(jnp.int32, (n_chunks, 128), 1) // S_GROUP_CH
    for s_tok in range(L):
        dst = s_tok % gpt
        cols = slice((s_tok // gpt) * 128, (s_tok // gpt + 1) * 128)
        z = None
        for g in range(S_GROUP_BATCH):
            piece = yg_ref[g, 0:n_chunks, cols]
            k = (g - dst) % gpt
            r = piece if k == 0 else pltpu.roll(piece, k * S_GROUP_CH, axis=1)
            z = r if z is None else jnp.where(lane_group == g, r, z)
        ypad_ref[pl.ds(s_tok, n_chunks, stride=S_PITCH), :] = z

    def pack(c, carry):
        y_ref[pl.ds(pl.multiple_of(c * L, L), L), :] = ypad_ref[pl.ds(pl.multiple_of(c * S_PITCH, 8), L), :]
        return carry
    lax.fori_loop(0, n_chunks, pack, 0, unroll=8)


def s5_scan(p, pw_tab, bc_tab, lam_rows, n_chunks, n_ctx_chunks):
    nt = p.shape[0]
    assert S_GROUP_BATCH * S_GROUP_CH == 128 and nt == n_chunks * S_CHUNK
    ncp = -(-n_chunks // 16) * 16
    lh = S_LH
    return pl.pallas_call(
        functools.partial(_s5_kernel, n_chunks=n_chunks, n_ctx_chunks=n_ctx_chunks),
        grid=(S_GROUPS // S_GROUP_BATCH,),
        in_specs=[pl.BlockSpec((nt, 128), lambda i: (0, C_WIDTH // 128 + i)),
                  pl.BlockSpec((S_GROUP_BATCH, 2, S_PW_ROWS, 128), lambda i: (i, 0, 0, 0)),
                  pl.BlockSpec((S_GROUP_BATCH, 4, S_GROUP_CH, 128), lambda i: (i, 0, 0, 0)),
                  pl.BlockSpec((1, 1, 2 * S_HALF), lambda i: (i, 0, 0))],
        out_specs=pl.BlockSpec((nt, 128), lambda i: (0, i)),
        out_shape=jax.ShapeDtypeStruct((nt, S_WIDTH), F32),
        scratch_shapes=[pltpu.VMEM((ncp, 2 * S_HALF), F32)] * 3 + [
            pltpu.VMEM((lh, 256), BF16), pltpu.VMEM((lh, 256), BF16),
            pltpu.VMEM((2 * lh, 128), F32), pltpu.VMEM((2 * lh, 128), F32), pltpu.VMEM((lh, lh), BF16),
            pltpu.VMEM((S_GROUP_BATCH, ncp, lh), F32), pltpu.VMEM((S_GROUP_BATCH, ncp, lh), F32),
            pltpu.VMEM((n_chunks * S_PITCH, 128), F32), pltpu.VMEM((n_chunks * S_PITCH, 128), F32)],
        compiler_params=_cparams(),
        name="s5_scan",
    )(p, pw_tab, bc_tab, lam_rows)


def s5_weights(lam_re, lam_im, log_dt, b_re, b_im, c_re, c_im):
    L = S_CHUNK
    lam = lax.complex(lam_re.astype(F32), lam_im.astype(F32))
    lam_dt = lam * jnp.exp(log_dt.astype(F32))[..., None]
    lam_bar = jnp.exp(lam_dt)
    b_bar = ((lam_bar - 1.0) / lam)[..., None] * lax.complex(b_re.astype(F32), b_im.astype(F32))
    c = lax.complex(c_re.astype(F32), c_im.astype(F32))
    n = jnp.arange(L, dtype=F32)
    lag = jnp.arange(2 * L, dtype=F32) - (L - 1)
    expo = jnp.concatenate([jnp.stack([L - 1 - n, n], axis=-1), jnp.stack([n + 1, L - n], axis=-1),
                            jnp.stack([lag, -lag], axis=-1)], axis=0)
    live = jnp.logical_and(expo >= 0, (jnp.arange(4 * L) < 4 * L - 1)[:, None])
    pw = jnp.where(live[:, :, None, None], jnp.exp(lam_dt[None] * jnp.maximum(expo, 0.0)[:, :, None, None]), 0.0)
    pw = jnp.transpose(pw, (2, 0, 1, 3)).reshape(S_GROUPS, S_PW_ROWS, 2 * S_STATE)
    pw_tab = jnp.stack([pw.real, pw.imag], axis=1)
    bt = jnp.transpose(b_bar, (1, 3, 0, 2)).reshape(S_GROUPS, S_GROUP_CH, 2 * S_STATE)
    ct = jnp.transpose(c, (1, 2, 0, 3)).reshape(S_GROUPS, S_GROUP_CH, 2 * S_STATE)
    bc_tab = jnp.stack([bt.real, bt.imag, ct.real, ct.imag], axis=1)
    lam_l = jnp.exp(lam_dt * L)
    nb = S_GROUPS // S_GROUP_BATCH
    lre = jnp.concatenate([lam_l[0].real, lam_l[1].real], axis=-1).reshape(nb, 1, S_HALF)
    lim = jnp.concatenate([lam_l[0].imag, lam_l[1].imag], axis=-1).reshape(nb, 1, S_HALF)
    return pw_tab, bc_tab, jnp.concatenate([lre, lim], axis=-1)


def _glu_kernel(y_ref, u_ref, d_ref, w_ref, b_ref, o_ref):
    z = jax.nn.gelu(y_ref[...] + d_ref[...] * u_ref[...])
    gate = jax.nn.sigmoid(_dot(z.astype(BF16), w_ref[...]) + b_ref[...])
    o_ref[...] = (z * gate).astype(BF16)


def s5_glu(y_ssm, p, d_skip, glu_w_bf16, glu_b, casts=()):
    nt = y_ssm.shape[0]
    vec = pl.BlockSpec((1, S_WIDTH), lambda i: (0, 0))
    kern, c_in, c_out, c_shape, c_args = _with_casts(_glu_kernel, casts, nt // TM, 5, 1)
    res = pl.pallas_call(
        kern,
        grid=(nt // TM,),
        in_specs=[pl.BlockSpec((TM, S_WIDTH), lambda i: (i, 0)),
                  pl.BlockSpec((TM, S_WIDTH), lambda i: (i, 1)),
                  vec, pl.BlockSpec((S_WIDTH, S_WIDTH), lambda i: (0, 0)), vec] + c_in,
        out_specs=[pl.BlockSpec((TM, S_WIDTH), lambda i: (i, 0))] + c_out,
        out_shape=[jax.ShapeDtypeStruct((nt, S_WIDTH), BF16)] + c_shape,
        compiler_params=_cparams(),
        name="s5_glu",
    )(y_ssm, p, d_skip.reshape(1, -1), glu_w_bf16, glu_b.reshape(1, -1), *c_args)
    return res[0], res[1:]


MOE_FCHUNKS = ((0, 512), (512, 512), (1024, 384))
MOE_YC = 512


def _dispatch_kernel(pos_ref, cnt_ref, pad_ref, start_ref, nu_ref, f_ref, xs_hbm, zero_ref, sem, zsem, *,
                     nt, n_blocks):
    i = pl.program_id(0)
    blk_rows = MOE_BM * SLAB

    def start_body(t, carry):
        src = f_ref.at[pl.ds(pl.multiple_of(t * SLAB, SLAB), SLAB), :]
        for k in range(TOP_K):
            p = pl.multiple_of(pos_ref[k * nt + i * TM + t] * SLAB, SLAB)
            pltpu.make_async_copy(src, xs_hbm.at[pl.ds(p, SLAB), :], sem).start()
        return carry

    def pad_copies(start):
        for e in range(N_EXPERTS):
            def body(r, carry):
                p = pl.multiple_of((start_ref[e] + r) * SLAB, SLAB)
                cp = pltpu.make_async_copy(zero_ref.at[pl.ds(0, SLAB), :], xs_hbm.at[pl.ds(p, SLAB), :], zsem)
                if start:
                    cp.start()
                else:
                    cp.wait()
                return carry
            lax.fori_loop(cnt_ref[e], pad_ref[e], body, 0)

        def tail(blk, carry):
            p = pl.multiple_of(blk * blk_rows, blk_rows)
            cp = pltpu.make_async_copy(zero_ref, xs_hbm.at[pl.ds(p, blk_rows), :], zsem)
            if start:
                cp.start()
            else:
                cp.wait()
            return carry
        lax.fori_loop(nu_ref[0], n_blocks, tail, 0)

    @pl.when(i == 0)
    def _():
        zero_ref[...] = jnp.zeros(zero_ref.shape, F32)
        pad_copies(True)

    lax.fori_loop(0, TM, start_body, 0, unroll=8)
    for _ in range(TOP_K):
        pltpu.make_async_copy(f_ref, xs_hbm.at[pl.ds(0, TM * SLAB), :], sem).wait()

    @pl.when(i == 0)
    def _():
        pad_copies(False)


def moe_dispatch(f_slab, pos, counts, padded, pad_start, n_used, n_blocks):
    nt = f_slab.shape[0] // SLAB
    grid_spec = pltpu.PrefetchScalarGridSpec(
        num_scalar_prefetch=5,
        grid=(nt // TM,),
        in_specs=[pl.BlockSpec((TM * SLAB, 128), lambda i, *_: (i, 0))],
        out_specs=pl.BlockSpec(memory_space=pl.ANY),
        scratch_shapes=[pltpu.VMEM((MOE_BM * SLAB, 128), F32), pltpu.SemaphoreType.DMA, pltpu.SemaphoreType.DMA],
    )
    return pl.pallas_call(
        functools.partial(_dispatch_kernel, nt=nt, n_blocks=n_blocks),
        grid_spec=grid_spec,
        out_shape=jax.ShapeDtypeStruct((n_blocks * MOE_BM * SLAB, 128), F32),
        compiler_params=_cparams(),
        name="moe_dispatch",
    )(pos, counts, padded, pad_start, n_used, f_slab)


def _moe_kernel(be_ref, nu_ref, xs_ref, wg_ref, wu_ref, wd_ref, y_ref, hbuf):
    @pl.when(pl.program_id(0) >= nu_ref[0])
    def _():
        y_ref[...] = jnp.zeros(y_ref.shape, F32)

    @pl.when(pl.program_id(0) < nu_ref[0])
    def _():
        x = _slab_to_rows(xs_ref, MOE_BM).astype(BF16)
        for f0, fw in MOE_FCHUNKS:
            hg = _dot(x, wg_ref[0, :, f0:f0 + fw])
            hu = _dot(x, wu_ref[0, :, f0:f0 + fw])
            hbuf[:, f0:f0 + fw] = (hg * jax.nn.sigmoid(hg) * hu).astype(BF16)
        for c0 in range(0, D_MODEL, MOE_YC):
            _rows_to_slab(y_ref, _dot(hbuf[...], wd_ref[0, :, c0:c0 + MOE_YC]), MOE_BM, c0 // 128)
        _zero_slab_padding(y_ref, MOE_BM)


def moe_experts(x_sorted, block_e, n_used, wg, wu, wd):
    n_blocks = block_e.shape[0]
    d, fexp = wg.shape[1], wg.shape[2]
    blk = lambda b, be, nu: (jnp.minimum(b, nu[0] - 1), 0)
    wsel = lambda b, be, nu: (be[jnp.minimum(b, nu[0] - 1)], 0, 0)
    grid_spec = pltpu.PrefetchScalarGridSpec(
        num_scalar_prefetch=2,
        grid=(n_blocks,),
        in_specs=[pl.BlockSpec((MOE_BM * SLAB, 128), blk),
                  pl.BlockSpec((1, d, fexp), wsel), pl.BlockSpec((1, d, fexp), wsel),
                  pl.BlockSpec((1, fexp, d), wsel)],
        out_specs=pl.BlockSpec((MOE_BM * SLAB, 128), lambda b, be, nu: (b, 0)),
        scratch_shapes=[pltpu.VMEM((MOE_BM, fexp), BF16)],
    )
    return pl.pallas_call(
        _moe_kernel,
        grid_spec=grid_spec,
        out_shape=jax.ShapeDtypeStruct(x_sorted.shape, F32),
        compiler_params=_cparams(),
        name="moe_experts",
    )(block_e, n_used, x_sorted, wg, wu, wd)


def moe_layout(eidx, rank, cnt):
    nt = eidx.shape[1]
    n_blocks = -(-(nt * TOP_K) // MOE_BM) + N_EXPERTS
    counts = cnt[:, 0].astype(jnp.int32)
    padded = (counts + MOE_BM - 1) // MOE_BM * MOE_BM
    pad_end = jnp.cumsum(padded)
    pad_start = pad_end - padded
    hot = eidx[:TOP_K, :, None] == jnp.arange(N_EXPERTS, dtype=jnp.int32)
    pos = (rank[:TOP_K] + jnp.sum(jnp.where(hot, pad_start, 0), axis=-1)).reshape(-1)
    blk_start = jnp.arange(n_blocks, dtype=jnp.int32) * MOE_BM
    block_e = jnp.minimum(jnp.sum(blk_start[:, None] >= pad_end[None, :], axis=1), N_EXPERTS - 1).astype(jnp.int32)
    n_used = (pad_end[-1:] // MOE_BM).astype(jnp.int32)
    return pos, counts, padded, pad_start, block_e, n_used


def _final_kernel(pos_ref, x_ref, y_hbm, wt_ref, mod_ref, g_ref, out_ref, gbuf, sem, *, nt, tile0):
    x = x_ref[...] + mod_ref[0][5:6] * _moe_combine(pos_ref, y_hbm, wt_ref, gbuf, sem, nt, tile0)
    ms = jnp.mean(x * x, axis=-1, keepdims=True)
    out_ref[...] = x * lax.rsqrt(ms + EPS) * g_ref[...]


def final_norm(x, moe, mods_l, g_final, n_ctx_blocks):
    nt, d = x.shape
    nlat = nt // TM - n_ctx_blocks
    y_sorted, pos, wts_t = moe
    grid_spec = pltpu.PrefetchScalarGridSpec(
        num_scalar_prefetch=1,
        grid=(nlat,),
        in_specs=[pl.BlockSpec((TM, d), lambda i, *_: (i + n_ctx_blocks, 0)),
                  pl.BlockSpec(memory_space=pl.ANY),
                  pl.BlockSpec((TM, 8), lambda i, *_: (i + n_ctx_blocks, 0)),
                  pl.BlockSpec((1, 6, d), lambda i, *_: (0, 0, 0)),
                  pl.BlockSpec((1, d), lambda i, *_: (0, 0))],
        out_specs=pl.BlockSpec((TM, d), lambda i, *_: (i, 0)),
        scratch_shapes=_COMBINE_SCRATCH,
    )
    return pl.pallas_call(
        functools.partial(_final_kernel, nt=nt, tile0=n_ctx_blocks),
        grid_spec=grid_spec,
        out_shape=jax.ShapeDtypeStruct((nlat * TM, d), F32),
        compiler_params=_cparams(),
        name="final_norm",
    )(pos, x, y_sorted, wts_t, mods_l, g_final.reshape(1, d))


def kernel(x, c, ctx, c_ctx, w_mod, b_mod, g_mix, g_ffn, w_in_even, w_out_even, sgu_ln_g, sgu_ln_b, sgu_w, sgu_b, conv_w, conv_b, conv_ln_g, conv_ln_b, w_in_odd, w_out_odd, attn_sink, ssm_lam_re, ssm_lam_im, ssm_log_dt, ssm_b_re, ssm_b_im, ssm_c_re, ssm_c_im, ssm_d, glu_w, glu_b, w_router, b_router, w_gate, w_up, w_down, g_final):
    bsz, n_lat, d = x.shape
    n_ctx = ctx.shape[1]
    assert bsz == 1 and d == D_MODEL and n_ctx % TM == 0 and n_lat % TM == 0
    nt = n_ctx + n_lat
    ncb = n_ctx // TM
    n_chunks = nt // S_CHUNK
    ncp = -(-n_chunks // 16) * 16

    xs = jnp.concatenate([ctx[0], x[0]], axis=0)
    cond8 = jnp.concatenate([c, c_ctx[None, :], jnp.zeros((6, d), F32)], axis=0)
    mods = adaln_all(cond8, w_mod, b_mod)[:, :2].reshape(DEPTH, 2, 6, d)
    cos_t, sin_t = rope_tables(n_ctx, n_lat)
    w_router_t = w_router.T
    b_router_col = jnp.broadcast_to(b_router.astype(F32)[:, None], (N_EXPERTS, 128))
    fexp = w_gate.shape[-1]
    w_stack = (w_gate.reshape(-1, fexp), w_up.reshape(-1, fexp), w_down.reshape(-1, d))
    expert_bf = {}

    def job(layer, which):
        return (w_stack[which], layer)

    def done(jobs, outs):
        for (layer, which), o in zip(jobs, outs):
            expert_bf[(layer, which)] = o

    moe = None
    for l in range(DEPTH):
        j = l // 2
        odd = l % 2 == 1
        if odd:
            wi = w_in_odd[j]
            w_in = jnp.concatenate([wi[:, :C_WIDTH], wi[:, C_WIDTH + 2 * KV_WIDTH:],
                                    wi[:, C_WIDTH:C_WIDTH + 2 * KV_WIDTH]], axis=1).astype(BF16)
            w_out = w_out_odd[j].astype(BF16)
        else:
            w_in = w_in_even[j].astype(BF16)
            w_out = w_out_even[j].astype(BF16)
        jobs = [(0, 0)] if l == 0 else []
        x_new, p, outs = inproj(xs, g_mix[l], mods[l], w_in, ncb, moe, mods[l - 1] if l > 0 else None,
                                casts=[job(*jb) for jb in jobs])
        done(jobs, outs)
        if x_new is not None:
            xs = x_new
        if odd:
            jobs = [(l, 0), (l, 1)]
            ya, outs = attention(p, attn_sink[j], cos_t, sin_t, n_ctx, casts=[job(*jb) for jb in jobs])
            done(jobs, outs)
            pw_tab, bc_tab, lam_rows = s5_weights(ssm_lam_re[j], ssm_lam_im[j], ssm_log_dt[j], ssm_b_re[j],
                                                  ssm_b_im[j], ssm_c_re[j], ssm_c_im[j])
            y_ssm = s5_scan(p, pw_tab, bc_tab, lam_rows, n_chunks, n_ctx // S_CHUNK)
            jobs = [(l + 1, 0)] if l + 1 < DEPTH else []
            yb, outs = s5_glu(y_ssm, p, ssm_d[j], glu_w[j].astype(BF16), glu_b[j], casts=[job(*jb) for jb in jobs])
            done(jobs, outs)
        else:
            bs_full = jnp.broadcast_to(sgu_b[j][:, :, None], (A_GROUPS, CHUNK, CHUNK)).astype(F32)
            jobs = [(l, 1)]
            ya, yb, outs = even_mixer(p, sgu_ln_g[j], sgu_ln_b[j], sgu_w[j].astype(BF16), bs_full,
                                      conv_w[j], conv_b[j], conv_ln_g[j], conv_ln_b[j], ncb,
                                      casts=[job(*jb) for jb in jobs])
            done(jobs, outs)
        jobs = [(l, 2)]
        (xs, f_slab, eidx, wts, rank, cnt), outs = outproj(
            ya, yb, w_out[:1024], w_out[1024:], xs, mods[l], g_ffn[l], w_router_t, b_router_col, ncb,
            casts=[job(*jb) for jb in jobs])
        done(jobs, outs)
        pos, counts, padded, pad_start, block_e, n_used = moe_layout(eidx, rank, cnt)
        x_sorted = moe_dispatch(f_slab, pos, counts, padded, pad_start, n_used, block_e.shape[0])
        y_sorted = moe_experts(x_sorted, block_e, n_used,
                               expert_bf[(l, 0)].reshape(N_EXPERTS, d, fexp),
                               expert_bf[(l, 1)].reshape(N_EXPERTS, d, fexp),
                               expert_bf[(l, 2)].reshape(N_EXPERTS, fexp, d))
        moe = (y_sorted, pos, wts.T)
    out = final_norm(xs, moe, mods[DEPTH - 1], g_final, ncb)
    return out.reshape(bsz, n_lat, d)
```

```python
import functools
import math

import jax
import jax.numpy as jnp
from jax import lax
from jax.experimental import pallas as pl
from jax.experimental.pallas import tpu as pltpu

F32 = jnp.float32
BF16 = jnp.bfloat16

D_MODEL = 2048
DEPTH = 4
GRID_W = 64
EPS = 1e-6
NEG_INF = -1e30

A_WIDTH = 1024
A_GROUPS = 8
CHUNK = 128
B_WIDTH = 1024
CONV_WIDTH = 31
CONV_HALO = 16

HEAD_DIM = 128
N_Q_HEADS = 8
N_KV_HEADS = 2
Q_PER_KV = 4
C_WIDTH = 1024
KV_WIDTH = 256
WINDOW = 128
ATTN_BLOCK = 128
ROPE_BASE = 10000.0
ATTN_SCALE = HEAD_DIM ** -0.5
S_WIDTH = 1024
S_GROUP_CH = 16
S_GROUPS = 64
S_STATE = 64
S_CHUNK = 32
S_GROUP_BATCH = 8
ODD_IN = C_WIDTH + 2 * KV_WIDTH + S_WIDTH

N_EXPERTS = 16
N_EXPERT_GROUPS = 4
EXPERTS_PER_GROUP = 4
TOP_K = 2
D_EXPERT = 1408
MOE_BM = 256

TM = 256
VMEM_LIMIT = 56 * 1024 * 1024


def _cparams(n_axes=1, vmem=VMEM_LIMIT):
    return pltpu.CompilerParams(dimension_semantics=("arbitrary",) * n_axes, vmem_limit_bytes=vmem)


def _dot(a, b):
    return jnp.dot(a, b, preferred_element_type=F32)


def _dot_nt(a, b):
    return lax.dot_general(a, b, (((1,), (1,)), ((), ())), preferred_element_type=F32)


ADALN_TN = 1024


def _adaln_kernel(cond_ref, w_ref, b_ref, o_ref):
    c = cond_ref[...]
    s = (c * jax.nn.sigmoid(c)).astype(BF16)
    o_ref[0] = _dot(s, w_ref[0].astype(BF16)) + b_ref[0]


def adaln_all(cond8, w_mod, b_mod):
    depth, d, n6 = w_mod.shape
    return pl.pallas_call(
        _adaln_kernel,
        grid=(depth, n6 // ADALN_TN),
        in_specs=[
            pl.BlockSpec((8, d), lambda l, j: (0, 0)),
            pl.BlockSpec((1, d, ADALN_TN), lambda l, j: (l, 0, j)),
            pl.BlockSpec((1, 1, ADALN_TN), lambda l, j: (l, 0, j)),
        ],
        out_specs=pl.BlockSpec((1, 8, ADALN_TN), lambda l, j: (l, 0, j)),
        out_shape=jax.ShapeDtypeStruct((depth, 8, n6), F32),
        compiler_params=_cparams(2),
        name="adaln",
    )(cond8, w_mod, b_mod.reshape(depth, 1, n6))


def _with_casts(kernel_fn, casts, n_steps, n_lead, n_out):
    k = len(casts)
    in_specs, out_specs, out_shapes = [], [], []
    steps = 1 << (n_steps.bit_length() - 1)
    for w, layer in casts:
        rows = w.shape[0] // DEPTH
        assert rows % (steps * 16) == 0
        blk = (rows // steps, w.shape[1])
        in_specs.append(pl.BlockSpec(blk, lambda i, *_, base=layer * steps: (base + jnp.minimum(i, steps - 1), 0)))
        out_specs.append(pl.BlockSpec(blk, lambda i, *_: (jnp.minimum(i, steps - 1), 0)))
        out_shapes.append(jax.ShapeDtypeStruct((rows, w.shape[1]), BF16))

    def wrapped(*refs):
        lead, rest = refs[:n_lead], refs[n_lead:]
        cast_in, rest = rest[:k], rest[k:]
        outs, rest = rest[:n_out], rest[n_out:]
        cast_out, scratch = rest[:k], rest[k:]
        kernel_fn(*lead, *outs, *scratch)
        for src, dst in zip(cast_in, cast_out):
            dst[...] = src[...].astype(BF16)

    return wrapped, in_specs, out_specs, out_shapes, [w for w, _ in casts]


def _mod_spec(n_ctx_blocks):
    return pl.BlockSpec((1, 6, D_MODEL), lambda i, *_: (jnp.where(i < n_ctx_blocks, 1, 0), 0, 0))


def _rms_mod(x, g, shift, scale):
    ms = jnp.mean(x * x, axis=-1, keepdims=True)
    y = x * lax.rsqrt(ms + EPS) * g
    return y * (1.0 + scale) + shift


SLAB_DATA = D_MODEL // 128
SLAB = SLAB_DATA + 4


def _slab_to_rows(ref, n_rows):
    return jnp.concatenate([ref[pl.ds(c, n_rows, stride=SLAB), :] for c in range(SLAB_DATA)], axis=1)


def _rows_to_slab(ref, val, n_rows, c0):
    for c in range(val.shape[1] // 128):
        ref[pl.ds(c0 + c, n_rows, stride=SLAB), :] = val[:, c * 128:(c + 1) * 128]


def _zero_slab_padding(ref, n_rows):
    for c in range(SLAB_DATA, SLAB):
        ref[pl.ds(c, n_rows, stride=SLAB), :] = jnp.zeros((n_rows, 128), F32)


def _expert_row_gather(pos_ref, y_hbm, gbuf, sem, tile, slot, nt, start):
    if not start:
        for k in range(TOP_K):
            pltpu.make_async_copy(y_hbm.at[pl.ds(0, TM * SLAB), :], gbuf.at[slot, k], sem.at[slot]).wait()
        return

    def body(t, carry):
        for k in range(TOP_K):
            p = pl.multiple_of(pos_ref[k * nt + tile * TM + t] * SLAB, SLAB)
            pltpu.make_async_copy(y_hbm.at[pl.ds(p, SLAB), :],
                                  gbuf.at[slot, k, pl.ds(pl.multiple_of(t * SLAB, SLAB), SLAB), :],
                                  sem.at[slot]).start()
        return carry
    lax.fori_loop(0, TM, body, 0, unroll=8)


def _moe_combine(pos_ref, y_hbm, wt_ref, gbuf, sem, nt, tile0):
    i = pl.program_id(0)
    slot = i % 2

    @pl.when(i == 0)
    def _():
        _expert_row_gather(pos_ref, y_hbm, gbuf, sem, tile0, 0, nt, True)

    @pl.when(i + 1 < pl.num_programs(0))
    def _():
        _expert_row_gather(pos_ref, y_hbm, gbuf, sem, tile0 + i + 1, 1 - slot, nt, True)

    _expert_row_gather(pos_ref, y_hbm, gbuf, sem, tile0 + i, slot, nt, False)
    wt = wt_ref[...]
    return (wt[:, 0:1] * _slab_to_rows(gbuf.at[slot, 0], TM) + wt[:, 1:2] * _slab_to_rows(gbuf.at[slot, 1], TM))


_COMBINE_SCRATCH = [pltpu.VMEM((2, TOP_K, TM * SLAB, 128), F32), pltpu.SemaphoreType.DMA((2,))]


INPROJ_NC = 512


def _inproj_kernel(*refs, combine, nt):
    if combine:
        pos_ref, x_ref, y_hbm, wt_ref, modp_ref, g_ref, mod_ref, w_ref, xo_ref, p_ref, gbuf, sem = refs
        x = x_ref[...] + modp_ref[0][5:6] * _moe_combine(pos_ref, y_hbm, wt_ref, gbuf, sem, nt, 0)
        xo_ref[...] = x
    else:
        x_ref, g_ref, mod_ref, w_ref, p_ref = refs
        x = x_ref[...]
    m = mod_ref[0]
    h = _rms_mod(x, g_ref[...], m[0:1], m[1:2]).astype(BF16)
    n = w_ref.shape[1]
    for j in range(0, n, INPROJ_NC):
        p_ref[:, j:j + INPROJ_NC] = _dot(h, w_ref[:, j:j + INPROJ_NC])


def inproj(x, g, mods_l, w_bf16, n_ctx_blocks, moe=None, mods_prev=None, casts=()):
    nt, d = x.shape
    n = w_bf16.shape[1]
    nblk = nt // TM
    row = pl.BlockSpec((TM, d), lambda i, *_: (i, 0))
    g_spec = pl.BlockSpec((1, d), lambda i, *_: (0, 0))
    w_spec = pl.BlockSpec((d, n), lambda i, *_: (0, 0), pipeline_mode=pl.Buffered(1))
    p_spec = pl.BlockSpec((TM, n), lambda i, *_: (i, 0))
    p_shape = jax.ShapeDtypeStruct((nt, n), F32)
    if moe is None:
        kern, c_in, c_out, c_shape, c_args = _with_casts(functools.partial(_inproj_kernel, combine=False, nt=nt),
                                                 casts, nblk, 4, 1)
        res = pl.pallas_call(
            kern,
            grid=(nblk,),
            in_specs=[row, g_spec, _mod_spec(n_ctx_blocks), w_spec] + c_in,
            out_specs=[p_spec] + c_out,
            out_shape=[p_shape] + c_shape,
            compiler_params=_cparams(),
            name="inproj",
        )(x, g.reshape(1, d), mods_l, w_bf16, *c_args)
        return None, res[0], res[1:]
    kern, c_in, c_out, c_shape, c_args = _with_casts(functools.partial(_inproj_kernel, combine=True, nt=nt),
                                                     casts, nblk, 8, 2)
    y_sorted, pos, wts_t = moe
    grid_spec = pltpu.PrefetchScalarGridSpec(
        num_scalar_prefetch=1,
        grid=(nblk,),
        in_specs=[row, pl.BlockSpec(memory_space=pl.ANY), pl.BlockSpec((TM, 8), lambda i, *_: (i, 0)),
                  _mod_spec(n_ctx_blocks), g_spec, _mod_spec(n_ctx_blocks), w_spec] + c_in,
        out_specs=[row, p_spec] + c_out,
        scratch_shapes=_COMBINE_SCRATCH,
    )
    res = pl.pallas_call(
        kern,
        grid_spec=grid_spec,
        out_shape=[jax.ShapeDtypeStruct((nt, d), F32), p_shape] + c_shape,
        compiler_params=_cparams(),
        name="combine_inproj",
    )(pos, x, y_sorted, wts_t, mods_prev, g.reshape(1, d), mods_l, w_bf16, *c_args)
    return res[0], res[1], res[2:]


CONV_RC = 64


def _layer_norm(x, g, b):
    mu = jnp.mean(x, axis=-1, keepdims=True)
    xc = x - mu
    var = jnp.mean(xc * xc, axis=-1, keepdims=True)
    return xc * lax.rsqrt(var + EPS) * g + b


def _even_kernel(u_ref, v_ref, a_ref, g_ref, ap_ref, gp_ref, an_ref, gn_ref,
                 lng_ref, lnb_ref, ws_ref, bs_ref, cw_ref, cb_ref, clg_ref, clb_ref,
                 ya_ref, yb_ref, hpad_ref, cacc_ref, shift_ref, *, n_ctx_blocks, n_blocks):
    i = pl.program_id(0)
    for c in range(TM // CHUNK):
        rows = slice(c * CHUNK, (c + 1) * CHUNK)
        vn = _layer_norm(jax.nn.gelu(v_ref[rows, :]), lng_ref[...], lnb_ref[...]).astype(BF16)
        for grp in range(A_GROUPS):
            cols = slice(grp * CHUNK, (grp + 1) * CHUNK)
            mixed = _dot(ws_ref[grp], vn[:, cols]) + bs_ref[grp]
            ya_ref[rows, cols] = (jax.nn.gelu(u_ref[rows, cols]) * mixed).astype(BF16)
    first = jnp.logical_or(i == 0, i == n_ctx_blocks)
    last = jnp.logical_or(i == n_ctx_blocks - 1, i == n_blocks - 1)
    hpad_ref[0:CONV_HALO, :] = jnp.where(first, 0.0, ap_ref[...] * jax.nn.sigmoid(gp_ref[...]))
    hpad_ref[CONV_HALO:CONV_HALO + TM, :] = a_ref[...] * jax.nn.sigmoid(g_ref[...])
    hpad_ref[CONV_HALO + TM:, :] = jnp.where(last, 0.0, an_ref[...] * jax.nn.sigmoid(gn_ref[...]))
    off = CONV_HALO - CONV_WIDTH // 2
    n_sh = shift_ref.shape[1]
    for cc in range(B_WIDTH // 128):
        cols = slice(cc * 128, (cc + 1) * 128)
        for j in range(1, 8):
            shift_ref[j - 1, :, cols] = hpad_ref[j:j + n_sh, cols]
    for cc in range(B_WIDTH // 128):
        cols = slice(cc * 128, (cc + 1) * 128)
        for rc in range(TM // CONV_RC):
            acc = jnp.zeros((CONV_RC, 128), F32)
            for k in range(CONV_WIDTH):
                q, j = divmod(k + off, 8)
                r0 = rc * CONV_RC + 8 * q
                tap = hpad_ref[r0:r0 + CONV_RC, cols] if j == 0 else shift_ref[j - 1, r0:r0 + CONV_RC, cols]
                acc = acc + cw_ref[k:k + 1, cols] * tap
            cacc_ref[rc * CONV_RC:(rc + 1) * CONV_RC, cols] = acc
    hc = _layer_norm(cacc_ref[...] + cb_ref[...], clg_ref[...], clb_ref[...])
    yb_ref[...] = (hc * jax.nn.sigmoid(hc)).astype(BF16)


def even_mixer(p, ln_g, ln_b, ws_bf16, bs_full, conv_w, conv_b, cln_g, cln_b, n_ctx_blocks, casts=()):
    nt = p.shape[0]
    nblk = nt // TM
    kern, c_in, c_out, c_shape, c_args = _with_casts(
        functools.partial(_even_kernel, n_ctx_blocks=n_ctx_blocks, n_blocks=nblk), casts, nblk, 16, 2)
    hb = TM // CONV_HALO
    last_h = nt // CONV_HALO - 1
    col = lambda j: pl.BlockSpec((TM, 1024), lambda i: (i, j))
    prev = lambda j: pl.BlockSpec((CONV_HALO, 1024), lambda i: (jnp.maximum(i * hb - 1, 0), j))
    nxt = lambda j: pl.BlockSpec((CONV_HALO, 1024), lambda i: (jnp.minimum((i + 1) * hb, last_h), j))
    vec = pl.BlockSpec((1, 1024), lambda i: (0, 0))
    out = pl.BlockSpec((TM, 1024), lambda i: (i, 0))
    res = pl.pallas_call(
        kern,
        grid=(nblk,),
        in_specs=[col(0), col(1), col(2), col(3), prev(2), prev(3), nxt(2), nxt(3),
                  vec, vec,
                  pl.BlockSpec((A_GROUPS, CHUNK, CHUNK), lambda i: (0, 0, 0)),
                  pl.BlockSpec((A_GROUPS, CHUNK, CHUNK), lambda i: (0, 0, 0)),
                  pl.BlockSpec((CONV_WIDTH, 1024), lambda i: (0, 0)),
                  vec, vec, vec] + c_in,
        out_specs=[out, out] + c_out,
        out_shape=[jax.ShapeDtypeStruct((nt, 1024), BF16)] * 2 + c_shape,
        scratch_shapes=[pltpu.VMEM((TM + 2 * CONV_HALO, 1024), F32), pltpu.VMEM((TM, 1024), F32),
                        pltpu.VMEM((7, TM + 2 * CONV_HALO - 8, 1024), F32)],
        compiler_params=_cparams(),
        name="even_mixer",
    )(p, p, p, p, p, p, p, p, ln_g.reshape(1, -1), ln_b.reshape(1, -1), ws_bf16, bs_full,
      conv_w, conv_b.reshape(1, -1), cln_g.reshape(1, -1), cln_b.reshape(1, -1), *c_args)
    return res[0], res[1], res[2:]


def _second_max(a0, a1, a2, a3):
    m01, n01 = jnp.maximum(a0, a1), jnp.minimum(a0, a1)
    m23, n23 = jnp.maximum(a2, a3), jnp.minimum(a2, a3)
    return jnp.maximum(m01, m23), jnp.maximum(jnp.minimum(m01, m23), jnp.maximum(n01, n23))


def _route_tile(logits, b_col):
    aff = jax.nn.sigmoid(logits)
    biased = aff + b_col
    row = lambda m, e: m[e:e + 1, :]
    g_sel = best = None
    for g in range(N_EXPERT_GROUPS):
        top1, top2 = _second_max(*[row(biased, EXPERTS_PER_GROUP * g + j) for j in range(EXPERTS_PER_GROUP)])
        score = top1 + top2
        if g == 0:
            g_sel, best = jnp.zeros(score.shape, jnp.int32), score
        else:
            upd = score > best
            g_sel, best = jnp.where(upd, g, g_sel), jnp.where(upd, score, best)

    def in_group(m, j):
        out = row(m, (N_EXPERT_GROUPS - 1) * EXPERTS_PER_GROUP + j)
        for g in range(N_EXPERT_GROUPS - 2, -1, -1):
            out = jnp.where(g_sel == g, row(m, EXPERTS_PER_GROUP * g + j), out)
        return out

    v = [in_group(biased, j) for j in range(EXPERTS_PER_GROUP)]
    a = [in_group(aff, j) for j in range(EXPERTS_PER_GROUP)]
    i1, b1, w1 = jnp.zeros(g_sel.shape, jnp.int32), v[0], a[0]
    for j in range(1, EXPERTS_PER_GROUP):
        upd = v[j] > b1
        i1, b1, w1 = jnp.where(upd, j, i1), jnp.where(upd, v[j], b1), jnp.where(upd, a[j], w1)
    first = i1 == 0
    i2, b2, w2 = jnp.where(first, 1, 0), jnp.where(first, v[1], v[0]), jnp.where(first, a[1], a[0])
    for j in range(1, EXPERTS_PER_GROUP):
        upd = jnp.logical_and(i1 != j, v[j] > b2)
        i2, b2, w2 = jnp.where(upd, j, i2), jnp.where(upd, v[j], b2), jnp.where(upd, a[j], w2)
    den = w1 + w2
    return EXPERTS_PER_GROUP * g_sel + i1, EXPERTS_PER_GROUP * g_sel + i2, w1 / den, w2 / den


def _rows8(r0, r1):
    sub = lax.broadcasted_iota(jnp.int32, (8, r0.shape[1]), 0)
    return jnp.where(sub == 0, r0, jnp.where(sub == 1, r1, jnp.zeros_like(r0)))


def _outproj_kernel(ya_ref, yb_ref, wa_ref, wb_ref, x_ref, mod_ref, g_ref, wr_ref, br_ref, tri_ref,
                    xo_ref, f_ref, e_ref, w_ref, r_ref, cnt_ref, carry_ref):
    i = pl.program_id(0)
    m = mod_ref[0]
    y = _dot(ya_ref[...], wa_ref[...]) + _dot(yb_ref[...], wb_ref[...])
    x = x_ref[...] + m[2:3] * y
    xo_ref[...] = x
    f = _rms_mod(x, g_ref[...], m[3:4], m[4:5])
    _rows_to_slab(f_ref, f, TM, 0)
    _zero_slab_padding(f_ref, TM)
    f_hi = f.astype(BF16)
    f_lo = (f - f_hi.astype(F32)).astype(BF16)
    wr = wr_ref[...]
    w_hi = wr.astype(BF16)
    w_lo = (wr - w_hi.astype(F32)).astype(BF16)
    logits = _dot_nt(w_hi, f_hi) + _dot_nt(w_lo, f_hi) + _dot_nt(w_hi, f_lo)
    e0, e1, w0, w1 = _route_tile(logits, br_ref[:, 0:1])

    @pl.when(i == 0)
    def _():
        carry_ref[...] = jnp.zeros(carry_ref.shape, F32)

    sub = lax.broadcasted_iota(jnp.int32, (N_EXPERTS, TM), 0)
    hot0, hot1 = sub == e0, sub == e1
    member = jnp.where(jnp.logical_or(hot0, hot1), 1.0, 0.0)
    before = _dot(member.astype(BF16), tri_ref[...]) + carry_ref[:, 0:1]
    r0 = jnp.sum(jnp.where(hot0, before, 0.0), axis=0, keepdims=True)
    r1 = jnp.sum(jnp.where(hot1, before, 0.0), axis=0, keepdims=True)
    carry_ref[...] = carry_ref[...] + jnp.sum(member, axis=1, keepdims=True)
    e_ref[...] = _rows8(e0, e1)
    w_ref[...] = _rows8(w0, w1)
    r_ref[...] = _rows8(r0.astype(jnp.int32), r1.astype(jnp.int32))
    cnt_ref[...] = carry_ref[...]


def outproj(ya, yb, wa, wb, x, mods_l, g_ffn, w_router_t, b_router_col, n_ctx_blocks, casts=()):
    nt, d = x.shape
    nblk = nt // TM
    kern, c_in, c_out, c_shape, c_args = _with_casts(_outproj_kernel, casts, nblk, 10, 6)
    half = pl.BlockSpec((TM, 1024), lambda i: (i, 0))
    wsp = pl.BlockSpec((1024, d), lambda i: (0, 0), pipeline_mode=pl.Buffered(1))
    row = pl.BlockSpec((TM, d), lambda i: (i, 0))
    r8 = pl.BlockSpec((8, TM), lambda i: (0, i))
    cnt = pl.BlockSpec((N_EXPERTS, 128), lambda i: (0, 0))
    tri = (jnp.arange(TM)[:, None] < jnp.arange(TM)[None, :]).astype(BF16)
    res = pl.pallas_call(
        kern,
        grid=(nblk,),
        in_specs=[half, half, wsp, wsp, row, _mod_spec(n_ctx_blocks),
                  pl.BlockSpec((1, d), lambda i: (0, 0)),
                  pl.BlockSpec((N_EXPERTS, d), lambda i: (0, 0)), cnt,
                  pl.BlockSpec((TM, TM), lambda i: (0, 0))] + c_in,
        out_specs=[row, pl.BlockSpec((TM * SLAB, 128), lambda i: (i, 0)), r8, r8, r8, cnt] + c_out,
        out_shape=[jax.ShapeDtypeStruct((nt, d), F32), jax.ShapeDtypeStruct((nt * SLAB, 128), F32),
                   jax.ShapeDtypeStruct((8, nt), jnp.int32), jax.ShapeDtypeStruct((8, nt), F32),
                   jax.ShapeDtypeStruct((8, nt), jnp.int32), jax.ShapeDtypeStruct((N_EXPERTS, 128), F32)] + c_shape,
        scratch_shapes=[pltpu.VMEM((N_EXPERTS, 128), F32)],
        compiler_params=_cparams(),
        name="outproj",
    )(ya, yb, wa, wb, x, mods_l, g_ffn.reshape(1, d), w_router_t, b_router_col, tri, *c_args)
    return res[:6], res[6:]


N_LOC = 3 * ATTN_BLOCK


def _rope(x, cos, sin):
    lane = lax.broadcasted_iota(jnp.int32, x.shape, 1)
    swapped = jnp.where(lane % 64 < 32, pltpu.roll(x, 96, axis=1), pltpu.roll(x, 32, axis=1))
    return x * cos + swapped * sin


def _attn_kernel(sink_ref, q_ref, kp_ref, kc_ref, kn_ref, vp_ref, vc_ref, vn_ref, kx_ref, vx_ref,
                 cosp_ref, cosc_ref, cosn_ref, sinp_ref, sinc_ref, sinn_ref, o_ref,
                 qs_ref, kbuf_ref, vbuf_ref, *, n_ctx_blocks, n_blocks, n_ctx):
    i = pl.program_id(0)
    nkeys = N_LOC + n_ctx
    nq = Q_PER_KV * ATTN_BLOCK
    is_lat = jnp.where(i >= n_ctx_blocks, 1, 0)
    prev_ok = jnp.where(i - 1 >= n_ctx_blocks, is_lat, 0)
    next_ok = jnp.where(i + 1 <= n_blocks - 1, is_lat, 0)
    qi = lax.broadcasted_iota(jnp.int32, (nq, nkeys), 0) & (ATTN_BLOCK - 1)
    kj = lax.broadcasted_iota(jnp.int32, (nq, nkeys), 1)
    rel = kj - ATTN_BLOCK - qi
    blk_ok = jnp.where(kj < ATTN_BLOCK, prev_ok, jnp.where(kj < 2 * ATTN_BLOCK, is_lat, next_ok))
    rel = jnp.where(blk_ok > 0, rel, WINDOW + 1)
    valid = jnp.logical_or(kj >= N_LOC, jnp.logical_and(rel >= -WINDOW, rel <= WINDOW))
    rowh = lax.broadcasted_iota(jnp.int32, (nq, 1), 0) // ATTN_BLOCK
    for h in range(N_KV_HEADS):
        hc = slice(h * HEAD_DIM, (h + 1) * HEAD_DIM)
        kbuf_ref[0:ATTN_BLOCK, :] = _rope(kp_ref[:, hc], cosp_ref[...], sinp_ref[...]).astype(BF16)
        kbuf_ref[ATTN_BLOCK:2 * ATTN_BLOCK, :] = _rope(kc_ref[:, hc], cosc_ref[...], sinc_ref[...]).astype(BF16)
        kbuf_ref[2 * ATTN_BLOCK:N_LOC, :] = _rope(kn_ref[:, hc], cosn_ref[...], sinn_ref[...]).astype(BF16)
        kbuf_ref[N_LOC:, :] = kx_ref[:, hc].astype(BF16)
        vbuf_ref[0:ATTN_BLOCK, :] = vp_ref[:, hc].astype(BF16)
        vbuf_ref[ATTN_BLOCK:2 * ATTN_BLOCK, :] = vc_ref[:, hc].astype(BF16)
        vbuf_ref[2 * ATTN_BLOCK:N_LOC, :] = vn_ref[:, hc].astype(BF16)
        vbuf_ref[N_LOC:, :] = vx_ref[:, hc].astype(BF16)
        sink = jnp.zeros((Q_PER_KV * ATTN_BLOCK, 1), F32)
        for gq in range(Q_PER_KV):
            head = h * Q_PER_KV + gq
            qc = slice(head * HEAD_DIM, (head + 1) * HEAD_DIM)
            qs_ref[gq * ATTN_BLOCK:(gq + 1) * ATTN_BLOCK, :] = _rope(
                q_ref[:, qc], cosc_ref[...], sinc_ref[...]).astype(BF16)
            sink = jnp.where(rowh == gq, sink_ref[head], sink)
        s = _dot_nt(qs_ref[...], kbuf_ref[...]) * ATTN_SCALE
        s = jnp.where(valid, s, NEG_INF)
        mx = jnp.maximum(jnp.max(s, axis=-1, keepdims=True), sink)
        p = jnp.exp(s - mx)
        den = jnp.sum(p, axis=-1, keepdims=True) + jnp.exp(sink - mx)
        o = _dot(p.astype(BF16), vbuf_ref[...]) / den
        for gq in range(Q_PER_KV):
            head = h * Q_PER_KV + gq
            o_ref[:, head * HEAD_DIM:(head + 1) * HEAD_DIM] = o[gq * ATTN_BLOCK:(gq + 1) * ATTN_BLOCK].astype(BF16)


def attention(p, sink, cos_t, sin_t, n_ctx, casts=()):
    nt = p.shape[0]
    nblk = nt // ATTN_BLOCK
    ncb = n_ctx // ATTN_BLOCK
    kern, c_in, c_out, c_shape, c_args = _with_casts(
        functools.partial(_attn_kernel, n_ctx_blocks=ncb, n_blocks=nblk, n_ctx=n_ctx), casts, nblk, 16, 1)
    kcol, vcol = 2048 // KV_WIDTH, 2048 // KV_WIDTH + 1
    pm = lambda i: jnp.maximum(i - 1, 0)
    nx = lambda i: jnp.minimum(i + 1, nblk - 1)
    kv = lambda f, c: pl.BlockSpec((ATTN_BLOCK, KV_WIDTH), lambda i, s: (f(i), c))
    tab = lambda f: pl.BlockSpec((ATTN_BLOCK, HEAD_DIM), lambda i, s: (f(i), 0))
    same = lambda i: i
    grid_spec = pltpu.PrefetchScalarGridSpec(
        num_scalar_prefetch=1,
        grid=(nblk,),
        in_specs=[pl.BlockSpec((ATTN_BLOCK, C_WIDTH), lambda i, s: (i, 0)),
                  kv(pm, kcol), kv(same, kcol), kv(nx, kcol), kv(pm, vcol), kv(same, vcol), kv(nx, vcol),
                  pl.BlockSpec((n_ctx, KV_WIDTH), lambda i, s: (0, kcol)),
                  pl.BlockSpec((n_ctx, KV_WIDTH), lambda i, s: (0, vcol)),
                  tab(pm), tab(same), tab(nx), tab(pm), tab(same), tab(nx)] + c_in,
        out_specs=[pl.BlockSpec((ATTN_BLOCK, C_WIDTH), lambda i, s: (i, 0))] + c_out,
        scratch_shapes=[pltpu.VMEM((Q_PER_KV * ATTN_BLOCK, HEAD_DIM), BF16),
                        pltpu.VMEM((N_LOC + n_ctx, HEAD_DIM), BF16),
                        pltpu.VMEM((N_LOC + n_ctx, HEAD_DIM), BF16)],
    )
    res = pl.pallas_call(
        kern,
        grid_spec=grid_spec,
        out_shape=[jax.ShapeDtypeStruct((nt, C_WIDTH), BF16)] + c_shape,
        compiler_params=_cparams(),
        name="attention",
    )(sink, p, p, p, p, p, p, p, p, p, cos_t, cos_t, cos_t, sin_t, sin_t, sin_t, *c_args)
    return res[0], res[1:]


def rope_tables(n_ctx, n_lat):
    rows = n_lat // GRID_W
    row = jnp.broadcast_to(jnp.arange(rows, dtype=F32)[:, None], (rows, GRID_W)).reshape(-1)
    col = jnp.broadcast_to(jnp.arange(GRID_W, dtype=F32)[None, :], (rows, GRID_W)).reshape(-1)
    half = HEAD_DIM // 2
    inv_freq = ROPE_BASE ** (-jnp.arange(0, half, 2, dtype=F32) / half)
    ang_r = row[:, None] * inv_freq
    ang_c = col[:, None] * inv_freq
    cos = jnp.concatenate([jnp.cos(ang_r), jnp.cos(ang_r), jnp.cos(ang_c), jnp.cos(ang_c)], axis=-1)
    sin = jnp.concatenate([-jnp.sin(ang_r), jnp.sin(ang_r), -jnp.sin(ang_c), jnp.sin(ang_c)], axis=-1)
    cos = jnp.concatenate([jnp.ones((n_ctx, HEAD_DIM), F32), cos], axis=0)
    sin = jnp.concatenate([jnp.zeros((n_ctx, HEAD_DIM), F32), sin], axis=0)
    return cos, sin


S_LH = S_CHUNK * S_GROUP_CH
S_HALF = S_GROUP_BATCH * 128


S_PW_ROWS = 4 * S_CHUNK
S_PITCH = S_CHUNK + 8


def _dot3_nt(x, a):
    x_hi, a_hi = x.astype(BF16), a.astype(BF16)
    x_lo, a_lo = (x - x_hi.astype(F32)).astype(BF16), (a - a_hi.astype(F32)).astype(BF16)
    return _dot_nt(x_hi, a_hi) + _dot_nt(x_lo, a_hi) + _dot_nt(x_hi, a_lo)


def _s5_kernel(u_ref, pw_ref, bc_ref, lam_ref, y_ref, s_ref, hf_ref, hr_ref,
               ws_ref, wo_ref, are_ref, aim_ref, t_ref, ug_ref, yg_ref, upad_ref, ypad_ref, *,
               n_chunks, n_ctx_chunks):
    L = S_CHUNK
    gpt = 128 // S_GROUP_CH
    ug_ref[:, n_chunks:, :] = jnp.zeros((S_GROUP_BATCH, ug_ref.shape[1] - n_chunks, S_LH), F32)

    def spread(c, carry):
        upad_ref[pl.ds(pl.multiple_of(c * S_PITCH, 8), L), :] = u_ref[pl.ds(pl.multiple_of(c * L, L), L), :]
        return carry
    lax.fori_loop(0, n_chunks, spread, 0, unroll=8)
    for s_tok in range(L):
        x = upad_ref[pl.ds(s_tok, n_chunks, stride=S_PITCH), :]
        dst = s_tok % gpt
        for g in range(S_GROUP_BATCH):
            k = (dst - g) % gpt
            r = x if k == 0 else pltpu.roll(x, k * S_GROUP_CH, axis=1)
            ug_ref[g, 0:n_chunks, s_tok * S_GROUP_CH:(s_tok + 1) * S_GROUP_CH] = (
                r[:, dst * S_GROUP_CH:(dst + 1) * S_GROUP_CH])

    def scaled_rows(g, x_re, x_im, row0, n_rows, emit):
        def body(r, carry):
            p_re = pw_ref[g, 0, pl.ds(row0 + r, 1), :]
            p_im = pw_ref[g, 1, pl.ds(row0 + r, 1), :]
            emit(pl.ds(pl.multiple_of(r * S_GROUP_CH, S_GROUP_CH), S_GROUP_CH),
                 x_re * p_re - x_im * p_im, x_re * p_im + x_im * p_re)
            return carry
        lax.fori_loop(0, n_rows, body, 0, unroll=4)

    def emit_ws(rows, re, im):
        ws_ref[rows, 0:128] = re.astype(BF16)
        ws_ref[rows, 128:256] = im.astype(BF16)

    def emit_wo(rows, re, im):
        wo_ref[rows, 0:128] = re.astype(BF16)
        wo_ref[rows, 128:256] = (-im).astype(BF16)

    def emit_a(rows, re, im):
        are_ref[rows, :] = re
        aim_ref[rows, :] = im

    for g in range(S_GROUP_BATCH):
        scaled_rows(g, bc_ref[g, 0], bc_ref[g, 1], 0, L, emit_ws)
        s = _dot(ug_ref[g].astype(BF16), ws_ref[...])
        s_ref[:, g * 128:(g + 1) * 128] = s[:, 0:128]
        s_ref[:, S_HALF + g * 128:S_HALF + (g + 1) * 128] = s[:, 128:256]
    hf_ref[...] = jnp.zeros(hf_ref.shape, F32)
    hr_ref[...] = jnp.zeros(hr_ref.shape, F32)
    lam = lam_ref[0]
    lam_re, lam_im = lam[:, :S_HALF], lam[:, S_HALF:]
    is_fwd = lax.broadcasted_iota(jnp.int32, (1, 2 * S_HALF), 1) % 128 < S_STATE

    def step(k, state):
        st_re, st_im = state
        cf = k
        cr = jnp.where(k < n_ctx_chunks, n_ctx_chunks - 1 - k, n_chunks - 1 - (k - n_ctx_chunks))
        st = jnp.concatenate([st_re, st_im], axis=1)
        hf_ref[pl.ds(cf, 1), :] = st
        hr_ref[pl.ds(cr, 1), :] = st
        s_in = jnp.where(is_fwd, s_ref[pl.ds(cf, 1), :], s_ref[pl.ds(cr, 1), :])
        new_re = lam_re * st_re - lam_im * st_im + s_in[:, :S_HALF]
        new_im = lam_re * st_im + lam_im * st_re + s_in[:, S_HALF:]
        return new_re, new_im

    zero = jnp.zeros((1, S_HALF), F32)
    lax.fori_loop(0, n_chunks, step, (zero, zero))
    hin = jnp.where(is_fwd, hf_ref[...], hr_ref[...]).astype(BF16)
    kwidth = 2 * L * S_GROUP_CH
    for g in range(S_GROUP_BATCH):
        hin_g = jnp.concatenate([hin[:, g * 128:(g + 1) * 128],
                                 hin[:, S_HALF + g * 128:S_HALF + (g + 1) * 128]], axis=1)
        c_re, c_im = bc_ref[g, 2], bc_ref[g, 3]
        scaled_rows(g, c_re, c_im, L, L, emit_wo)
        scaled_rows(g, c_re, c_im, 2 * L, 2 * L, emit_a)
        kern = _dot3_nt(bc_ref[g, 0], are_ref[...]) - _dot3_nt(bc_ref[g, 1], aim_ref[...])
        for s_tok in range(L):
            off = (L - 1 - s_tok) * S_GROUP_CH
            win = kern if off == 0 else pltpu.roll(kern, kwidth - off, axis=1)
            t_ref[s_tok * S_GROUP_CH:(s_tok + 1) * S_GROUP_CH, :] = win[:, :S_LH].astype(BF16)
        yg_ref[g] = _dot(ug_ref[g].astype(BF16), t_ref[...]) + _dot_nt(hin_g, wo_ref[...])
    lane_group = lax.broadcasted_iota(jnp.int32, (n_chunks, 128), 1) // S_GROUP_CH
    for s_tok in range(L):
        dst = s_tok % gpt
        cols = slice((s_tok // gpt) * 128, (s_tok // gpt + 1) * 128)
        z = None
        for g in range(S_GROUP_BATCH):
            piece = yg_ref[g, 0:n_chunks, cols]
            k = (g - dst) % gpt
            r = piece if k == 0 else pltpu.roll(piece, k * S_GROUP_CH, axis=1)
            z = r if z is None else jnp.where(lane_group == g, r, z)
        ypad_ref[pl.ds(s_tok, n_chunks, stride=S_PITCH), :] = z

    def pack(c, carry):
        y_ref[pl.ds(pl.multiple_of(c * L, L), L), :] = ypad_ref[pl.ds(pl.multiple_of(c * S_PITCH, 8), L), :]
        return carry
    lax.fori_loop(0, n_chunks, pack, 0, unroll=8)


def s5_scan(p, pw_tab, bc_tab, lam_rows, n_chunks, n_ctx_chunks):
    nt = p.shape[0]
    assert S_GROUP_BATCH * S_GROUP_CH == 128 and nt == n_chunks * S_CHUNK
    ncp = -(-n_chunks // 16) * 16
    lh = S_LH
    return pl.pallas_call(
        functools.partial(_s5_kernel, n_chunks=n_chunks, n_ctx_chunks=n_ctx_chunks),
        grid=(S_GROUPS // S_GROUP_BATCH,),
        in_specs=[pl.BlockSpec((nt, 128), lambda i: (0, C_WIDTH // 128 + i)),
                  pl.BlockSpec((S_GROUP_BATCH, 2, S_PW_ROWS, 128), lambda i: (i, 0, 0, 0)),
                  pl.BlockSpec((S_GROUP_BATCH, 4, S_GROUP_CH, 128), lambda i: (i, 0, 0, 0)),
                  pl.BlockSpec((1, 1, 2 * S_HALF), lambda i: (i, 0, 0))],
        out_specs=pl.BlockSpec((nt, 128), lambda i: (0, i)),
        out_shape=jax.ShapeDtypeStruct((nt, S_WIDTH), F32),
        scratch_shapes=[pltpu.VMEM((ncp, 2 * S_HALF), F32)] * 3 + [
            pltpu.VMEM((lh, 256), BF16), pltpu.VMEM((lh, 256), BF16),
            pltpu.VMEM((2 * lh, 128), F32), pltpu.VMEM((2 * lh, 128), F32), pltpu.VMEM((lh, lh), BF16),
            pltpu.VMEM((S_GROUP_BATCH, ncp, lh), F32), pltpu.VMEM((S_GROUP_BATCH, ncp, lh), F32),
            pltpu.VMEM((n_chunks * S_PITCH, 128), F32), pltpu.VMEM((n_chunks * S_PITCH, 128), F32)],
        compiler_params=_cparams(),
        name="s5_scan",
    )(p, pw_tab, bc_tab, lam_rows)


def s5_weights(lam_re, lam_im, log_dt, b_re, b_im, c_re, c_im):
    L = S_CHUNK
    lam = lax.complex(lam_re.astype(F32), lam_im.astype(F32))
    lam_dt = lam * jnp.exp(log_dt.astype(F32))[..., None]
    lam_bar = jnp.exp(lam_dt)
    b_bar = ((lam_bar - 1.0) / lam)[..., None] * lax.complex(b_re.astype(F32), b_im.astype(F32))
    c = lax.complex(c_re.astype(F32), c_im.astype(F32))
    n = jnp.arange(L, dtype=F32)
    lag = jnp.arange(2 * L, dtype=F32) - (L - 1)
    expo = jnp.concatenate([jnp.stack([L - 1 - n, n], axis=-1), jnp.stack([n + 1, L - n], axis=-1),
                            jnp.stack([lag, -lag], axis=-1)], axis=0)
    live = jnp.logical_and(expo >= 0, (jnp.arange(4 * L) < 4 * L - 1)[:, None])
    pw = jnp.where(live[:, :, None, None], jnp.exp(lam_dt[None] * jnp.maximum(expo, 0.0)[:, :, None, None]), 0.0)
    pw = jnp.transpose(pw, (2, 0, 1, 3)).reshape(S_GROUPS, S_PW_ROWS, 2 * S_STATE)
    pw_tab = jnp.stack([pw.real, pw.imag], axis=1)
    bt = jnp.transpose(b_bar, (1, 3, 0, 2)).reshape(S_GROUPS, S_GROUP_CH, 2 * S_STATE)
    ct = jnp.transpose(c, (1, 2, 0, 3)).reshape(S_GROUPS, S_GROUP_CH, 2 * S_STATE)
    bc_tab = jnp.stack([bt.real, bt.imag, ct.real, ct.imag], axis=1)
    lam_l = jnp.exp(lam_dt * L)
    nb = S_GROUPS // S_GROUP_BATCH
    lre = jnp.concatenate([lam_l[0].real, lam_l[1].real], axis=-1).reshape(nb, 1, S_HALF)
    lim = jnp.concatenate([lam_l[0].imag, lam_l[1].imag], axis=-1).reshape(nb, 1, S_HALF)
    return pw_tab, bc_tab, jnp.concatenate([lre, lim], axis=-1)


def _glu_kernel(y_ref, u_ref, d_ref, w_ref, b_ref, o_ref):
    z = jax.nn.gelu(y_ref[...] + d_ref[...] * u_ref[...])
    gate = jax.nn.sigmoid(_dot(z.astype(BF16), w_ref[...]) + b_ref[...])
    o_ref[...] = (z * gate).astype(BF16)


def s5_glu(y_ssm, p, d_skip, glu_w_bf16, glu_b, casts=()):
    nt = y_ssm.shape[0]
    vec = pl.BlockSpec((1, S_WIDTH), lambda i: (0, 0))
    kern, c_in, c_out, c_shape, c_args = _with_casts(_glu_kernel, casts, nt // TM, 5, 1)
    res = pl.pallas_call(
        kern,
        grid=(nt // TM,),
        in_specs=[pl.BlockSpec((TM, S_WIDTH), lambda i: (i, 0)),
                  pl.BlockSpec((TM, S_WIDTH), lambda i: (i, 1)),
                  vec, pl.BlockSpec((S_WIDTH, S_WIDTH), lambda i: (0, 0)), vec] + c_in,
        out_specs=[pl.BlockSpec((TM, S_WIDTH), lambda i: (i, 0))] + c_out,
        out_shape=[jax.ShapeDtypeStruct((nt, S_WIDTH), BF16)] + c_shape,
        compiler_params=_cparams(),
        name="s5_glu",
    )(y_ssm, p, d_skip.reshape(1, -1), glu_w_bf16, glu_b.reshape(1, -1), *c_args)
    return res[0], res[1:]


MOE_FCHUNKS = ((0, 512), (512, 512), (1024, 384))
MOE_YC = 512


def _dispatch_kernel(pos_ref, cnt_ref, pad_ref, start_ref, nu_ref, f_hbm, xs_hbm, fbuf, zero_ref, lsem, sem, zsem,
                     *, nt, n_blocks):
    i = pl.program_id(0)
    n_tiles = pl.num_programs(0)
    blk_rows = MOE_BM * SLAB
    tile_rows = TM * SLAB

    def tile_load(tile):
        return pltpu.make_async_copy(f_hbm.at[pl.ds(pl.multiple_of(tile * tile_rows, tile_rows), tile_rows), :],
                                     fbuf.at[tile % 3], lsem.at[tile % 3])

    def start_body(t, carry):
        src = fbuf.at[i % 3, pl.ds(pl.multiple_of(t * SLAB, SLAB), SLAB), :]
        for k in range(TOP_K):
            p = pl.multiple_of(pos_ref[k * nt + i * TM + t] * SLAB, SLAB)
            pltpu.make_async_copy(src, xs_hbm.at[pl.ds(p, SLAB), :], sem.at[i % 2]).start()
        return carry

    def wait_tile(slot):
        for _ in range(TOP_K):
            pltpu.make_async_copy(fbuf.at[0], xs_hbm.at[pl.ds(0, tile_rows), :], sem.at[slot]).wait()

    def pad_copies(start):
        for e in range(N_EXPERTS):
            def body(r, carry):
                p = pl.multiple_of((start_ref[e] + r) * SLAB, SLAB)
                cp = pltpu.make_async_copy(zero_ref.at[pl.ds(0, SLAB), :], xs_hbm.at[pl.ds(p, SLAB), :], zsem)
                if start:
                    cp.start()
                else:
                    cp.wait()
                return carry
            lax.fori_loop(cnt_ref[e], pad_ref[e], body, 0)

        def tail(blk, carry):
            p = pl.multiple_of(blk * blk_rows, blk_rows)
            cp = pltpu.make_async_copy(zero_ref, xs_hbm.at[pl.ds(p, blk_rows), :], zsem)
            if start:
                cp.start()
            else:
                cp.wait()
            return carry
        lax.fori_loop(nu_ref[0], n_blocks, tail, 0)

    @pl.when(i == 0)
    def _():
        tile_load(0).start()
        zero_ref[...] = jnp.zeros(zero_ref.shape, F32)
        pad_copies(True)

    @pl.when(i + 1 < n_tiles)
    def _():
        tile_load(i + 1).start()

    tile_load(i).wait()
    lax.fori_loop(0, TM, start_body, 0, unroll=8)

    @pl.when(i > 0)
    def _():
        wait_tile((i - 1) % 2)

    @pl.when(i == n_tiles - 1)
    def _():
        wait_tile(i % 2)

    @pl.when(i == 0)
    def _():
        pad_copies(False)


def moe_dispatch(f_slab, pos, counts, padded, pad_start, n_used, n_blocks):
    nt = f_slab.shape[0] // SLAB
    grid_spec = pltpu.PrefetchScalarGridSpec(
        num_scalar_prefetch=5,
        grid=(nt // TM,),
        in_specs=[pl.BlockSpec(memory_space=pl.ANY)],
        out_specs=pl.BlockSpec(memory_space=pl.ANY),
        scratch_shapes=[pltpu.VMEM((3, TM * SLAB, 128), F32), pltpu.VMEM((MOE_BM * SLAB, 128), F32),
                        pltpu.SemaphoreType.DMA((3,)), pltpu.SemaphoreType.DMA((2,)), pltpu.SemaphoreType.DMA],
    )
    return pl.pallas_call(
        functools.partial(_dispatch_kernel, nt=nt, n_blocks=n_blocks),
        grid_spec=grid_spec,
        out_shape=jax.ShapeDtypeStruct((n_blocks * MOE_BM * SLAB, 128), F32),
        compiler_params=_cparams(),
        name="moe_dispatch",
    )(pos, counts, padded, pad_start, n_used, f_slab)


def _moe_kernel(be_ref, nu_ref, xs_ref, wg_ref, wu_ref, wd_ref, y_ref, hbuf):
    @pl.when(pl.program_id(0) >= nu_ref[0])
    def _():
        y_ref[...] = jnp.zeros(y_ref.shape, F32)

    @pl.when(pl.program_id(0) < nu_ref[0])
    def _():
        x = _slab_to_rows(xs_ref, MOE_BM).astype(BF16)
        for f0, fw in MOE_FCHUNKS:
            hg = _dot(x, wg_ref[0, :, f0:f0 + fw])
            hu = _dot(x, wu_ref[0, :, f0:f0 + fw])
            hbuf[:, f0:f0 + fw] = (hg * jax.nn.sigmoid(hg) * hu).astype(BF16)
        for c0 in range(0, D_MODEL, MOE_YC):
            _rows_to_slab(y_ref, _dot(hbuf[...], wd_ref[0, :, c0:c0 + MOE_YC]), MOE_BM, c0 // 128)
        _zero_slab_padding(y_ref, MOE_BM)


def moe_experts(x_sorted, block_e, n_used, wg, wu, wd):
    n_blocks = block_e.shape[0]
    d, fexp = wg.shape[1], wg.shape[2]
    blk = lambda b, be, nu: (jnp.minimum(b, nu[0] - 1), 0)
    wsel = lambda b, be, nu: (be[jnp.minimum(b, nu[0] - 1)], 0, 0)
    grid_spec = pltpu.PrefetchScalarGridSpec(
        num_scalar_prefetch=2,
        grid=(n_blocks,),
        in_specs=[pl.BlockSpec((MOE_BM * SLAB, 128), blk),
                  pl.BlockSpec((1, d, fexp), wsel), pl.BlockSpec((1, d, fexp), wsel),
                  pl.BlockSpec((1, fexp, d), wsel)],
        out_specs=pl.BlockSpec((MOE_BM * SLAB, 128), lambda b, be, nu: (b, 0)),
        scratch_shapes=[pltpu.VMEM((MOE_BM, fexp), BF16)],
    )
    return pl.pallas_call(
        _moe_kernel,
        grid_spec=grid_spec,
        out_shape=jax.ShapeDtypeStruct(x_sorted.shape, F32),
        compiler_params=_cparams(),
        name="moe_experts",
    )(block_e, n_used, x_sorted, wg, wu, wd)


def moe_layout(eidx, rank, cnt):
    nt = eidx.shape[1]
    n_blocks = -(-(nt * TOP_K) // MOE_BM) + N_EXPERTS
    counts = cnt[:, 0].astype(jnp.int32)
    padded = (counts + MOE_BM - 1) // MOE_BM * MOE_BM
    pad_end = jnp.cumsum(padded)
    pad_start = pad_end - padded
    hot = eidx[:TOP_K, :, None] == jnp.arange(N_EXPERTS, dtype=jnp.int32)
    pos = (rank[:TOP_K] + jnp.sum(jnp.where(hot, pad_start, 0), axis=-1)).reshape(-1)
    blk_start = jnp.arange(n_blocks, dtype=jnp.int32) * MOE_BM
    block_e = jnp.minimum(jnp.sum(blk_start[:, None] >= pad_end[None, :], axis=1), N_EXPERTS - 1).astype(jnp.int32)
    n_used = (pad_end[-1:] // MOE_BM).astype(jnp.int32)
    return pos, counts, padded, pad_start, block_e, n_used


def _final_kernel(pos_ref, x_ref, y_hbm, wt_ref, mod_ref, g_ref, out_ref, gbuf, sem, *, nt, tile0):
    x = x_ref[...] + mod_ref[0][5:6] * _moe_combine(pos_ref, y_hbm, wt_ref, gbuf, sem, nt, tile0)
    ms = jnp.mean(x * x, axis=-1, keepdims=True)
    out_ref[...] = x * lax.rsqrt(ms + EPS) * g_ref[...]


def final_norm(x, moe, mods_l, g_final, n_ctx_blocks):
    nt, d = x.shape
    nlat = nt // TM - n_ctx_blocks
    y_sorted, pos, wts_t = moe
    grid_spec = pltpu.PrefetchScalarGridSpec(
        num_scalar_prefetch=1,
        grid=(nlat,),
        in_specs=[pl.BlockSpec((TM, d), lambda i, *_: (i + n_ctx_blocks, 0)),
                  pl.BlockSpec(memory_space=pl.ANY),
                  pl.BlockSpec((TM, 8), lambda i, *_: (i + n_ctx_blocks, 0)),
                  pl.BlockSpec((1, 6, d), lambda i, *_: (0, 0, 0)),
                  pl.BlockSpec((1, d), lambda i, *_: (0, 0))],
        out_specs=pl.BlockSpec((TM, d), lambda i, *_: (i, 0)),
        scratch_shapes=_COMBINE_SCRATCH,
    )
    return pl.pallas_call(
        functools.partial(_final_kernel, nt=nt, tile0=n_ctx_blocks),
        grid_spec=grid_spec,
        out_shape=jax.ShapeDtypeStruct((nlat * TM, d), F32),
        compiler_params=_cparams(),
        name="final_norm",
    )(pos, x, y_sorted, wts_t, mods_l, g_final.reshape(1, d))


def kernel(x, c, ctx, c_ctx, w_mod, b_mod, g_mix, g_ffn, w_in_even, w_out_even, sgu_ln_g, sgu_ln_b, sgu_w, sgu_b, conv_w, conv_b, conv_ln_g, conv_ln_b, w_in_odd, w_out_odd, attn_sink, ssm_lam_re, ssm_lam_im, ssm_log_dt, ssm_b_re, ssm_b_im, ssm_c_re, ssm_c_im, ssm_d, glu_w, glu_b, w_router, b_router, w_gate, w_up, w_down, g_final):
    bsz, n_lat, d = x.shape
    n_ctx = ctx.shape[1]
    assert bsz == 1 and d == D_MODEL and n_ctx % TM == 0 and n_lat % TM == 0
    nt = n_ctx + n_lat
    ncb = n_ctx // TM
    n_chunks = nt // S_CHUNK
    ncp = -(-n_chunks // 16) * 16

    xs = jnp.concatenate([ctx[0], x[0]], axis=0)
    cond8 = jnp.concatenate([c, c_ctx[None, :], jnp.zeros((6, d), F32)], axis=0)
    mods = adaln_all(cond8, w_mod, b_mod)[:, :2].reshape(DEPTH, 2, 6, d)
    cos_t, sin_t = rope_tables(n_ctx, n_lat)
    w_router_t = w_router.T
    b_router_col = jnp.broadcast_to(b_router.astype(F32)[:, None], (N_EXPERTS, 128))
    fexp = w_gate.shape[-1]
    w_stack = (w_gate.reshape(-1, fexp), w_up.reshape(-1, fexp), w_down.reshape(-1, d))
    expert_bf = {}

    def job(layer, which):
        return (w_stack[which], layer)

    def done(jobs, outs):
        for (layer, which), o in zip(jobs, outs):
            expert_bf[(layer, which)] = o

    moe = None
    for l in range(DEPTH):
        j = l // 2
        odd = l % 2 == 1
        if odd:
            wi = w_in_odd[j]
            w_in = jnp.concatenate([wi[:, :C_WIDTH], wi[:, C_WIDTH + 2 * KV_WIDTH:],
                                    wi[:, C_WIDTH:C_WIDTH + 2 * KV_WIDTH]], axis=1).astype(BF16)
            w_out = w_out_odd[j].astype(BF16)
        else:
            w_in = w_in_even[j].astype(BF16)
            w_out = w_out_even[j].astype(BF16)
        jobs = [(l, 0)] if (l == 0 or odd) else []
        x_new, p, outs = inproj(xs, g_mix[l], mods[l], w_in, ncb, moe, mods[l - 1] if l > 0 else None,
                                casts=[job(*jb) for jb in jobs])
        done(jobs, outs)
        if x_new is not None:
            xs = x_new
        if odd:
            jobs = [(l, 1)]
            ya, outs = attention(p, attn_sink[j], cos_t, sin_t, n_ctx, casts=[job(*jb) for jb in jobs])
            done(jobs, outs)
            pw_tab, bc_tab, lam_rows = s5_weights(ssm_lam_re[j], ssm_lam_im[j], ssm_log_dt[j], ssm_b_re[j],
                                                  ssm_b_im[j], ssm_c_re[j], ssm_c_im[j])
            y_ssm = s5_scan(p, pw_tab, bc_tab, lam_rows, n_chunks, n_ctx // S_CHUNK)
            jobs = [(l + 1, 0)] if l + 1 < DEPTH else []
            yb, outs = s5_glu(y_ssm, p, ssm_d[j], glu_w[j].astype(BF16), glu_b[j], casts=[job(*jb) for jb in jobs])
            done(jobs, outs)
        else:
            bs_full = jnp.broadcast_to(sgu_b[j][:, :, None], (A_GROUPS, CHUNK, CHUNK)).astype(F32)
            jobs = [(l, 1)]
            ya, yb, outs = even_mixer(p, sgu_ln_g[j], sgu_ln_b[j], sgu_w[j].astype(BF16), bs_full,
                                      conv_w[j], conv_b[j], conv_ln_g[j], conv_ln_b[j], ncb,
                                      casts=[job(*jb) for jb in jobs])
            done(jobs, outs)
        jobs = [(l, 2)]
        (xs, f_slab, eidx, wts, rank, cnt), outs = outproj(
            ya, yb, w_out[:1024], w_out[1024:], xs, mods[l], g_ffn[l], w_router_t, b_router_col, ncb,
            casts=[job(*jb) for jb in jobs])
        done(jobs, outs)
        pos, counts, padded, pad_start, block_e, n_used = moe_layout(eidx, rank, cnt)
        x_sorted = moe_dispatch(f_slab, pos, counts, padded, pad_start, n_used, block_e.shape[0])
        y_sorted = moe_experts(x_sorted, block_e, n_used,
                               expert_bf[(l, 0)].reshape(N_EXPERTS, d, fexp),
                               expert_bf[(l, 1)].reshape(N_EXPERTS, d, fexp),
                               expert_bf[(l, 2)].reshape(N_EXPERTS, fexp, d))
        moe = (y_sorted, pos, wts.T)
    out = final_norm(xs, moe, mods[DEPTH - 1], g_final, ncb)
    return out.reshape(bsz, n_lat, d)
```

```python
import functools
import math

import jax
import jax.numpy as jnp
import numpy as np
from jax import lax
from jax.experimental import pallas as pl
from jax.experimental.pallas import tpu as pltpu

F32 = jnp.float32
BF16 = jnp.bfloat16

D_MODEL = 2048
DEPTH = 4
GRID_W = 64
EPS = 1e-6
NEG_INF = -1e30

A_WIDTH = 1024
A_GROUPS = 8
CHUNK = 128
B_WIDTH = 1024
CONV_WIDTH = 31
CONV_HALO = 16

HEAD_DIM = 128
N_Q_HEADS = 8
N_KV_HEADS = 2
Q_PER_KV = 4
C_WIDTH = 1024
KV_WIDTH = 256
WINDOW = 128
ATTN_BLOCK = 128
ROPE_BASE = 10000.0
ATTN_SCALE = HEAD_DIM ** -0.5
S_WIDTH = 1024
S_GROUP_CH = 16
S_GROUPS = 64
S_STATE = 64
S_CHUNK = 32
S_GROUP_BATCH = 8
ODD_IN = C_WIDTH + 2 * KV_WIDTH + S_WIDTH

N_EXPERTS = 16
N_EXPERT_GROUPS = 4
EXPERTS_PER_GROUP = 4
TOP_K = 2
D_EXPERT = 1408
MOE_BM = 256

TM = 256
VMEM_LIMIT = 56 * 1024 * 1024


def _cparams(n_axes=1, vmem=VMEM_LIMIT):
    return pltpu.CompilerParams(dimension_semantics=("arbitrary",) * n_axes, vmem_limit_bytes=vmem)


def _dot(a, b):
    return jnp.dot(a, b, preferred_element_type=F32)


def _dot_nt(a, b):
    return lax.dot_general(a, b, (((1,), (1,)), ((), ())), preferred_element_type=F32)


ADALN_TN = 1024


def _adaln_kernel(cond_ref, w_ref, b_ref, o_ref):
    c = cond_ref[...]
    s = (c * jax.nn.sigmoid(c)).astype(BF16)
    o_ref[0] = _dot(s, w_ref[0].astype(BF16)) + b_ref[0]


def adaln_all(cond8, w_mod, b_mod):
    depth, d, n6 = w_mod.shape
    return pl.pallas_call(
        _adaln_kernel,
        grid=(depth, n6 // ADALN_TN),
        in_specs=[
            pl.BlockSpec((8, d), lambda l, j: (0, 0)),
            pl.BlockSpec((1, d, ADALN_TN), lambda l, j: (l, 0, j)),
            pl.BlockSpec((1, 1, ADALN_TN), lambda l, j: (l, 0, j)),
        ],
        out_specs=pl.BlockSpec((1, 8, ADALN_TN), lambda l, j: (l, 0, j)),
        out_shape=jax.ShapeDtypeStruct((depth, 8, n6), F32),
        compiler_params=_cparams(2),
        name="adaln",
    )(cond8, w_mod, b_mod.reshape(depth, 1, n6))


def _with_casts(kernel_fn, casts, n_steps, n_lead, n_out):
    k = len(casts)
    in_specs, out_specs, out_shapes = [], [], []
    steps = 1 << (n_steps.bit_length() - 1)
    for w, layer in casts:
        rows = w.shape[0] // DEPTH
        assert rows % (steps * 16) == 0
        blk = (rows // steps, w.shape[1])
        in_specs.append(pl.BlockSpec(blk, lambda i, *_, base=layer * steps: (base + jnp.minimum(i, steps - 1), 0)))
        out_specs.append(pl.BlockSpec(blk, lambda i, *_: (jnp.minimum(i, steps - 1), 0)))
        out_shapes.append(jax.ShapeDtypeStruct((rows, w.shape[1]), BF16))

    def wrapped(*refs):
        lead, rest = refs[:n_lead], refs[n_lead:]
        cast_in, rest = rest[:k], rest[k:]
        outs, rest = rest[:n_out], rest[n_out:]
        cast_out, scratch = rest[:k], rest[k:]
        kernel_fn(*lead, *outs, *scratch)
        for src, dst in zip(cast_in, cast_out):
            dst[...] = src[...].astype(BF16)

    return wrapped, in_specs, out_specs, out_shapes, [w for w, _ in casts]


def _mod_spec(n_ctx_blocks):
    return pl.BlockSpec((1, 6, D_MODEL), lambda i, *_: (jnp.where(i < n_ctx_blocks, 1, 0), 0, 0))


def _rms_mod(x, g, shift, scale):
    ms = jnp.mean(x * x, axis=-1, keepdims=True)
    y = x * lax.rsqrt(ms + EPS) * g
    return y * (1.0 + scale) + shift


SLAB_DATA = D_MODEL // 128
SLAB = SLAB_DATA + 4


def _slab_to_rows(ref, n_rows):
    return jnp.concatenate([ref[pl.ds(c, n_rows, stride=SLAB), :] for c in range(SLAB_DATA)], axis=1)


def _rows_to_slab(ref, val, n_rows, c0):
    for c in range(val.shape[1] // 128):
        ref[pl.ds(c0 + c, n_rows, stride=SLAB), :] = val[:, c * 128:(c + 1) * 128]


def _zero_slab_padding(ref, n_rows):
    for c in range(SLAB_DATA, SLAB):
        ref[pl.ds(c, n_rows, stride=SLAB), :] = jnp.zeros((n_rows, 128), F32)


def _expert_row_gather(pos_ref, y_hbm, gbuf, sem, tile, slot, nt, start):
    if not start:
        for k in range(TOP_K):
            pltpu.make_async_copy(y_hbm.at[pl.ds(0, TM * SLAB), :], gbuf.at[slot, k], sem.at[slot]).wait()
        return

    def body(t, carry):
        for k in range(TOP_K):
            p = pl.multiple_of(pos_ref[k * nt + tile * TM + t] * SLAB, SLAB)
            pltpu.make_async_copy(y_hbm.at[pl.ds(p, SLAB), :],
                                  gbuf.at[slot, k, pl.ds(pl.multiple_of(t * SLAB, SLAB), SLAB), :],
                                  sem.at[slot]).start()
        return carry
    lax.fori_loop(0, TM, body, 0, unroll=8)


def _moe_combine(pos_ref, y_hbm, wt_ref, gbuf, sem, nt, tile0):
    i = pl.program_id(0)
    slot = i % 2

    @pl.when(i == 0)
    def _():
        _expert_row_gather(pos_ref, y_hbm, gbuf, sem, tile0, 0, nt, True)

    @pl.when(i + 1 < pl.num_programs(0))
    def _():
        _expert_row_gather(pos_ref, y_hbm, gbuf, sem, tile0 + i + 1, 1 - slot, nt, True)

    _expert_row_gather(pos_ref, y_hbm, gbuf, sem, tile0 + i, slot, nt, False)
    wt = wt_ref[...]
    return (wt[:, 0:1] * _slab_to_rows(gbuf.at[slot, 0], TM) + wt[:, 1:2] * _slab_to_rows(gbuf.at[slot, 1], TM))


_COMBINE_SCRATCH = [pltpu.VMEM((2, TOP_K, TM * SLAB, 128), F32), pltpu.SemaphoreType.DMA((2,))]


INPROJ_NC = 512


def _inproj_kernel(*refs, combine, nt, n_ctx_blocks=0):
    if combine:
        pos_ref, x_ref, y_hbm, wt_ref, modp_ref, g_ref, mod_ref, w_ref, xo_ref, p_ref, gbuf, sem = refs
        x = x_ref[...] + modp_ref[0][5:6] * _moe_combine(pos_ref, y_hbm, wt_ref, gbuf, sem, nt, 0)
        xo_ref[...] = x
    else:
        ctx_ref, lat_ref, g_ref, mod_ref, w_ref, xo_ref, p_ref = refs

        @pl.when(pl.program_id(0) < n_ctx_blocks)
        def _():
            xo_ref[...] = ctx_ref[...]

        @pl.when(pl.program_id(0) >= n_ctx_blocks)
        def _():
            xo_ref[...] = lat_ref[...]

        x = xo_ref[...]
    m = mod_ref[0]
    h = _rms_mod(x, g_ref[...], m[0:1], m[1:2]).astype(BF16)
    n = w_ref.shape[1]
    for j in range(0, n, INPROJ_NC):
        p_ref[:, j:j + INPROJ_NC] = _dot(h, w_ref[:, j:j + INPROJ_NC])


def inproj(x, g, mods_l, w_bf16, n_ctx_blocks, moe=None, mods_prev=None, casts=()):
    if moe is None:
        ctx2d, lat2d = x
        nt, d = ctx2d.shape[0] + lat2d.shape[0], ctx2d.shape[1]
    else:
        nt, d = x.shape
    n = w_bf16.shape[1]
    nblk = nt // TM
    row = pl.BlockSpec((TM, d), lambda i, *_: (i, 0))
    g_spec = pl.BlockSpec((1, d), lambda i, *_: (0, 0))
    w_spec = pl.BlockSpec((d, n), lambda i, *_: (0, 0), pipeline_mode=pl.Buffered(1))
    p_spec = pl.BlockSpec((TM, n), lambda i, *_: (i, 0))
    p_shape = jax.ShapeDtypeStruct((nt, n), F32)
    if moe is None:
        kern, c_in, c_out, c_shape, c_args = _with_casts(
            functools.partial(_inproj_kernel, combine=False, nt=nt, n_ctx_blocks=n_ctx_blocks), casts, nblk, 5, 2)
        ncb = n_ctx_blocks
        res = pl.pallas_call(
            kern,
            grid=(nblk,),
            in_specs=[pl.BlockSpec((TM, d), lambda i, *_: (jnp.minimum(i, ncb - 1), 0)),
                      pl.BlockSpec((TM, d), lambda i, *_: (jnp.maximum(i - ncb, 0), 0)),
                      g_spec, _mod_spec(n_ctx_blocks), w_spec] + c_in,
            out_specs=[row, p_spec] + c_out,
            out_shape=[jax.ShapeDtypeStruct((nt, d), F32), p_shape] + c_shape,
            compiler_params=_cparams(),
            name="inproj",
        )(ctx2d, lat2d, g.reshape(1, d), mods_l, w_bf16, *c_args)
        return res[0], res[1], res[2:]
    kern, c_in, c_out, c_shape, c_args = _with_casts(functools.partial(_inproj_kernel, combine=True, nt=nt),
                                                     casts, nblk, 8, 2)
    y_sorted, pos, wts_t = moe
    grid_spec = pltpu.PrefetchScalarGridSpec(
        num_scalar_prefetch=1,
        grid=(nblk,),
        in_specs=[row, pl.BlockSpec(memory_space=pl.ANY), pl.BlockSpec((TM, 8), lambda i, *_: (i, 0)),
                  _mod_spec(n_ctx_blocks), g_spec, _mod_spec(n_ctx_blocks), w_spec] + c_in,
        out_specs=[row, p_spec] + c_out,
        scratch_shapes=_COMBINE_SCRATCH,
    )
    res = pl.pallas_call(
        kern,
        grid_spec=grid_spec,
        out_shape=[jax.ShapeDtypeStruct((nt, d), F32), p_shape] + c_shape,
        compiler_params=_cparams(),
        name="combine_inproj",
    )(pos, x, y_sorted, wts_t, mods_prev, g.reshape(1, d), mods_l, w_bf16, *c_args)
    return res[0], res[1], res[2:]


CONV_RC = 64


def _layer_norm(x, g, b):
    mu = jnp.mean(x, axis=-1, keepdims=True)
    xc = x - mu
    var = jnp.mean(xc * xc, axis=-1, keepdims=True)
    return xc * lax.rsqrt(var + EPS) * g + b


def _even_kernel(u_ref, v_ref, a_ref, g_ref, ap_ref, gp_ref, an_ref, gn_ref,
                 lng_ref, lnb_ref, ws_ref, bs_ref, cw_ref, cb_ref, clg_ref, clb_ref,
                 ya_ref, yb_ref, hpad_ref, cacc_ref, shift_ref, *, n_ctx_blocks, n_blocks):
    i = pl.program_id(0)
    for c in range(TM // CHUNK):
        rows = slice(c * CHUNK, (c + 1) * CHUNK)
        vn = _layer_norm(jax.nn.gelu(v_ref[rows, :]), lng_ref[...], lnb_ref[...]).astype(BF16)
        for grp in range(A_GROUPS):
            cols = slice(grp * CHUNK, (grp + 1) * CHUNK)
            mixed = _dot(ws_ref[grp], vn[:, cols]) + bs_ref[grp]
            ya_ref[rows, cols] = (jax.nn.gelu(u_ref[rows, cols]) * mixed).astype(BF16)
    first = jnp.logical_or(i == 0, i == n_ctx_blocks)
    last = jnp.logical_or(i == n_ctx_blocks - 1, i == n_blocks - 1)
    hpad_ref[0:CONV_HALO, :] = jnp.where(first, 0.0, ap_ref[...] * jax.nn.sigmoid(gp_ref[...]))
    hpad_ref[CONV_HALO:CONV_HALO + TM, :] = a_ref[...] * jax.nn.sigmoid(g_ref[...])
    hpad_ref[CONV_HALO + TM:, :] = jnp.where(last, 0.0, an_ref[...] * jax.nn.sigmoid(gn_ref[...]))
    off = CONV_HALO - CONV_WIDTH // 2
    n_sh = shift_ref.shape[1]
    for cc in range(B_WIDTH // 128):
        cols = slice(cc * 128, (cc + 1) * 128)
        for j in range(1, 8):
            shift_ref[j - 1, :, cols] = hpad_ref[j:j + n_sh, cols]
    for cc in range(B_WIDTH // 128):
        cols = slice(cc * 128, (cc + 1) * 128)
        for rc in range(TM // CONV_RC):
            acc = jnp.zeros((CONV_RC, 128), F32)
            for k in range(CONV_WIDTH):
                q, j = divmod(k + off, 8)
                r0 = rc * CONV_RC + 8 * q
                tap = hpad_ref[r0:r0 + CONV_RC, cols] if j == 0 else shift_ref[j - 1, r0:r0 + CONV_RC, cols]
                acc = acc + cw_ref[k:k + 1, cols] * tap
            cacc_ref[rc * CONV_RC:(rc + 1) * CONV_RC, cols] = acc
    hc = _layer_norm(cacc_ref[...] + cb_ref[...], clg_ref[...], clb_ref[...])
    yb_ref[...] = (hc * jax.nn.sigmoid(hc)).astype(BF16)


def even_mixer(p, ln_g, ln_b, ws_bf16, bs_full, conv_w, conv_b, cln_g, cln_b, n_ctx_blocks, casts=()):
    nt = p.shape[0]
    nblk = nt // TM
    kern, c_in, c_out, c_shape, c_args = _with_casts(
        functools.partial(_even_kernel, n_ctx_blocks=n_ctx_blocks, n_blocks=nblk), casts, nblk, 16, 2)
    hb = TM // CONV_HALO
    last_h = nt // CONV_HALO - 1
    col = lambda j: pl.BlockSpec((TM, 1024), lambda i: (i, j))
    prev = lambda j: pl.BlockSpec((CONV_HALO, 1024), lambda i: (jnp.maximum(i * hb - 1, 0), j))
    nxt = lambda j: pl.BlockSpec((CONV_HALO, 1024), lambda i: (jnp.minimum((i + 1) * hb, last_h), j))
    vec = pl.BlockSpec((1, 1024), lambda i: (0, 0))
    out = pl.BlockSpec((TM, 1024), lambda i: (i, 0))
    res = pl.pallas_call(
        kern,
        grid=(nblk,),
        in_specs=[col(0), col(1), col(2), col(3), prev(2), prev(3), nxt(2), nxt(3),
                  vec, vec,
                  pl.BlockSpec((A_GROUPS, CHUNK, CHUNK), lambda i: (0, 0, 0)),
                  pl.BlockSpec((A_GROUPS, CHUNK, CHUNK), lambda i: (0, 0, 0)),
                  pl.BlockSpec((CONV_WIDTH, 1024), lambda i: (0, 0)),
                  vec, vec, vec] + c_in,
        out_specs=[out, out] + c_out,
        out_shape=[jax.ShapeDtypeStruct((nt, 1024), BF16)] * 2 + c_shape,
        scratch_shapes=[pltpu.VMEM((TM + 2 * CONV_HALO, 1024), F32), pltpu.VMEM((TM, 1024), F32),
                        pltpu.VMEM((7, TM + 2 * CONV_HALO - 8, 1024), F32)],
        compiler_params=_cparams(),
        name="even_mixer",
    )(p, p, p, p, p, p, p, p, ln_g.reshape(1, -1), ln_b.reshape(1, -1), ws_bf16, bs_full,
      conv_w, conv_b.reshape(1, -1), cln_g.reshape(1, -1), cln_b.reshape(1, -1), *c_args)
    return res[0], res[1], res[2:]


def _second_max(a0, a1, a2, a3):
    m01, n01 = jnp.maximum(a0, a1), jnp.minimum(a0, a1)
    m23, n23 = jnp.maximum(a2, a3), jnp.minimum(a2, a3)
    return jnp.maximum(m01, m23), jnp.maximum(jnp.minimum(m01, m23), jnp.maximum(n01, n23))


def _route_tile(logits, b_col):
    aff = jax.nn.sigmoid(logits)
    biased = aff + b_col
    row = lambda m, e: m[e:e + 1, :]
    g_sel = best = None
    for g in range(N_EXPERT_GROUPS):
        top1, top2 = _second_max(*[row(biased, EXPERTS_PER_GROUP * g + j) for j in range(EXPERTS_PER_GROUP)])
        score = top1 + top2
        if g == 0:
            g_sel, best = jnp.zeros(score.shape, jnp.int32), score
        else:
            upd = score > best
            g_sel, best = jnp.where(upd, g, g_sel), jnp.where(upd, score, best)

    def in_group(m, j):
        out = row(m, (N_EXPERT_GROUPS - 1) * EXPERTS_PER_GROUP + j)
        for g in range(N_EXPERT_GROUPS - 2, -1, -1):
            out = jnp.where(g_sel == g, row(m, EXPERTS_PER_GROUP * g + j), out)
        return out

    v = [in_group(biased, j) for j in range(EXPERTS_PER_GROUP)]
    a = [in_group(aff, j) for j in range(EXPERTS_PER_GROUP)]
    i1, b1, w1 = jnp.zeros(g_sel.shape, jnp.int32), v[0], a[0]
    for j in range(1, EXPERTS_PER_GROUP):
        upd = v[j] > b1
        i1, b1, w1 = jnp.where(upd, j, i1), jnp.where(upd, v[j], b1), jnp.where(upd, a[j], w1)
    first = i1 == 0
    i2, b2, w2 = jnp.where(first, 1, 0), jnp.where(first, v[1], v[0]), jnp.where(first, a[1], a[0])
    for j in range(1, EXPERTS_PER_GROUP):
        upd = jnp.logical_and(i1 != j, v[j] > b2)
        i2, b2, w2 = jnp.where(upd, j, i2), jnp.where(upd, v[j], b2), jnp.where(upd, a[j], w2)
    den = w1 + w2
    return EXPERTS_PER_GROUP * g_sel + i1, EXPERTS_PER_GROUP * g_sel + i2, w1 / den, w2 / den


def _rows8(r0, r1):
    sub = lax.broadcasted_iota(jnp.int32, (8, r0.shape[1]), 0)
    return jnp.where(sub == 0, r0, jnp.where(sub == 1, r1, jnp.zeros_like(r0)))


def _outproj_kernel(ya_ref, yb_ref, wa_ref, wb_ref, x_ref, mod_ref, g_ref, wr_ref, br_ref, tri_ref,
                    xo_ref, f_ref, e_ref, w_ref, r_ref, cnt_ref, carry_ref):
    i = pl.program_id(0)
    m = mod_ref[0]
    y = _dot(ya_ref[...], wa_ref[...]) + _dot(yb_ref[...], wb_ref[...])
    x = x_ref[...] + m[2:3] * y
    xo_ref[...] = x
    f = _rms_mod(x, g_ref[...], m[3:4], m[4:5])
    _rows_to_slab(f_ref, f, TM, 0)
    _zero_slab_padding(f_ref, TM)
    f_hi = f.astype(BF16)
    f_lo = (f - f_hi.astype(F32)).astype(BF16)
    wr = wr_ref[...]
    w_hi = wr.astype(BF16)
    w_lo = (wr - w_hi.astype(F32)).astype(BF16)
    logits = _dot_nt(w_hi, f_hi) + _dot_nt(w_lo, f_hi) + _dot_nt(w_hi, f_lo)
    e0, e1, w0, w1 = _route_tile(logits, br_ref[:, 0:1])

    @pl.when(i == 0)
    def _():
        carry_ref[...] = jnp.zeros(carry_ref.shape, F32)

    sub = lax.broadcasted_iota(jnp.int32, (N_EXPERTS, TM), 0)
    hot0, hot1 = sub == e0, sub == e1
    member = jnp.where(jnp.logical_or(hot0, hot1), 1.0, 0.0)
    before = _dot(member.astype(BF16), tri_ref[...]) + carry_ref[:, 0:1]
    r0 = jnp.sum(jnp.where(hot0, before, 0.0), axis=0, keepdims=True)
    r1 = jnp.sum(jnp.where(hot1, before, 0.0), axis=0, keepdims=True)
    carry_ref[...] = carry_ref[...] + jnp.sum(member, axis=1, keepdims=True)
    e_ref[...] = _rows8(e0, e1)
    w_ref[...] = _rows8(w0, w1)
    r_ref[...] = _rows8(r0.astype(jnp.int32), r1.astype(jnp.int32))
    cnt_ref[...] = carry_ref[...]


def outproj(ya, yb, wa, wb, x, mods_l, g_ffn, w_router_t, b_router_col, n_ctx_blocks, casts=()):
    nt, d = x.shape
    nblk = nt // TM
    kern, c_in, c_out, c_shape, c_args = _with_casts(_outproj_kernel, casts, nblk, 10, 6)
    half = pl.BlockSpec((TM, 1024), lambda i: (i, 0))
    wsp = pl.BlockSpec((1024, d), lambda i: (0, 0), pipeline_mode=pl.Buffered(1))
    row = pl.BlockSpec((TM, d), lambda i: (i, 0))
    r8 = pl.BlockSpec((8, TM), lambda i: (0, i))
    cnt = pl.BlockSpec((N_EXPERTS, 128), lambda i: (0, 0))
    tri = (jnp.arange(TM)[:, None] < jnp.arange(TM)[None, :]).astype(BF16)
    res = pl.pallas_call(
        kern,
        grid=(nblk,),
        in_specs=[half, half, wsp, wsp, row, _mod_spec(n_ctx_blocks),
                  pl.BlockSpec((1, d), lambda i: (0, 0)),
                  pl.BlockSpec((N_EXPERTS, d), lambda i: (0, 0)), cnt,
                  pl.BlockSpec((TM, TM), lambda i: (0, 0))] + c_in,
        out_specs=[row, pl.BlockSpec((TM * SLAB, 128), lambda i: (i, 0)), r8, r8, r8, cnt] + c_out,
        out_shape=[jax.ShapeDtypeStruct((nt, d), F32), jax.ShapeDtypeStruct((nt * SLAB, 128), F32),
                   jax.ShapeDtypeStruct((8, nt), jnp.int32), jax.ShapeDtypeStruct((8, nt), F32),
                   jax.ShapeDtypeStruct((8, nt), jnp.int32), jax.ShapeDtypeStruct((N_EXPERTS, 128), F32)] + c_shape,
        scratch_shapes=[pltpu.VMEM((N_EXPERTS, 128), F32)],
        compiler_params=_cparams(),
        name="outproj",
    )(ya, yb, wa, wb, x, mods_l, g_ffn.reshape(1, d), w_router_t, b_router_col, tri, *c_args)
    return res[:6], res[6:]


N_LOC = 3 * ATTN_BLOCK


def _rope(x, cos, sin):
    lane = lax.broadcasted_iota(jnp.int32, x.shape, 1)
    swapped = jnp.where(lane % 64 < 32, pltpu.roll(x, 96, axis=1), pltpu.roll(x, 32, axis=1))
    return x * cos + swapped * sin


def _attn_kernel(sink_ref, q_ref, kp_ref, kc_ref, kn_ref, vp_ref, vc_ref, vn_ref, kx_ref, vx_ref,
                 cosp_ref, cosc_ref, cosn_ref, sinp_ref, sinc_ref, sinn_ref, o_ref,
                 qs_ref, kbuf_ref, vbuf_ref, *, n_ctx_blocks, n_blocks, n_ctx):
    i = pl.program_id(0)
    nkeys = N_LOC + n_ctx
    nq = Q_PER_KV * ATTN_BLOCK
    is_lat = jnp.where(i >= n_ctx_blocks, 1, 0)
    prev_ok = jnp.where(i - 1 >= n_ctx_blocks, is_lat, 0)
    next_ok = jnp.where(i + 1 <= n_blocks - 1, is_lat, 0)
    qi = lax.broadcasted_iota(jnp.int32, (nq, nkeys), 0) & (ATTN_BLOCK - 1)
    kj = lax.broadcasted_iota(jnp.int32, (nq, nkeys), 1)
    rel = kj - ATTN_BLOCK - qi
    blk_ok = jnp.where(kj < ATTN_BLOCK, prev_ok, jnp.where(kj < 2 * ATTN_BLOCK, is_lat, next_ok))
    rel = jnp.where(blk_ok > 0, rel, WINDOW + 1)
    valid = jnp.logical_or(kj >= N_LOC, jnp.logical_and(rel >= -WINDOW, rel <= WINDOW))
    rowh = lax.broadcasted_iota(jnp.int32, (nq, 1), 0) // ATTN_BLOCK
    for h in range(N_KV_HEADS):
        hc = slice(h * HEAD_DIM, (h + 1) * HEAD_DIM)
        kbuf_ref[0:ATTN_BLOCK, :] = _rope(kp_ref[:, hc], cosp_ref[...], sinp_ref[...]).astype(BF16)
        kbuf_ref[ATTN_BLOCK:2 * ATTN_BLOCK, :] = _rope(kc_ref[:, hc], cosc_ref[...], sinc_ref[...]).astype(BF16)
        kbuf_ref[2 * ATTN_BLOCK:N_LOC, :] = _rope(kn_ref[:, hc], cosn_ref[...], sinn_ref[...]).astype(BF16)
        kbuf_ref[N_LOC:, :] = kx_ref[:, hc].astype(BF16)
        vbuf_ref[0:ATTN_BLOCK, :] = vp_ref[:, hc].astype(BF16)
        vbuf_ref[ATTN_BLOCK:2 * ATTN_BLOCK, :] = vc_ref[:, hc].astype(BF16)
        vbuf_ref[2 * ATTN_BLOCK:N_LOC, :] = vn_ref[:, hc].astype(BF16)
        vbuf_ref[N_LOC:, :] = vx_ref[:, hc].astype(BF16)
        sink = jnp.zeros((Q_PER_KV * ATTN_BLOCK, 1), F32)
        for gq in range(Q_PER_KV):
            head = h * Q_PER_KV + gq
            qc = slice(head * HEAD_DIM, (head + 1) * HEAD_DIM)
            qs_ref[gq * ATTN_BLOCK:(gq + 1) * ATTN_BLOCK, :] = _rope(
                q_ref[:, qc], cosc_ref[...], sinc_ref[...]).astype(BF16)
            sink = jnp.where(rowh == gq, sink_ref[head], sink)
        s = _dot_nt(qs_ref[...], kbuf_ref[...]) * ATTN_SCALE
        s = jnp.where(valid, s, NEG_INF)
        mx = jnp.maximum(jnp.max(s, axis=-1, keepdims=True), sink)
        p = jnp.exp(s - mx)
        den = jnp.sum(p, axis=-1, keepdims=True) + jnp.exp(sink - mx)
        o = _dot(p.astype(BF16), vbuf_ref[...]) / den
        for gq in range(Q_PER_KV):
            head = h * Q_PER_KV + gq
            o_ref[:, head * HEAD_DIM:(head + 1) * HEAD_DIM] = o[gq * ATTN_BLOCK:(gq + 1) * ATTN_BLOCK].astype(BF16)


def attention(p, sink, cos_t, sin_t, n_ctx, casts=()):
    nt = p.shape[0]
    nblk = nt // ATTN_BLOCK
    ncb = n_ctx // ATTN_BLOCK
    kern, c_in, c_out, c_shape, c_args = _with_casts(
        functools.partial(_attn_kernel, n_ctx_blocks=ncb, n_blocks=nblk, n_ctx=n_ctx), casts, nblk, 16, 1)
    kcol, vcol = 2048 // KV_WIDTH, 2048 // KV_WIDTH + 1
    pm = lambda i: jnp.maximum(i - 1, 0)
    nx = lambda i: jnp.minimum(i + 1, nblk - 1)
    kv = lambda f, c: pl.BlockSpec((ATTN_BLOCK, KV_WIDTH), lambda i, s: (f(i), c))
    tab = lambda f: pl.BlockSpec((ATTN_BLOCK, HEAD_DIM), lambda i, s: (f(i), 0))
    same = lambda i: i
    grid_spec = pltpu.PrefetchScalarGridSpec(
        num_scalar_prefetch=1,
        grid=(nblk,),
        in_specs=[pl.BlockSpec((ATTN_BLOCK, C_WIDTH), lambda i, s: (i, 0)),
                  kv(pm, kcol), kv(same, kcol), kv(nx, kcol), kv(pm, vcol), kv(same, vcol), kv(nx, vcol),
                  pl.BlockSpec((n_ctx, KV_WIDTH), lambda i, s: (0, kcol)),
                  pl.BlockSpec((n_ctx, KV_WIDTH), lambda i, s: (0, vcol)),
                  tab(pm), tab(same), tab(nx), tab(pm), tab(same), tab(nx)] + c_in,
        out_specs=[pl.BlockSpec((ATTN_BLOCK, C_WIDTH), lambda i, s: (i, 0))] + c_out,
        scratch_shapes=[pltpu.VMEM((Q_PER_KV * ATTN_BLOCK, HEAD_DIM), BF16),
                        pltpu.VMEM((N_LOC + n_ctx, HEAD_DIM), BF16),
                        pltpu.VMEM((N_LOC + n_ctx, HEAD_DIM), BF16)],
    )
    res = pl.pallas_call(
        kern,
        grid_spec=grid_spec,
        out_shape=[jax.ShapeDtypeStruct((nt, C_WIDTH), BF16)] + c_shape,
        compiler_params=_cparams(),
        name="attention",
    )(sink, p, p, p, p, p, p, p, p, p, cos_t, cos_t, cos_t, sin_t, sin_t, sin_t, *c_args)
    return res[0], res[1:]


def rope_tables(n_ctx, n_lat):
    f32 = np.float32
    rows = n_lat // GRID_W
    row = np.repeat(np.arange(rows, dtype=f32), GRID_W)
    col = np.tile(np.arange(GRID_W, dtype=f32), rows)
    half = HEAD_DIM // 2
    inv_freq = np.power(f32(ROPE_BASE), -np.arange(0, half, 2, dtype=f32) / f32(half)).astype(f32)
    ang_r = row[:, None] * inv_freq
    ang_c = col[:, None] * inv_freq
    cos = np.concatenate([np.cos(ang_r), np.cos(ang_r), np.cos(ang_c), np.cos(ang_c)], axis=-1)
    sin = np.concatenate([-np.sin(ang_r), np.sin(ang_r), -np.sin(ang_c), np.sin(ang_c)], axis=-1)
    cos = np.concatenate([np.ones((n_ctx, HEAD_DIM), f32), cos], axis=0).astype(f32)
    sin = np.concatenate([np.zeros((n_ctx, HEAD_DIM), f32), sin], axis=0).astype(f32)
    return jnp.asarray(cos), jnp.asarray(sin)


S_LH = S_CHUNK * S_GROUP_CH
S_HALF = S_GROUP_BATCH * 128


S_PW_ROWS = 4 * S_CHUNK
S_PITCH = S_CHUNK + 8


def _dot3_nt(x, a):
    x_hi, a_hi = x.astype(BF16), a.astype(BF16)
    x_lo, a_lo = (x - x_hi.astype(F32)).astype(BF16), (a - a_hi.astype(F32)).astype(BF16)
    return _dot_nt(x_hi, a_hi) + _dot_nt(x_lo, a_hi) + _dot_nt(x_hi, a_lo)


def _s5_kernel(u_ref, pw_ref, bc_ref, lam_ref, y_ref, s_ref, hf_ref, hr_ref,
               ws_ref, wo_ref, are_ref, aim_ref, t_ref, ug_ref, yg_ref, upad_ref, ypad_ref, *,
               n_chunks, n_ctx_chunks):
    L = S_CHUNK
    gpt = 128 // S_GROUP_CH
    ug_ref[:, n_chunks:, :] = jnp.zeros((S_GROUP_BATCH, ug_ref.shape[1] - n_chunks, S_LH), F32)

    def spread(c, carry):
        upad_ref[pl.ds(pl.multiple_of(c * S_PITCH, 8), L), :] = u_ref[pl.ds(pl.multiple_of(c * L, L), L), :]
        return carry
    lax.fori_loop(0, n_chunks, spread, 0, unroll=8)
    for s_tok in range(L):
        x = upad_ref[pl.ds(s_tok, n_chunks, stride=S_PITCH), :]
        dst = s_tok % gpt
        for g in range(S_GROUP_BATCH):
            k = (dst - g) % gpt
            r = x if k == 0 else pltpu.roll(x, k * S_GROUP_CH, axis=1)
            ug_ref[g, 0:n_chunks, s_tok * S_GROUP_CH:(s_tok + 1) * S_GROUP_CH] = (
                r[:, dst * S_GROUP_CH:(dst + 1) * S_GROUP_CH])

    def scaled_rows(g, x_re, x_im, row0, n_rows, emit):
        def body(r, carry):
            p_re = pw_ref[g, 0, pl.ds(row0 + r, 1), :]
            p_im = pw_ref[g, 1, pl.ds(row0 + r, 1), :]
            emit(pl.ds(pl.multiple_of(r * S_GROUP_CH, S_GROUP_CH), S_GROUP_CH),
                 x_re * p_re - x_im * p_im, x_re * p_im + x_im * p_re)
            return carry
        lax.fori_loop(0, n_rows, body, 0, unroll=4)

    def emit_ws(rows, re, im):
        ws_ref[rows, 0:128] = re.astype(BF16)
        ws_ref[rows, 128:256] = im.astype(BF16)

    def emit_wo(rows, re, im):
        wo_ref[rows, 0:128] = re.astype(BF16)
        wo_ref[rows, 128:256] = (-im).astype(BF16)

    def emit_a(rows, re, im):
        are_ref[rows, :] = re
        aim_ref[rows, :] = im

    for g in range(S_GROUP_BATCH):
        scaled_rows(g, bc_ref[g, 0], bc_ref[g, 1], 0, L, emit_ws)
        s = _dot(ug_ref[g].astype(BF16), ws_ref[...])
        s_ref[:, g * 128:(g + 1) * 128] = s[:, 0:128]
        s_ref[:, S_HALF + g * 128:S_HALF + (g + 1) * 128] = s[:, 128:256]
    hf_ref[...] = jnp.zeros(hf_ref.shape, F32)
    hr_ref[...] = jnp.zeros(hr_ref.shape, F32)
    lam = lam_ref[0]
    lam_re, lam_im = lam[:, :S_HALF], lam[:, S_HALF:]
    is_fwd = lax.broadcasted_iota(jnp.int32, (1, 2 * S_HALF), 1) % 128 < S_STATE

    def step(k, state):
        st_re, st_im = state
        cf = k
        cr = jnp.where(k < n_ctx_chunks, n_ctx_chunks - 1 - k, n_chunks - 1 - (k - n_ctx_chunks))
        st = jnp.concatenate([st_re, st_im], axis=1)
        hf_ref[pl.ds(cf, 1), :] = st
        hr_ref[pl.ds(cr, 1), :] = st
        s_in = jnp.where(is_fwd, s_ref[pl.ds(cf, 1), :], s_ref[pl.ds(cr, 1), :])
        new_re = lam_re * st_re - lam_im * st_im + s_in[:, :S_HALF]
        new_im = lam_re * st_im + lam_im * st_re + s_in[:, S_HALF:]
        return new_re, new_im

    zero = jnp.zeros((1, S_HALF), F32)
    lax.fori_loop(0, n_chunks, step, (zero, zero))
    hin = jnp.where(is_fwd, hf_ref[...], hr_ref[...]).astype(BF16)
    kwidth = 2 * L * S_GROUP_CH
    for g in range(S_GROUP_BATCH):
        hin_g = jnp.concatenate([hin[:, g * 128:(g + 1) * 128],
                                 hin[:, S_HALF + g * 128:S_HALF + (g + 1) * 128]], axis=1)
        c_re, c_im = bc_ref[g, 2], bc_ref[g, 3]
        scaled_rows(g, c_re, c_im, L, L, emit_wo)
        scaled_rows(g, c_re, c_im, 2 * L, 2 * L, emit_a)
        kern = _dot3_nt(bc_ref[g, 0], are_ref[...]) - _dot3_nt(bc_ref[g, 1], aim_ref[...])
        for s_tok in range(L):
            off = (L - 1 - s_tok) * S_GROUP_CH
            win = kern if off == 0 else pltpu.roll(kern, kwidth - off, axis=1)
            t_ref[s_tok * S_GROUP_CH:(s_tok + 1) * S_GROUP_CH, :] = win[:, :S_LH].astype(BF16)
        yg_ref[g] = _dot(ug_ref[g].astype(BF16), t_ref[...]) + _dot_nt(hin_g, wo_ref[...])
    lane_group = lax.broadcasted_iota(jnp.int32, (n_chunks, 128), 1) // S_GROUP_CH
    for s_tok in range(L):
        dst = s_tok % gpt
        cols = slice((s_tok // gpt) * 128, (s_tok // gpt + 1) * 128)
        z = None
        for g in range(S_GROUP_BATCH):
            piece = yg_ref[g, 0:n_chunks, cols]
            k = (g - dst) % gpt
            r = piece if k == 0 else pltpu.roll(piece, k * S_GROUP_CH, axis=1)
            z = r if z is None else jnp.where(lane_group == g, r, z)
        ypad_ref[pl.ds(s_tok, n_chunks, stride=S_PITCH), :] = z

    def pack(c, carry):
        y_ref[pl.ds(pl.multiple_of(c * L, L), L), :] = ypad_ref[pl.ds(pl.multiple_of(c * S_PITCH, 8), L), :]
        return carry
    lax.fori_loop(0, n_chunks, pack, 0, unroll=8)


def s5_scan(p, pw_tab, bc_tab, lam_rows, n_chunks, n_ctx_chunks):
    nt = p.shape[0]
    assert S_GROUP_BATCH * S_GROUP_CH == 128 and nt == n_chunks * S_CHUNK
    ncp = -(-n_chunks // 16) * 16
    lh = S_LH
    return pl.pallas_call(
        functools.partial(_s5_kernel, n_chunks=n_chunks, n_ctx_chunks=n_ctx_chunks),
        grid=(S_GROUPS // S_GROUP_BATCH,),
        in_specs=[pl.BlockSpec((nt, 128), lambda i: (0, C_WIDTH // 128 + i)),
                  pl.BlockSpec((S_GROUP_BATCH, 2, S_PW_ROWS, 128), lambda i: (i, 0, 0, 0)),
                  pl.BlockSpec((S_GROUP_BATCH, 4, S_GROUP_CH, 128), lambda i: (i, 0, 0, 0)),
                  pl.BlockSpec((1, 1, 2 * S_HALF), lambda i: (i, 0, 0))],
        out_specs=pl.BlockSpec((nt, 128), lambda i: (0, i)),
        out_shape=jax.ShapeDtypeStruct((nt, S_WIDTH), F32),
        scratch_shapes=[pltpu.VMEM((ncp, 2 * S_HALF), F32)] * 3 + [
            pltpu.VMEM((lh, 256), BF16), pltpu.VMEM((lh, 256), BF16),
            pltpu.VMEM((2 * lh, 128), F32), pltpu.VMEM((2 * lh, 128), F32), pltpu.VMEM((lh, lh), BF16),
            pltpu.VMEM((S_GROUP_BATCH, ncp, lh), F32), pltpu.VMEM((S_GROUP_BATCH, ncp, lh), F32),
            pltpu.VMEM((n_chunks * S_PITCH, 128), F32), pltpu.VMEM((n_chunks * S_PITCH, 128), F32)],
        compiler_params=_cparams(),
        name="s5_scan",
    )(p, pw_tab, bc_tab, lam_rows)


def s5_weights(lam_re, lam_im, log_dt, b_re, b_im, c_re, c_im):
    L = S_CHUNK
    lam = lax.complex(lam_re.astype(F32), lam_im.astype(F32))
    lam_dt = lam * jnp.exp(log_dt.astype(F32))[..., None]
    lam_bar = jnp.exp(lam_dt)
    b_bar = ((lam_bar - 1.0) / lam)[..., None] * lax.complex(b_re.astype(F32), b_im.astype(F32))
    c = lax.complex(c_re.astype(F32), c_im.astype(F32))
    n = np.arange(L)
    lag = np.arange(2 * L) - (L - 1)
    expo = np.concatenate([np.stack([L - 1 - n, n], axis=-1), np.stack([n + 1, L - n], axis=-1),
                           np.stack([lag, -lag], axis=-1)], axis=0)
    live = np.logical_and(expo >= 0, (np.arange(4 * L) < 4 * L - 1)[:, None])
    base = jnp.exp(lam_dt[None] * jnp.arange(L + 1, dtype=F32)[:, None, None, None])
    picked = jnp.stack([base[np.clip(expo[:, dr], 0, L), dr] for dr in range(2)], axis=1)
    pw = jnp.where(live[:, :, None, None], picked, 0.0)
    pw = jnp.transpose(pw, (2, 0, 1, 3)).reshape(S_GROUPS, S_PW_ROWS, 2 * S_STATE)
    pw_tab = jnp.stack([pw.real, pw.imag], axis=1)
    bt = jnp.transpose(b_bar, (1, 3, 0, 2)).reshape(S_GROUPS, S_GROUP_CH, 2 * S_STATE)
    ct = jnp.transpose(c, (1, 2, 0, 3)).reshape(S_GROUPS, S_GROUP_CH, 2 * S_STATE)
    bc_tab = jnp.stack([bt.real, bt.imag, ct.real, ct.imag], axis=1)
    lam_l = jnp.exp(lam_dt * L)
    nb = S_GROUPS // S_GROUP_BATCH
    lre = jnp.concatenate([lam_l[0].real, lam_l[1].real], axis=-1).reshape(nb, 1, S_HALF)
    lim = jnp.concatenate([lam_l[0].imag, lam_l[1].imag], axis=-1).reshape(nb, 1, S_HALF)
    return pw_tab, bc_tab, jnp.concatenate([lre, lim], axis=-1)


def _glu_kernel(y_ref, u_ref, d_ref, w_ref, b_ref, o_ref):
    z = jax.nn.gelu(y_ref[...] + d_ref[...] * u_ref[...])
    gate = jax.nn.sigmoid(_dot(z.astype(BF16), w_ref[...]) + b_ref[...])
    o_ref[...] = (z * gate).astype(BF16)


def s5_glu(y_ssm, p, d_skip, glu_w_bf16, glu_b, casts=()):
    nt = y_ssm.shape[0]
    vec = pl.BlockSpec((1, S_WIDTH), lambda i: (0, 0))
    kern, c_in, c_out, c_shape, c_args = _with_casts(_glu_kernel, casts, nt // TM, 5, 1)
    res = pl.pallas_call(
        kern,
        grid=(nt // TM,),
        in_specs=[pl.BlockSpec((TM, S_WIDTH), lambda i: (i, 0)),
                  pl.BlockSpec((TM, S_WIDTH), lambda i: (i, 1)),
                  vec, pl.BlockSpec((S_WIDTH, S_WIDTH), lambda i: (0, 0)), vec] + c_in,
        out_specs=[pl.BlockSpec((TM, S_WIDTH), lambda i: (i, 0))] + c_out,
        out_shape=[jax.ShapeDtypeStruct((nt, S_WIDTH), BF16)] + c_shape,
        compiler_params=_cparams(),
        name="s5_glu",
    )(y_ssm, p, d_skip.reshape(1, -1), glu_w_bf16, glu_b.reshape(1, -1), *c_args)
    return res[0], res[1:]


MOE_FCHUNKS = ((0, 512), (512, 512), (1024, 384))
MOE_YC = 512


def _dispatch_kernel(pos_ref, cnt_ref, pad_ref, start_ref, nu_ref, f_hbm, xs_hbm, fbuf, zero_ref, lsem, sem, zsem,
                     *, nt, n_blocks):
    i = pl.program_id(0)
    n_tiles = pl.num_programs(0)
    blk_rows = MOE_BM * SLAB
    tile_rows = TM * SLAB

    def tile_load(tile):
        return pltpu.make_async_copy(f_hbm.at[pl.ds(pl.multiple_of(tile * tile_rows, tile_rows), tile_rows), :],
                                     fbuf.at[tile % 3], lsem.at[tile % 3])

    def start_body(t, carry):
        src = fbuf.at[i % 3, pl.ds(pl.multiple_of(t * SLAB, SLAB), SLAB), :]
        for k in range(TOP_K):
            p = pl.multiple_of(pos_ref[k * nt + i * TM + t] * SLAB, SLAB)
            pltpu.make_async_copy(src, xs_hbm.at[pl.ds(p, SLAB), :], sem.at[i % 2]).start()
        return carry

    def wait_tile(slot):
        for _ in range(TOP_K):
            pltpu.make_async_copy(fbuf.at[0], xs_hbm.at[pl.ds(0, tile_rows), :], sem.at[slot]).wait()

    def pad_copies(start):
        for e in range(N_EXPERTS):
            def body(r, carry):
                p = pl.multiple_of((start_ref[e] + r) * SLAB, SLAB)
                cp = pltpu.make_async_copy(zero_ref.at[pl.ds(0, SLAB), :], xs_hbm.at[pl.ds(p, SLAB), :], zsem)
                if start:
                    cp.start()
                else:
                    cp.wait()
                return carry
            lax.fori_loop(cnt_ref[e], pad_ref[e], body, 0)

        def tail(blk, carry):
            p = pl.multiple_of(blk * blk_rows, blk_rows)
            cp = pltpu.make_async_copy(zero_ref, xs_hbm.at[pl.ds(p, blk_rows), :], zsem)
            if start:
                cp.start()
            else:
                cp.wait()
            return carry
        lax.fori_loop(nu_ref[0], n_blocks, tail, 0)

    @pl.when(i == 0)
    def _():
        tile_load(0).start()
        zero_ref[...] = jnp.zeros(zero_ref.shape, F32)
        pad_copies(True)

    @pl.when(i + 1 < n_tiles)
    def _():
        tile_load(i + 1).start()

    tile_load(i).wait()
    lax.fori_loop(0, TM, start_body, 0, unroll=8)

    @pl.when(i > 0)
    def _():
        wait_tile((i - 1) % 2)

    @pl.when(i == n_tiles - 1)
    def _():
        wait_tile(i % 2)

    @pl.when(i == 0)
    def _():
        pad_copies(False)


def moe_dispatch(f_slab, pos, counts, padded, pad_start, n_used, n_blocks):
    nt = f_slab.shape[0] // SLAB
    grid_spec = pltpu.PrefetchScalarGridSpec(
        num_scalar_prefetch=5,
        grid=(nt // TM,),
        in_specs=[pl.BlockSpec(memory_space=pl.ANY)],
        out_specs=pl.BlockSpec(memory_space=pl.ANY),
        scratch_shapes=[pltpu.VMEM((3, TM * SLAB, 128), F32), pltpu.VMEM((MOE_BM * SLAB, 128), F32),
                        pltpu.SemaphoreType.DMA((3,)), pltpu.SemaphoreType.DMA((2,)), pltpu.SemaphoreType.DMA],
    )
    return pl.pallas_call(
        functools.partial(_dispatch_kernel, nt=nt, n_blocks=n_blocks),
        grid_spec=grid_spec,
        out_shape=jax.ShapeDtypeStruct((n_blocks * MOE_BM * SLAB, 128), F32),
        compiler_params=_cparams(),
        name="moe_dispatch",
    )(pos, counts, padded, pad_start, n_used, f_slab)


def _moe_kernel(be_ref, nu_ref, xs_ref, wg_ref, wu_ref, wd_ref, y_ref, hbuf):
    @pl.when(pl.program_id(0) >= nu_ref[0])
    def _():
        y_ref[...] = jnp.zeros(y_ref.shape, F32)

    @pl.when(pl.program_id(0) < nu_ref[0])
    def _():
        x = _slab_to_rows(xs_ref, MOE_BM).astype(BF16)
        for f0, fw in MOE_FCHUNKS:
            hg = _dot(x, wg_ref[0, :, f0:f0 + fw])
            hu = _dot(x, wu_ref[0, :, f0:f0 + fw])
            hbuf[:, f0:f0 + fw] = (hg * jax.nn.sigmoid(hg) * hu).astype(BF16)
        for c0 in range(0, D_MODEL, MOE_YC):
            _rows_to_slab(y_ref, _dot(hbuf[...], wd_ref[0, :, c0:c0 + MOE_YC]), MOE_BM, c0 // 128)
        _zero_slab_padding(y_ref, MOE_BM)


def moe_experts(x_sorted, block_e, n_used, wg, wu, wd):
    n_blocks = block_e.shape[0]
    d, fexp = wg.shape[1], wg.shape[2]
    blk = lambda b, be, nu: (jnp.minimum(b, nu[0] - 1), 0)
    wsel = lambda b, be, nu: (be[jnp.minimum(b, nu[0] - 1)], 0, 0)
    grid_spec = pltpu.PrefetchScalarGridSpec(
        num_scalar_prefetch=2,
        grid=(n_blocks,),
        in_specs=[pl.BlockSpec((MOE_BM * SLAB, 128), blk),
                  pl.BlockSpec((1, d, fexp), wsel), pl.BlockSpec((1, d, fexp), wsel),
                  pl.BlockSpec((1, fexp, d), wsel)],
        out_specs=pl.BlockSpec((MOE_BM * SLAB, 128), lambda b, be, nu: (b, 0)),
        scratch_shapes=[pltpu.VMEM((MOE_BM, fexp), BF16)],
    )
    return pl.pallas_call(
        _moe_kernel,
        grid_spec=grid_spec,
        out_shape=jax.ShapeDtypeStruct(x_sorted.shape, F32),
        compiler_params=_cparams(),
        name="moe_experts",
    )(block_e, n_used, x_sorted, wg, wu, wd)


def moe_layout(eidx, rank, cnt):
    nt = eidx.shape[1]
    n_blocks = -(-(nt * TOP_K) // MOE_BM) + N_EXPERTS
    counts = cnt[:, 0].astype(jnp.int32)
    padded = (counts + MOE_BM - 1) // MOE_BM * MOE_BM
    pad_end = jnp.cumsum(padded)
    pad_start = pad_end - padded
    hot = eidx[:TOP_K, :, None] == jnp.arange(N_EXPERTS, dtype=jnp.int32)
    pos = (rank[:TOP_K] + jnp.sum(jnp.where(hot, pad_start, 0), axis=-1)).reshape(-1)
    blk_start = jnp.arange(n_blocks, dtype=jnp.int32) * MOE_BM
    block_e = jnp.minimum(jnp.sum(blk_start[:, None] >= pad_end[None, :], axis=1), N_EXPERTS - 1).astype(jnp.int32)
    n_used = (pad_end[-1:] // MOE_BM).astype(jnp.int32)
    return pos, counts, padded, pad_start, block_e, n_used


def _final_kernel(pos_ref, x_ref, y_hbm, wt_ref, mod_ref, g_ref, out_ref, gbuf, sem, *, nt, tile0):
    x = x_ref[...] + mod_ref[0][5:6] * _moe_combine(pos_ref, y_hbm, wt_ref, gbuf, sem, nt, tile0)
    ms = jnp.mean(x * x, axis=-1, keepdims=True)
    out_ref[...] = x * lax.rsqrt(ms + EPS) * g_ref[...]


def final_norm(x, moe, mods_l, g_final, n_ctx_blocks):
    nt, d = x.shape
    nlat = nt // TM - n_ctx_blocks
    y_sorted, pos, wts_t = moe
    grid_spec = pltpu.PrefetchScalarGridSpec(
        num_scalar_prefetch=1,
        grid=(nlat,),
        in_specs=[pl.BlockSpec((TM, d), lambda i, *_: (i + n_ctx_blocks, 0)),
                  pl.BlockSpec(memory_space=pl.ANY),
                  pl.BlockSpec((TM, 8), lambda i, *_: (i + n_ctx_blocks, 0)),
                  pl.BlockSpec((1, 6, d), lambda i, *_: (0, 0, 0)),
                  pl.BlockSpec((1, d), lambda i, *_: (0, 0))],
        out_specs=pl.BlockSpec((TM, d), lambda i, *_: (i, 0)),
        scratch_shapes=_COMBINE_SCRATCH,
    )
    return pl.pallas_call(
        functools.partial(_final_kernel, nt=nt, tile0=n_ctx_blocks),
        grid_spec=grid_spec,
        out_shape=jax.ShapeDtypeStruct((nlat * TM, d), F32),
        compiler_params=_cparams(),
        name="final_norm",
    )(pos, x, y_sorted, wts_t, mods_l, g_final.reshape(1, d))


def kernel(x, c, ctx, c_ctx, w_mod, b_mod, g_mix, g_ffn, w_in_even, w_out_even, sgu_ln_g, sgu_ln_b, sgu_w, sgu_b, conv_w, conv_b, conv_ln_g, conv_ln_b, w_in_odd, w_out_odd, attn_sink, ssm_lam_re, ssm_lam_im, ssm_log_dt, ssm_b_re, ssm_b_im, ssm_c_re, ssm_c_im, ssm_d, glu_w, glu_b, w_router, b_router, w_gate, w_up, w_down, g_final):
    bsz, n_lat, d = x.shape
    n_ctx = ctx.shape[1]
    assert bsz == 1 and d == D_MODEL and n_ctx % TM == 0 and n_lat % TM == 0
    nt = n_ctx + n_lat
    ncb = n_ctx // TM
    n_chunks = nt // S_CHUNK
    ncp = -(-n_chunks // 16) * 16

    xs = (ctx[0], x[0])
    cond8 = jnp.concatenate([c, c_ctx[None, :], jnp.zeros((6, d), F32)], axis=0)
    mods = adaln_all(cond8, w_mod, b_mod)[:, :2].reshape(DEPTH, 2, 6, d)
    cos_t, sin_t = rope_tables(n_ctx, n_lat)
    w_router_t = w_router.T
    b_router_col = jnp.broadcast_to(b_router.astype(F32)[:, None], (N_EXPERTS, 128))
    fexp = w_gate.shape[-1]
    w_stack = (w_gate.reshape(-1, fexp), w_up.reshape(-1, fexp), w_down.reshape(-1, d))
    expert_bf = {}

    def job(layer, which):
        return (w_stack[which], layer)

    def done(jobs, outs):
        for (layer, which), o in zip(jobs, outs):
            expert_bf[(layer, which)] = o

    moe = None
    for l in range(DEPTH):
        j = l // 2
        odd = l % 2 == 1
        if odd:
            wi = w_in_odd[j]
            w_in = jnp.concatenate([wi[:, :C_WIDTH], wi[:, C_WIDTH + 2 * KV_WIDTH:],
                                    wi[:, C_WIDTH:C_WIDTH + 2 * KV_WIDTH]], axis=1).astype(BF16)
            w_out = w_out_odd[j].astype(BF16)
        else:
            w_in = w_in_even[j].astype(BF16)
            w_out = w_out_even[j].astype(BF16)
        jobs = [(l, 0)] if (l == 0 or odd) else []
        x_new, p, outs = inproj(xs, g_mix[l], mods[l], w_in, ncb, moe, mods[l - 1] if l > 0 else None,
                                casts=[job(*jb) for jb in jobs])
        done(jobs, outs)
        if x_new is not None:
            xs = x_new
        if odd:
            jobs = [(l, 1)]
            ya, outs = attention(p, attn_sink[j], cos_t, sin_t, n_ctx, casts=[job(*jb) for jb in jobs])
            done(jobs, outs)
            pw_tab, bc_tab, lam_rows = s5_weights(ssm_lam_re[j], ssm_lam_im[j], ssm_log_dt[j], ssm_b_re[j],
                                                  ssm_b_im[j], ssm_c_re[j], ssm_c_im[j])
            y_ssm = s5_scan(p, pw_tab, bc_tab, lam_rows, n_chunks, n_ctx // S_CHUNK)
            jobs = [(l + 1, 0)] if l + 1 < DEPTH else []
            yb, outs = s5_glu(y_ssm, p, ssm_d[j], glu_w[j].astype(BF16), glu_b[j], casts=[job(*jb) for jb in jobs])
            done(jobs, outs)
        else:
            bs_full = jnp.broadcast_to(sgu_b[j][:, :, None], (A_GROUPS, CHUNK, CHUNK)).astype(F32)
            jobs = [(l, 1)]
            ya, yb, outs = even_mixer(p, sgu_ln_g[j], sgu_ln_b[j], sgu_w[j].astype(BF16), bs_full,
                                      conv_w[j], conv_b[j], conv_ln_g[j], conv_ln_b[j], ncb,
                                      casts=[job(*jb) for jb in jobs])
            done(jobs, outs)
        jobs = [(l, 2)]
        (xs, f_slab, eidx, wts, rank, cnt), outs = outproj(
            ya, yb, w_out[:1024], w_out[1024:], xs, mods[l], g_ffn[l], w_router_t, b_router_col, ncb,
            casts=[job(*jb) for jb in jobs])
        done(jobs, outs)
        pos, counts, padded, pad_start, block_e, n_used = moe_layout(eidx, rank, cnt)
        x_sorted = moe_dispatch(f_slab, pos, counts, padded, pad_start, n_used, block_e.shape[0])
        y_sorted = moe_experts(x_sorted, block_e, n_used,
                               expert_bf[(l, 0)].reshape(N_EXPERTS, d, fexp),
                               expert_bf[(l, 1)].reshape(N_EXPERTS, d, fexp),
                               expert_bf[(l, 2)].reshape(N_EXPERTS, fexp, d))
        moe = (y_sorted, pos, wts.T)
    out = final_norm(xs, moe, mods[DEPTH - 1], g_final, ncb)
    return out.reshape(bsz, n_lat, d)
```

```python
import functools
import math

import jax
import jax.numpy as jnp
import numpy as np
from jax import lax
from jax.experimental import pallas as pl
from jax.experimental.pallas import tpu as pltpu

F32 = jnp.float32
BF16 = jnp.bfloat16

D_MODEL = 2048
DEPTH = 4
GRID_W = 64
EPS = 1e-6
NEG_INF = -1e30

A_WIDTH = 1024
A_GROUPS = 8
CHUNK = 128
B_WIDTH = 1024
CONV_WIDTH = 31
CONV_HALO = 16

HEAD_DIM = 128
N_Q_HEADS = 8
N_KV_HEADS = 2
Q_PER_KV = 4
C_WIDTH = 1024
KV_WIDTH = 256
WINDOW = 128
ATTN_BLOCK = 128
ROPE_BASE = 10000.0
ATTN_SCALE = HEAD_DIM ** -0.5
S_WIDTH = 1024
S_GROUP_CH = 16
S_GROUPS = 64
S_STATE = 64
S_CHUNK = 32
S_GROUP_BATCH = 8
ODD_IN = C_WIDTH + 2 * KV_WIDTH + S_WIDTH

N_EXPERTS = 16
N_EXPERT_GROUPS = 4
EXPERTS_PER_GROUP = 4
TOP_K = 2
D_EXPERT = 1408
MOE_BM = 256

TM = 256
VMEM_LIMIT = 56 * 1024 * 1024


def _cparams(n_axes=1, vmem=VMEM_LIMIT):
    return pltpu.CompilerParams(dimension_semantics=("arbitrary",) * n_axes, vmem_limit_bytes=vmem)


def _dot(a, b):
    return jnp.dot(a, b, preferred_element_type=F32)


def _dot_nt(a, b):
    return lax.dot_general(a, b, (((1,), (1,)), ((), ())), preferred_element_type=F32)


ADALN_TN = 1024


def _adaln_kernel(cond_ref, w_ref, b_ref, o_ref):
    c = cond_ref[...]
    s = (c * jax.nn.sigmoid(c)).astype(BF16)
    o_ref[0] = _dot(s, w_ref[0].astype(BF16)) + b_ref[0]


def adaln_all(cond8, w_mod, b_mod):
    depth, d, n6 = w_mod.shape
    return pl.pallas_call(
        _adaln_kernel,
        grid=(depth, n6 // ADALN_TN),
        in_specs=[
            pl.BlockSpec((8, d), lambda l, j: (0, 0)),
            pl.BlockSpec((1, d, ADALN_TN), lambda l, j: (l, 0, j)),
            pl.BlockSpec((1, 1, ADALN_TN), lambda l, j: (l, 0, j)),
        ],
        out_specs=pl.BlockSpec((1, 8, ADALN_TN), lambda l, j: (l, 0, j)),
        out_shape=jax.ShapeDtypeStruct((depth, 8, n6), F32),
        compiler_params=_cparams(2),
        name="adaln",
    )(cond8, w_mod, b_mod.reshape(depth, 1, n6))


def _with_casts(kernel_fn, casts, n_steps, n_lead, n_out):
    k = len(casts)
    in_specs, out_specs, out_shapes = [], [], []
    steps = 1 << (n_steps.bit_length() - 1)
    for w, layer in casts:
        rows = w.shape[0] // DEPTH
        assert rows % (steps * 16) == 0
        blk = (rows // steps, w.shape[1])
        in_specs.append(pl.BlockSpec(blk, lambda i, *_, base=layer * steps: (base + jnp.minimum(i, steps - 1), 0)))
        out_specs.append(pl.BlockSpec(blk, lambda i, *_: (jnp.minimum(i, steps - 1), 0)))
        out_shapes.append(jax.ShapeDtypeStruct((rows, w.shape[1]), BF16))

    def wrapped(*refs):
        lead, rest = refs[:n_lead], refs[n_lead:]
        cast_in, rest = rest[:k], rest[k:]
        outs, rest = rest[:n_out], rest[n_out:]
        cast_out, scratch = rest[:k], rest[k:]
        kernel_fn(*lead, *outs, *scratch)
        for src, dst in zip(cast_in, cast_out):
            dst[...] = src[...].astype(BF16)

    return wrapped, in_specs, out_specs, out_shapes, [w for w, _ in casts]


def _mod_spec(n_ctx_blocks):
    return pl.BlockSpec((1, 6, D_MODEL), lambda i, *_: (jnp.where(i < n_ctx_blocks, 1, 0), 0, 0))


def _rms_mod(x, g, shift, scale):
    ms = jnp.mean(x * x, axis=-1, keepdims=True)
    y = x * lax.rsqrt(ms + EPS) * g
    return y * (1.0 + scale) + shift


SLAB_DATA = D_MODEL // 128
SLAB = SLAB_DATA + 4


def _slab_to_rows(ref, n_rows):
    return jnp.concatenate([ref[pl.ds(c, n_rows, stride=SLAB), :] for c in range(SLAB_DATA)], axis=1)


def _rows_to_slab(ref, val, n_rows, c0):
    for c in range(val.shape[1] // 128):
        ref[pl.ds(c0 + c, n_rows, stride=SLAB), :] = val[:, c * 128:(c + 1) * 128]


def _zero_slab_padding(ref, n_rows):
    for c in range(SLAB_DATA, SLAB):
        ref[pl.ds(c, n_rows, stride=SLAB), :] = jnp.zeros((n_rows, 128), F32)


def _expert_row_gather(pos_ref, y_hbm, gbuf, sem, tile, slot, nt, start):
    if not start:
        for k in range(TOP_K):
            pltpu.make_async_copy(y_hbm.at[pl.ds(0, TM * SLAB), :], gbuf.at[slot, k], sem.at[slot]).wait()
        return

    def body(t, carry):
        for k in range(TOP_K):
            p = pl.multiple_of(pos_ref[k * nt + tile * TM + t] * SLAB, SLAB)
            pltpu.make_async_copy(y_hbm.at[pl.ds(p, SLAB), :],
                                  gbuf.at[slot, k, pl.ds(pl.multiple_of(t * SLAB, SLAB), SLAB), :],
                                  sem.at[slot]).start()
        return carry
    lax.fori_loop(0, TM, body, 0, unroll=8)


def _moe_combine(pos_ref, y_hbm, wt_ref, gbuf, sem, nt, tile0):
    i = pl.program_id(0)
    slot = i % 2

    @pl.when(i == 0)
    def _():
        _expert_row_gather(pos_ref, y_hbm, gbuf, sem, tile0, 0, nt, True)

    @pl.when(i + 1 < pl.num_programs(0))
    def _():
        _expert_row_gather(pos_ref, y_hbm, gbuf, sem, tile0 + i + 1, 1 - slot, nt, True)

    _expert_row_gather(pos_ref, y_hbm, gbuf, sem, tile0 + i, slot, nt, False)
    wt = wt_ref[...]
    return (wt[:, 0:1] * _slab_to_rows(gbuf.at[slot, 0], TM) + wt[:, 1:2] * _slab_to_rows(gbuf.at[slot, 1], TM))


_COMBINE_SCRATCH = [pltpu.VMEM((2, TOP_K, TM * SLAB, 128), F32), pltpu.SemaphoreType.DMA((2,))]


INPROJ_NC = 512


def _inproj_kernel(*refs, combine, nt, n_ctx_blocks=0):
    if combine:
        pos_ref, x_ref, y_hbm, wt_ref, modp_ref, g_ref, mod_ref, w_ref, xo_ref, p_ref, gbuf, sem = refs
        x = x_ref[...] + modp_ref[0][5:6] * _moe_combine(pos_ref, y_hbm, wt_ref, gbuf, sem, nt, 0)
        xo_ref[...] = x
    else:
        ctx_ref, lat_ref, g_ref, mod_ref, w_ref, xo_ref, p_ref = refs

        @pl.when(pl.program_id(0) < n_ctx_blocks)
        def _():
            xo_ref[...] = ctx_ref[...]

        @pl.when(pl.program_id(0) >= n_ctx_blocks)
        def _():
            xo_ref[...] = lat_ref[...]

        x = xo_ref[...]
    m = mod_ref[0]
    h = _rms_mod(x, g_ref[...], m[0:1], m[1:2]).astype(BF16)
    n = w_ref.shape[1]
    for j in range(0, n, INPROJ_NC):
        p_ref[:, j:j + INPROJ_NC] = _dot(h, w_ref[:, j:j + INPROJ_NC])


def inproj(x, g, mods_l, w_bf16, n_ctx_blocks, moe=None, mods_prev=None, casts=()):
    if moe is None:
        ctx2d, lat2d = x
        nt, d = ctx2d.shape[0] + lat2d.shape[0], ctx2d.shape[1]
    else:
        nt, d = x.shape
    n = w_bf16.shape[1]
    nblk = nt // TM
    row = pl.BlockSpec((TM, d), lambda i, *_: (i, 0))
    g_spec = pl.BlockSpec((1, d), lambda i, *_: (0, 0))
    w_spec = pl.BlockSpec((d, n), lambda i, *_: (0, 0), pipeline_mode=pl.Buffered(1))
    p_spec = pl.BlockSpec((TM, n), lambda i, *_: (i, 0))
    p_shape = jax.ShapeDtypeStruct((nt, n), F32)
    if moe is None:
        kern, c_in, c_out, c_shape, c_args = _with_casts(
            functools.partial(_inproj_kernel, combine=False, nt=nt, n_ctx_blocks=n_ctx_blocks), casts, nblk, 5, 2)
        ncb = n_ctx_blocks
        res = pl.pallas_call(
            kern,
            grid=(nblk,),
            in_specs=[pl.BlockSpec((TM, d), lambda i, *_: (jnp.minimum(i, ncb - 1), 0)),
                      pl.BlockSpec((TM, d), lambda i, *_: (jnp.maximum(i - ncb, 0), 0)),
                      g_spec, _mod_spec(n_ctx_blocks), w_spec] + c_in,
            out_specs=[row, p_spec] + c_out,
            out_shape=[jax.ShapeDtypeStruct((nt, d), F32), p_shape] + c_shape,
            compiler_params=_cparams(),
            name="inproj",
        )(ctx2d, lat2d, g.reshape(1, d), mods_l, w_bf16, *c_args)
        return res[0], res[1], res[2:]
    kern, c_in, c_out, c_shape, c_args = _with_casts(functools.partial(_inproj_kernel, combine=True, nt=nt),
                                                     casts, nblk, 8, 2)
    y_sorted, pos, wts_t = moe
    grid_spec = pltpu.PrefetchScalarGridSpec(
        num_scalar_prefetch=1,
        grid=(nblk,),
        in_specs=[row, pl.BlockSpec(memory_space=pl.ANY), pl.BlockSpec((TM, 8), lambda i, *_: (i, 0)),
                  _mod_spec(n_ctx_blocks), g_spec, _mod_spec(n_ctx_blocks), w_spec] + c_in,
        out_specs=[row, p_spec] + c_out,
        scratch_shapes=_COMBINE_SCRATCH,
    )
    res = pl.pallas_call(
        kern,
        grid_spec=grid_spec,
        out_shape=[jax.ShapeDtypeStruct((nt, d), F32), p_shape] + c_shape,
        compiler_params=_cparams(),
        name="combine_inproj",
    )(pos, x, y_sorted, wts_t, mods_prev, g.reshape(1, d), mods_l, w_bf16, *c_args)
    return res[0], res[1], res[2:]


CONV_RC = 64


def _layer_norm(x, g, b):
    mu = jnp.mean(x, axis=-1, keepdims=True)
    xc = x - mu
    var = jnp.mean(xc * xc, axis=-1, keepdims=True)
    return xc * lax.rsqrt(var + EPS) * g + b


def _even_kernel(u_ref, v_ref, a_ref, g_ref, ap_ref, gp_ref, an_ref, gn_ref,
                 lng_ref, lnb_ref, ws_ref, bs_ref, cw_ref, cb_ref, clg_ref, clb_ref,
                 ya_ref, yb_ref, hpad_ref, cacc_ref, shift_ref, *, n_ctx_blocks, n_blocks):
    i = pl.program_id(0)
    for c in range(TM // CHUNK):
        rows = slice(c * CHUNK, (c + 1) * CHUNK)
        vn = _layer_norm(jax.nn.gelu(v_ref[rows, :]), lng_ref[...], lnb_ref[...]).astype(BF16)
        for grp in range(A_GROUPS):
            cols = slice(grp * CHUNK, (grp + 1) * CHUNK)
            mixed = _dot(ws_ref[grp], vn[:, cols]) + bs_ref[grp]
            ya_ref[rows, cols] = (jax.nn.gelu(u_ref[rows, cols]) * mixed).astype(BF16)
    first = jnp.logical_or(i == 0, i == n_ctx_blocks)
    last = jnp.logical_or(i == n_ctx_blocks - 1, i == n_blocks - 1)
    hpad_ref[0:CONV_HALO, :] = jnp.where(first, 0.0, ap_ref[...] * jax.nn.sigmoid(gp_ref[...]))
    hpad_ref[CONV_HALO:CONV_HALO + TM, :] = a_ref[...] * jax.nn.sigmoid(g_ref[...])
    hpad_ref[CONV_HALO + TM:, :] = jnp.where(last, 0.0, an_ref[...] * jax.nn.sigmoid(gn_ref[...]))
    off = CONV_HALO - CONV_WIDTH // 2
    n_sh = shift_ref.shape[1]
    for cc in range(B_WIDTH // 128):
        cols = slice(cc * 128, (cc + 1) * 128)
        for j in range(1, 8):
            shift_ref[j - 1, :, cols] = hpad_ref[j:j + n_sh, cols]
    for cc in range(B_WIDTH // 128):
        cols = slice(cc * 128, (cc + 1) * 128)
        for rc in range(TM // CONV_RC):
            acc = jnp.zeros((CONV_RC, 128), F32)
            for k in range(CONV_WIDTH):
                q, j = divmod(k + off, 8)
                r0 = rc * CONV_RC + 8 * q
                tap = hpad_ref[r0:r0 + CONV_RC, cols] if j == 0 else shift_ref[j - 1, r0:r0 + CONV_RC, cols]
                acc = acc + cw_ref[k:k + 1, cols] * tap
            cacc_ref[rc * CONV_RC:(rc + 1) * CONV_RC, cols] = acc
    hc = _layer_norm(cacc_ref[...] + cb_ref[...], clg_ref[...], clb_ref[...])
    yb_ref[...] = (hc * jax.nn.sigmoid(hc)).astype(BF16)


def even_mixer(p, ln_g, ln_b, ws_bf16, bs_full, conv_w, conv_b, cln_g, cln_b, n_ctx_blocks, casts=()):
    nt = p.shape[0]
    nblk = nt // TM
    kern, c_in, c_out, c_shape, c_args = _with_casts(
        functools.partial(_even_kernel, n_ctx_blocks=n_ctx_blocks, n_blocks=nblk), casts, nblk, 16, 2)
    hb = TM // CONV_HALO
    last_h = nt // CONV_HALO - 1
    col = lambda j: pl.BlockSpec((TM, 1024), lambda i: (i, j))
    prev = lambda j: pl.BlockSpec((CONV_HALO, 1024), lambda i: (jnp.maximum(i * hb - 1, 0), j))
    nxt = lambda j: pl.BlockSpec((CONV_HALO, 1024), lambda i: (jnp.minimum((i + 1) * hb, last_h), j))
    vec = pl.BlockSpec((1, 1024), lambda i: (0, 0))
    out = pl.BlockSpec((TM, 1024), lambda i: (i, 0))
    res = pl.pallas_call(
        kern,
        grid=(nblk,),
        in_specs=[col(0), col(1), col(2), col(3), prev(2), prev(3), nxt(2), nxt(3),
                  vec, vec,
                  pl.BlockSpec((A_GROUPS, CHUNK, CHUNK), lambda i: (0, 0, 0)),
                  pl.BlockSpec((A_GROUPS, CHUNK, CHUNK), lambda i: (0, 0, 0)),
                  pl.BlockSpec((CONV_WIDTH, 1024), lambda i: (0, 0)),
                  vec, vec, vec] + c_in,
        out_specs=[out, out] + c_out,
        out_shape=[jax.ShapeDtypeStruct((nt, 1024), BF16)] * 2 + c_shape,
        scratch_shapes=[pltpu.VMEM((TM + 2 * CONV_HALO, 1024), F32), pltpu.VMEM((TM, 1024), F32),
                        pltpu.VMEM((7, TM + 2 * CONV_HALO - 8, 1024), F32)],
        compiler_params=_cparams(),
        name="even_mixer",
    )(p, p, p, p, p, p, p, p, ln_g.reshape(1, -1), ln_b.reshape(1, -1), ws_bf16, bs_full,
      conv_w, conv_b.reshape(1, -1), cln_g.reshape(1, -1), cln_b.reshape(1, -1), *c_args)
    return res[0], res[1], res[2:]


def _second_max(a0, a1, a2, a3):
    m01, n01 = jnp.maximum(a0, a1), jnp.minimum(a0, a1)
    m23, n23 = jnp.maximum(a2, a3), jnp.minimum(a2, a3)
    return jnp.maximum(m01, m23), jnp.maximum(jnp.minimum(m01, m23), jnp.maximum(n01, n23))


def _route_tile(logits, b_col):
    aff = jax.nn.sigmoid(logits)
    biased = aff + b_col
    row = lambda m, e: m[e:e + 1, :]
    g_sel = best = None
    for g in range(N_EXPERT_GROUPS):
        top1, top2 = _second_max(*[row(biased, EXPERTS_PER_GROUP * g + j) for j in range(EXPERTS_PER_GROUP)])
        score = top1 + top2
        if g == 0:
            g_sel, best = jnp.zeros(score.shape, jnp.int32), score
        else:
            upd = score > best
            g_sel, best = jnp.where(upd, g, g_sel), jnp.where(upd, score, best)

    def in_group(m, j):
        out = row(m, (N_EXPERT_GROUPS - 1) * EXPERTS_PER_GROUP + j)
        for g in range(N_EXPERT_GROUPS - 2, -1, -1):
            out = jnp.where(g_sel == g, row(m, EXPERTS_PER_GROUP * g + j), out)
        return out

    v = [in_group(biased, j) for j in range(EXPERTS_PER_GROUP)]
    a = [in_group(aff, j) for j in range(EXPERTS_PER_GROUP)]
    i1, b1, w1 = jnp.zeros(g_sel.shape, jnp.int32), v[0], a[0]
    for j in range(1, EXPERTS_PER_GROUP):
        upd = v[j] > b1
        i1, b1, w1 = jnp.where(upd, j, i1), jnp.where(upd, v[j], b1), jnp.where(upd, a[j], w1)
    first = i1 == 0
    i2, b2, w2 = jnp.where(first, 1, 0), jnp.where(first, v[1], v[0]), jnp.where(first, a[1], a[0])
    for j in range(1, EXPERTS_PER_GROUP):
        upd = jnp.logical_and(i1 != j, v[j] > b2)
        i2, b2, w2 = jnp.where(upd, j, i2), jnp.where(upd, v[j], b2), jnp.where(upd, a[j], w2)
    den = w1 + w2
    return EXPERTS_PER_GROUP * g_sel + i1, EXPERTS_PER_GROUP * g_sel + i2, w1 / den, w2 / den


def _rows8(r0, r1):
    sub = lax.broadcasted_iota(jnp.int32, (8, r0.shape[1]), 0)
    return jnp.where(sub == 0, r0, jnp.where(sub == 1, r1, jnp.zeros_like(r0)))


def _outproj_kernel(ya_ref, yb_ref, wa_ref, wb_ref, x_ref, mod_ref, g_ref, wr_ref, br_ref, tri_ref,
                    xo_ref, f_ref, e_ref, w_ref, r_ref, cnt_ref, carry_ref):
    i = pl.program_id(0)
    m = mod_ref[0]
    y = _dot(ya_ref[...], wa_ref[...]) + _dot(yb_ref[...], wb_ref[...])
    x = x_ref[...] + m[2:3] * y
    xo_ref[...] = x
    f = _rms_mod(x, g_ref[...], m[3:4], m[4:5])
    _rows_to_slab(f_ref, f, TM, 0)
    _zero_slab_padding(f_ref, TM)
    f_hi = f.astype(BF16)
    f_lo = (f - f_hi.astype(F32)).astype(BF16)
    wr = wr_ref[...]
    w_hi = wr.astype(BF16)
    w_lo = (wr - w_hi.astype(F32)).astype(BF16)
    logits = _dot_nt(w_hi, f_hi) + _dot_nt(w_lo, f_hi) + _dot_nt(w_hi, f_lo)
    e0, e1, w0, w1 = _route_tile(logits, br_ref[:, 0:1])

    @pl.when(i == 0)
    def _():
        carry_ref[...] = jnp.zeros(carry_ref.shape, F32)

    sub = lax.broadcasted_iota(jnp.int32, (N_EXPERTS, TM), 0)
    hot0, hot1 = sub == e0, sub == e1
    member = jnp.where(jnp.logical_or(hot0, hot1), 1.0, 0.0)
    before = _dot(member.astype(BF16), tri_ref[...]) + carry_ref[:, 0:1]
    r0 = jnp.sum(jnp.where(hot0, before, 0.0), axis=0, keepdims=True)
    r1 = jnp.sum(jnp.where(hot1, before, 0.0), axis=0, keepdims=True)
    carry_ref[...] = carry_ref[...] + jnp.sum(member, axis=1, keepdims=True)
    e_ref[...] = _rows8(e0, e1)
    w_ref[...] = _rows8(w0, w1)
    r_ref[...] = _rows8(r0.astype(jnp.int32), r1.astype(jnp.int32))
    cnt_ref[...] = carry_ref[...]


def outproj(ya, yb, wa, wb, x, mods_l, g_ffn, w_router_t, b_router_col, n_ctx_blocks, casts=()):
    nt, d = x.shape
    nblk = nt // TM
    kern, c_in, c_out, c_shape, c_args = _with_casts(_outproj_kernel, casts, nblk, 10, 6)
    half = pl.BlockSpec((TM, 1024), lambda i: (i, 0))
    wsp = pl.BlockSpec((1024, d), lambda i: (0, 0), pipeline_mode=pl.Buffered(1))
    row = pl.BlockSpec((TM, d), lambda i: (i, 0))
    r8 = pl.BlockSpec((8, TM), lambda i: (0, i))
    cnt = pl.BlockSpec((N_EXPERTS, 128), lambda i: (0, 0))
    tri = (jnp.arange(TM)[:, None] < jnp.arange(TM)[None, :]).astype(BF16)
    res = pl.pallas_call(
        kern,
        grid=(nblk,),
        in_specs=[half, half, wsp, wsp, row, _mod_spec(n_ctx_blocks),
                  pl.BlockSpec((1, d), lambda i: (0, 0)),
                  pl.BlockSpec((N_EXPERTS, d), lambda i: (0, 0)), cnt,
                  pl.BlockSpec((TM, TM), lambda i: (0, 0))] + c_in,
        out_specs=[row, pl.BlockSpec((TM * SLAB, 128), lambda i: (i, 0)), r8, r8, r8, cnt] + c_out,
        out_shape=[jax.ShapeDtypeStruct((nt, d), F32), jax.ShapeDtypeStruct((nt * SLAB, 128), F32),
                   jax.ShapeDtypeStruct((8, nt), jnp.int32), jax.ShapeDtypeStruct((8, nt), F32),
                   jax.ShapeDtypeStruct((8, nt), jnp.int32), jax.ShapeDtypeStruct((N_EXPERTS, 128), F32)] + c_shape,
        scratch_shapes=[pltpu.VMEM((N_EXPERTS, 128), F32)],
        compiler_params=_cparams(),
        name="outproj",
    )(ya, yb, wa, wb, x, mods_l, g_ffn.reshape(1, d), w_router_t, b_router_col, tri, *c_args)
    return res[:6], res[6:]


N_LOC = 3 * ATTN_BLOCK


def _rope(x, cos, sin):
    lane = lax.broadcasted_iota(jnp.int32, x.shape, 1)
    swapped = jnp.where(lane % 64 < 32, pltpu.roll(x, 96, axis=1), pltpu.roll(x, 32, axis=1))
    return x * cos + swapped * sin


def _attn_kernel(sink_ref, q_ref, kp_ref, kc_ref, kn_ref, vp_ref, vc_ref, vn_ref, kx_ref, vx_ref,
                 cosp_ref, cosc_ref, cosn_ref, sinp_ref, sinc_ref, sinn_ref, o_ref,
                 qs_ref, kbuf_ref, vbuf_ref, *, n_ctx_blocks, n_blocks, n_ctx):
    i = pl.program_id(0)
    nkeys = N_LOC + n_ctx
    nq = Q_PER_KV * ATTN_BLOCK
    is_lat = jnp.where(i >= n_ctx_blocks, 1, 0)
    prev_ok = jnp.where(i - 1 >= n_ctx_blocks, is_lat, 0)
    next_ok = jnp.where(i + 1 <= n_blocks - 1, is_lat, 0)
    qi = lax.broadcasted_iota(jnp.int32, (nq, nkeys), 0) & (ATTN_BLOCK - 1)
    kj = lax.broadcasted_iota(jnp.int32, (nq, nkeys), 1)
    rel = kj - ATTN_BLOCK - qi
    blk_ok = jnp.where(kj < ATTN_BLOCK, prev_ok, jnp.where(kj < 2 * ATTN_BLOCK, is_lat, next_ok))
    rel = jnp.where(blk_ok > 0, rel, WINDOW + 1)
    valid = jnp.logical_or(kj >= N_LOC, jnp.logical_and(rel >= -WINDOW, rel <= WINDOW))
    rowh = lax.broadcasted_iota(jnp.int32, (nq, 1), 0) // ATTN_BLOCK
    for h in range(N_KV_HEADS):
        hc = slice(h * HEAD_DIM, (h + 1) * HEAD_DIM)
        kbuf_ref[0:ATTN_BLOCK, :] = _rope(kp_ref[:, hc], cosp_ref[...], sinp_ref[...]).astype(BF16)
        kbuf_ref[ATTN_BLOCK:2 * ATTN_BLOCK, :] = _rope(kc_ref[:, hc], cosc_ref[...], sinc_ref[...]).astype(BF16)
        kbuf_ref[2 * ATTN_BLOCK:N_LOC, :] = _rope(kn_ref[:, hc], cosn_ref[...], sinn_ref[...]).astype(BF16)
        kbuf_ref[N_LOC:, :] = kx_ref[:, hc].astype(BF16)
        vbuf_ref[0:ATTN_BLOCK, :] = vp_ref[:, hc].astype(BF16)
        vbuf_ref[ATTN_BLOCK:2 * ATTN_BLOCK, :] = vc_ref[:, hc].astype(BF16)
        vbuf_ref[2 * ATTN_BLOCK:N_LOC, :] = vn_ref[:, hc].astype(BF16)
        vbuf_ref[N_LOC:, :] = vx_ref[:, hc].astype(BF16)
        sink = jnp.zeros((Q_PER_KV * ATTN_BLOCK, 1), F32)
        for gq in range(Q_PER_KV):
            head = h * Q_PER_KV + gq
            qc = slice(head * HEAD_DIM, (head + 1) * HEAD_DIM)
            qs_ref[gq * ATTN_BLOCK:(gq + 1) * ATTN_BLOCK, :] = _rope(
                q_ref[:, qc], cosc_ref[...], sinc_ref[...]).astype(BF16)
            sink = jnp.where(rowh == gq, sink_ref[head], sink)
        s = _dot_nt(qs_ref[...], kbuf_ref[...]) * ATTN_SCALE
        s = jnp.where(valid, s, NEG_INF)
        mx = jnp.maximum(jnp.max(s, axis=-1, keepdims=True), sink)
        p = jnp.exp(s - mx)
        den = jnp.sum(p, axis=-1, keepdims=True) + jnp.exp(sink - mx)
        o = _dot(p.astype(BF16), vbuf_ref[...]) / den
        for gq in range(Q_PER_KV):
            head = h * Q_PER_KV + gq
            o_ref[:, head * HEAD_DIM:(head + 1) * HEAD_DIM] = o[gq * ATTN_BLOCK:(gq + 1) * ATTN_BLOCK].astype(BF16)


def attention(p, sink, cos_t, sin_t, n_ctx, casts=()):
    nt = p.shape[0]
    nblk = nt // ATTN_BLOCK
    ncb = n_ctx // ATTN_BLOCK
    kern, c_in, c_out, c_shape, c_args = _with_casts(
        functools.partial(_attn_kernel, n_ctx_blocks=ncb, n_blocks=nblk, n_ctx=n_ctx), casts, nblk, 16, 1)
    kcol, vcol = 2048 // KV_WIDTH, 2048 // KV_WIDTH + 1
    pm = lambda i: jnp.maximum(i - 1, 0)
    nx = lambda i: jnp.minimum(i + 1, nblk - 1)
    kv = lambda f, c: pl.BlockSpec((ATTN_BLOCK, KV_WIDTH), lambda i, s: (f(i), c))
    tab = lambda f: pl.BlockSpec((ATTN_BLOCK, HEAD_DIM), lambda i, s: (f(i), 0))
    same = lambda i: i
    grid_spec = pltpu.PrefetchScalarGridSpec(
        num_scalar_prefetch=1,
        grid=(nblk,),
        in_specs=[pl.BlockSpec((ATTN_BLOCK, C_WIDTH), lambda i, s: (i, 0)),
                  kv(pm, kcol), kv(same, kcol), kv(nx, kcol), kv(pm, vcol), kv(same, vcol), kv(nx, vcol),
                  pl.BlockSpec((n_ctx, KV_WIDTH), lambda i, s: (0, kcol)),
                  pl.BlockSpec((n_ctx, KV_WIDTH), lambda i, s: (0, vcol)),
                  tab(pm), tab(same), tab(nx), tab(pm), tab(same), tab(nx)] + c_in,
        out_specs=[pl.BlockSpec((ATTN_BLOCK, C_WIDTH), lambda i, s: (i, 0))] + c_out,
        scratch_shapes=[pltpu.VMEM((Q_PER_KV * ATTN_BLOCK, HEAD_DIM), BF16),
                        pltpu.VMEM((N_LOC + n_ctx, HEAD_DIM), BF16),
                        pltpu.VMEM((N_LOC + n_ctx, HEAD_DIM), BF16)],
    )
    res = pl.pallas_call(
        kern,
        grid_spec=grid_spec,
        out_shape=[jax.ShapeDtypeStruct((nt, C_WIDTH), BF16)] + c_shape,
        compiler_params=_cparams(),
        name="attention",
    )(sink, p, p, p, p, p, p, p, p, p, cos_t, cos_t, cos_t, sin_t, sin_t, sin_t, *c_args)
    return res[0], res[1:]


def rope_tables(n_ctx, n_lat):
    f32 = np.float32
    rows = n_lat // GRID_W
    row = np.repeat(np.arange(rows, dtype=f32), GRID_W)
    col = np.tile(np.arange(GRID_W, dtype=f32), rows)
    half = HEAD_DIM // 2
    inv_freq = np.power(f32(ROPE_BASE), -np.arange(0, half, 2, dtype=f32) / f32(half)).astype(f32)
    ang_r = row[:, None] * inv_freq
    ang_c = col[:, None] * inv_freq
    cos = np.concatenate([np.cos(ang_r), np.cos(ang_r), np.cos(ang_c), np.cos(ang_c)], axis=-1)
    sin = np.concatenate([-np.sin(ang_r), np.sin(ang_r), -np.sin(ang_c), np.sin(ang_c)], axis=-1)
    cos = np.concatenate([np.ones((n_ctx, HEAD_DIM), f32), cos], axis=0).astype(f32)
    sin = np.concatenate([np.zeros((n_ctx, HEAD_DIM), f32), sin], axis=0).astype(f32)
    return jnp.asarray(cos), jnp.asarray(sin)


S_LH = S_CHUNK * S_GROUP_CH
S_HALF = S_GROUP_BATCH * 128


S_PW_ROWS = 4 * S_CHUNK
S_PITCH = S_CHUNK + 8


def _dot3_nt(x, a):
    x_hi, a_hi = x.astype(BF16), a.astype(BF16)
    x_lo, a_lo = (x - x_hi.astype(F32)).astype(BF16), (a - a_hi.astype(F32)).astype(BF16)
    return _dot_nt(x_hi, a_hi) + _dot_nt(x_lo, a_hi) + _dot_nt(x_hi, a_lo)


def _s5_kernel(u_ref, pw_ref, bc_ref, lam_ref, y_ref, s_ref, hf_ref, hr_ref,
               ws_ref, wo_ref, are_ref, aim_ref, t_ref, ug_ref, yg_ref, upad_ref, ypad_ref, *,
               n_chunks, n_ctx_chunks):
    L = S_CHUNK
    gpt = 128 // S_GROUP_CH
    ug_ref[:, n_chunks:, :] = jnp.zeros((S_GROUP_BATCH, ug_ref.shape[1] - n_chunks, S_LH), F32)

    def spread(c, carry):
        upad_ref[pl.ds(pl.multiple_of(c * S_PITCH, 8), L), :] = u_ref[pl.ds(pl.multiple_of(c * L, L), L), :]
        return carry
    lax.fori_loop(0, n_chunks, spread, 0, unroll=8)
    for s_tok in range(L):
        x = upad_ref[pl.ds(s_tok, n_chunks, stride=S_PITCH), :]
        dst = s_tok % gpt
        for g in range(S_GROUP_BATCH):
            k = (dst - g) % gpt
            r = x if k == 0 else pltpu.roll(x, k * S_GROUP_CH, axis=1)
            ug_ref[g, 0:n_chunks, s_tok * S_GROUP_CH:(s_tok + 1) * S_GROUP_CH] = (
                r[:, dst * S_GROUP_CH:(dst + 1) * S_GROUP_CH])

    def scaled_rows(g, x_re, x_im, row0, n_rows, emit):
        def body(r, carry):
            p_re = pw_ref[g, 0, pl.ds(row0 + r, 1), :]
            p_im = pw_ref[g, 1, pl.ds(row0 + r, 1), :]
            emit(pl.ds(pl.multiple_of(r * S_GROUP_CH, S_GROUP_CH), S_GROUP_CH),
                 x_re * p_re - x_im * p_im, x_re * p_im + x_im * p_re)
            return carry
        lax.fori_loop(0, n_rows, body, 0, unroll=4)

    def emit_ws(rows, re, im):
        ws_ref[rows, 0:128] = re.astype(BF16)
        ws_ref[rows, 128:256] = im.astype(BF16)

    def emit_wo(rows, re, im):
        wo_ref[rows, 0:128] = re.astype(BF16)
        wo_ref[rows, 128:256] = (-im).astype(BF16)

    def emit_a(rows, re, im):
        are_ref[rows, :] = re
        aim_ref[rows, :] = im

    for g in range(S_GROUP_BATCH):
        scaled_rows(g, bc_ref[g, 0], bc_ref[g, 1], 0, L, emit_ws)
        s = _dot(ug_ref[g].astype(BF16), ws_ref[...])
        s_ref[:, g * 128:(g + 1) * 128] = s[:, 0:128]
        s_ref[:, S_HALF + g * 128:S_HALF + (g + 1) * 128] = s[:, 128:256]
    hf_ref[...] = jnp.zeros(hf_ref.shape, F32)
    hr_ref[...] = jnp.zeros(hr_ref.shape, F32)
    lam = lam_ref[0]
    lam_re, lam_im = lam[:, :S_HALF], lam[:, S_HALF:]
    is_fwd = lax.broadcasted_iota(jnp.int32, (1, 2 * S_HALF), 1) % 128 < S_STATE

    def step(k, state):
        st_re, st_im = state
        cf = k
        cr = jnp.where(k < n_ctx_chunks, n_ctx_chunks - 1 - k, n_chunks - 1 - (k - n_ctx_chunks))
        st = jnp.concatenate([st_re, st_im], axis=1)
        hf_ref[pl.ds(cf, 1), :] = st
        hr_ref[pl.ds(cr, 1), :] = st
        s_in = jnp.where(is_fwd, s_ref[pl.ds(cf, 1), :], s_ref[pl.ds(cr, 1), :])
        new_re = lam_re * st_re - lam_im * st_im + s_in[:, :S_HALF]
        new_im = lam_re * st_im + lam_im * st_re + s_in[:, S_HALF:]
        return new_re, new_im

    zero = jnp.zeros((1, S_HALF), F32)
    lax.fori_loop(0, n_chunks, step, (zero, zero))
    hin = jnp.where(is_fwd, hf_ref[...], hr_ref[...]).astype(BF16)
    kwidth = 2 * L * S_GROUP_CH
    for g in range(S_GROUP_BATCH):
        hin_g = jnp.concatenate([hin[:, g * 128:(g + 1) * 128],
                                 hin[:, S_HALF + g * 128:S_HALF + (g + 1) * 128]], axis=1)
        c_re, c_im = bc_ref[g, 2], bc_ref[g, 3]
        scaled_rows(g, c_re, c_im, L, L, emit_wo)
        scaled_rows(g, c_re, c_im, 2 * L, 2 * L, emit_a)
        kern = _dot3_nt(bc_ref[g, 0], are_ref[...]) - _dot3_nt(bc_ref[g, 1], aim_ref[...])
        for s_tok in range(L):
            off = (L - 1 - s_tok) * S_GROUP_CH
            win = kern if off == 0 else pltpu.roll(kern, kwidth - off, axis=1)
            t_ref[s_tok * S_GROUP_CH:(s_tok + 1) * S_GROUP_CH, :] = win[:, :S_LH].astype(BF16)
        yg_ref[g] = _dot(ug_ref[g].astype(BF16), t_ref[...]) + _dot_nt(hin_g, wo_ref[...])
    lane_group = lax.broadcasted_iota(jnp.int32, (n_chunks, 128), 1) // S_GROUP_CH
    for s_tok in range(L):
        dst = s_tok % gpt
        cols = slice((s_tok // gpt) * 128, (s_tok // gpt + 1) * 128)
        z = None
        for g in range(S_GROUP_BATCH):
            piece = yg_ref[g, 0:n_chunks, cols]
            k = (g - dst) % gpt
            r = piece if k == 0 else pltpu.roll(piece, k * S_GROUP_CH, axis=1)
            z = r if z is None else jnp.where(lane_group == g, r, z)
        ypad_ref[pl.ds(s_tok, n_chunks, stride=S_PITCH), :] = z

    def pack(c, carry):
        y_ref[pl.ds(pl.multiple_of(c * L, L), L), :] = ypad_ref[pl.ds(pl.multiple_of(c * S_PITCH, 8), L), :]
        return carry
    lax.fori_loop(0, n_chunks, pack, 0, unroll=8)


def s5_scan(p, pw_tab, bc_tab, lam_rows, n_chunks, n_ctx_chunks):
    nt = p.shape[0]
    assert S_GROUP_BATCH * S_GROUP_CH == 128 and nt == n_chunks * S_CHUNK
    ncp = -(-n_chunks // 16) * 16
    lh = S_LH
    return pl.pallas_call(
        functools.partial(_s5_kernel, n_chunks=n_chunks, n_ctx_chunks=n_ctx_chunks),
        grid=(S_GROUPS // S_GROUP_BATCH,),
        in_specs=[pl.BlockSpec((nt, 128), lambda i: (0, C_WIDTH // 128 + i)),
                  pl.BlockSpec((S_GROUP_BATCH, 2, S_PW_ROWS, 128), lambda i: (i, 0, 0, 0)),
                  pl.BlockSpec((S_GROUP_BATCH, 4, S_GROUP_CH, 128), lambda i: (i, 0, 0, 0)),
                  pl.BlockSpec((1, 1, 2 * S_HALF), lambda i: (i, 0, 0))],
        out_specs=pl.BlockSpec((nt, 128), lambda i: (0, i)),
        out_shape=jax.ShapeDtypeStruct((nt, S_WIDTH), F32),
        scratch_shapes=[pltpu.VMEM((ncp, 2 * S_HALF), F32)] * 3 + [
            pltpu.VMEM((lh, 256), BF16), pltpu.VMEM((lh, 256), BF16),
            pltpu.VMEM((2 * lh, 128), F32), pltpu.VMEM((2 * lh, 128), F32), pltpu.VMEM((lh, lh), BF16),
            pltpu.VMEM((S_GROUP_BATCH, ncp, lh), F32), pltpu.VMEM((S_GROUP_BATCH, ncp, lh), F32),
            pltpu.VMEM((n_chunks * S_PITCH, 128), F32), pltpu.VMEM((n_chunks * S_PITCH, 128), F32)],
        compiler_params=_cparams(),
        name="s5_scan",
    )(p, pw_tab, bc_tab, lam_rows)


def s5_weights(lam_re, lam_im, log_dt, b_re, b_im, c_re, c_im):
    L = S_CHUNK
    lam = lax.complex(lam_re.astype(F32), lam_im.astype(F32))
    lam_dt = lam * jnp.exp(log_dt.astype(F32))[..., None]
    lam_bar = jnp.exp(lam_dt)
    b_bar = ((lam_bar - 1.0) / lam)[..., None] * lax.complex(b_re.astype(F32), b_im.astype(F32))
    c = lax.complex(c_re.astype(F32), c_im.astype(F32))
    n = np.arange(L)
    lag = np.arange(2 * L) - (L - 1)
    expo = np.concatenate([np.stack([L - 1 - n, n], axis=-1), np.stack([n + 1, L - n], axis=-1),
                           np.stack([lag, -lag], axis=-1)], axis=0)
    live = np.logical_and(expo >= 0, (np.arange(4 * L) < 4 * L - 1)[:, None])
    base = jnp.exp(lam_dt[None] * jnp.arange(L + 1, dtype=F32)[:, None, None, None])
    picked = jnp.stack([base[np.clip(expo[:, dr], 0, L), dr] for dr in range(2)], axis=1)
    pw = jnp.where(live[:, :, None, None], picked, 0.0)
    pw = jnp.transpose(pw, (2, 0, 1, 3)).reshape(S_GROUPS, S_PW_ROWS, 2 * S_STATE)
    pw_tab = jnp.stack([pw.real, pw.imag], axis=1)
    bt = jnp.transpose(b_bar, (1, 3, 0, 2)).reshape(S_GROUPS, S_GROUP_CH, 2 * S_STATE)
    ct = jnp.transpose(c, (1, 2, 0, 3)).reshape(S_GROUPS, S_GROUP_CH, 2 * S_STATE)
    bc_tab = jnp.stack([bt.real, bt.imag, ct.real, ct.imag], axis=1)
    lam_l = jnp.exp(lam_dt * L)
    nb = S_GROUPS // S_GROUP_BATCH
    lre = jnp.concatenate([lam_l[0].real, lam_l[1].real], axis=-1).reshape(nb, 1, S_HALF)
    lim = jnp.concatenate([lam_l[0].imag, lam_l[1].imag], axis=-1).reshape(nb, 1, S_HALF)
    return pw_tab, bc_tab, jnp.concatenate([lre, lim], axis=-1)


def _glu_kernel(y_ref, u_ref, d_ref, w_ref, b_ref, o_ref):
    z = jax.nn.gelu(y_ref[...] + d_ref[...] * u_ref[...])
    gate = jax.nn.sigmoid(_dot(z.astype(BF16), w_ref[...]) + b_ref[...])
    o_ref[...] = (z * gate).astype(BF16)


def s5_glu(y_ssm, p, d_skip, glu_w_bf16, glu_b, casts=()):
    nt = y_ssm.shape[0]
    vec = pl.BlockSpec((1, S_WIDTH), lambda i: (0, 0))
    kern, c_in, c_out, c_shape, c_args = _with_casts(_glu_kernel, casts, nt // TM, 5, 1)
    res = pl.pallas_call(
        kern,
        grid=(nt // TM,),
        in_specs=[pl.BlockSpec((TM, S_WIDTH), lambda i: (i, 0)),
                  pl.BlockSpec((TM, S_WIDTH), lambda i: (i, 1)),
                  vec, pl.BlockSpec((S_WIDTH, S_WIDTH), lambda i: (0, 0)), vec] + c_in,
        out_specs=[pl.BlockSpec((TM, S_WIDTH), lambda i: (i, 0))] + c_out,
        out_shape=[jax.ShapeDtypeStruct((nt, S_WIDTH), BF16)] + c_shape,
        compiler_params=_cparams(),
        name="s5_glu",
    )(y_ssm, p, d_skip.reshape(1, -1), glu_w_bf16, glu_b.reshape(1, -1), *c_args)
    return res[0], res[1:]


MOE_FCHUNKS = ((0, 512), (512, 512), (1024, 384))
MOE_YC = 512


def _dispatch_kernel(pos_ref, cnt_ref, pad_ref, start_ref, nu_ref, f_hbm, xs_hbm, fbuf, zero_ref, lsem, sem, zsem,
                     *, nt, n_blocks):
    i = pl.program_id(0)
    n_tiles = pl.num_programs(0)
    blk_rows = MOE_BM * SLAB
    tile_rows = TM * SLAB

    def tile_load(tile):
        return pltpu.make_async_copy(f_hbm.at[pl.ds(pl.multiple_of(tile * tile_rows, tile_rows), tile_rows), :],
                                     fbuf.at[tile % 3], lsem.at[tile % 3])

    def start_body(t, carry):
        src = fbuf.at[i % 3, pl.ds(pl.multiple_of(t * SLAB, SLAB), SLAB), :]
        for k in range(TOP_K):
            p = pl.multiple_of(pos_ref[k * nt + i * TM + t] * SLAB, SLAB)
            pltpu.make_async_copy(src, xs_hbm.at[pl.ds(p, SLAB), :], sem.at[i % 2]).start(priority=k)
        return carry

    def wait_tile(slot):
        for _ in range(TOP_K):
            pltpu.make_async_copy(fbuf.at[0], xs_hbm.at[pl.ds(0, tile_rows), :], sem.at[slot]).wait()

    def pad_copies(start):
        for e in range(N_EXPERTS):
            def body(r, carry):
                p = pl.multiple_of((start_ref[e] + r) * SLAB, SLAB)
                cp = pltpu.make_async_copy(zero_ref.at[pl.ds(0, SLAB), :], xs_hbm.at[pl.ds(p, SLAB), :], zsem)
                if start:
                    cp.start()
                else:
                    cp.wait()
                return carry
            lax.fori_loop(cnt_ref[e], pad_ref[e], body, 0)

        def tail(blk, carry):
            p = pl.multiple_of(blk * blk_rows, blk_rows)
            cp = pltpu.make_async_copy(zero_ref, xs_hbm.at[pl.ds(p, blk_rows), :], zsem)
            if start:
                cp.start()
            else:
                cp.wait()
            return carry
        lax.fori_loop(nu_ref[0], n_blocks, tail, 0)

    @pl.when(i == 0)
    def _():
        tile_load(0).start()
        zero_ref[...] = jnp.zeros(zero_ref.shape, F32)
        pad_copies(True)

    @pl.when(i + 1 < n_tiles)
    def _():
        tile_load(i + 1).start()

    tile_load(i).wait()
    lax.fori_loop(0, TM, start_body, 0, unroll=8)

    @pl.when(i > 0)
    def _():
        wait_tile((i - 1) % 2)

    @pl.when(i == n_tiles - 1)
    def _():
        wait_tile(i % 2)

    @pl.when(i == 0)
    def _():
        pad_copies(False)


def moe_dispatch(f_slab, pos, counts, padded, pad_start, n_used, n_blocks):
    nt = f_slab.shape[0] // SLAB
    grid_spec = pltpu.PrefetchScalarGridSpec(
        num_scalar_prefetch=5,
        grid=(nt // TM,),
        in_specs=[pl.BlockSpec(memory_space=pl.ANY)],
        out_specs=pl.BlockSpec(memory_space=pl.ANY),
        scratch_shapes=[pltpu.VMEM((3, TM * SLAB, 128), F32), pltpu.VMEM((MOE_BM * SLAB, 128), F32),
                        pltpu.SemaphoreType.DMA((3,)), pltpu.SemaphoreType.DMA((2,)), pltpu.SemaphoreType.DMA],
    )
    return pl.pallas_call(
        functools.partial(_dispatch_kernel, nt=nt, n_blocks=n_blocks),
        grid_spec=grid_spec,
        out_shape=jax.ShapeDtypeStruct((n_blocks * MOE_BM * SLAB, 128), F32),
        compiler_params=_cparams(),
        name="moe_dispatch",
    )(pos, counts, padded, pad_start, n_used, f_slab)


def _moe_kernel(be_ref, nu_ref, xs_ref, wg_ref, wu_ref, wd_ref, y_ref, hbuf):
    @pl.when(pl.program_id(0) >= nu_ref[0])
    def _():
        y_ref[...] = jnp.zeros(y_ref.shape, F32)

    @pl.when(pl.program_id(0) < nu_ref[0])
    def _():
        x = _slab_to_rows(xs_ref, MOE_BM).astype(BF16)
        for f0, fw in MOE_FCHUNKS:
            hg = _dot(x, wg_ref[0, :, f0:f0 + fw])
            hu = _dot(x, wu_ref[0, :, f0:f0 + fw])
            hbuf[:, f0:f0 + fw] = (hg * jax.nn.sigmoid(hg) * hu).astype(BF16)
        for c0 in range(0, D_MODEL, MOE_YC):
            _rows_to_slab(y_ref, _dot(hbuf[...], wd_ref[0, :, c0:c0 + MOE_YC]), MOE_BM, c0 // 128)
        _zero_slab_padding(y_ref, MOE_BM)


def moe_experts(x_sorted, block_e, n_used, wg, wu, wd):
    n_blocks = block_e.shape[0]
    d, fexp = wg.shape[1], wg.shape[2]
    blk = lambda b, be, nu: (jnp.minimum(b, nu[0] - 1), 0)
    wsel = lambda b, be, nu: (be[jnp.minimum(b, nu[0] - 1)], 0, 0)
    grid_spec = pltpu.PrefetchScalarGridSpec(
        num_scalar_prefetch=2,
        grid=(n_blocks,),
        in_specs=[pl.BlockSpec((MOE_BM * SLAB, 128), blk),
                  pl.BlockSpec((1, d, fexp), wsel), pl.BlockSpec((1, d, fexp), wsel),
                  pl.BlockSpec((1, fexp, d), wsel)],
        out_specs=pl.BlockSpec((MOE_BM * SLAB, 128), lambda b, be, nu: (b, 0)),
        scratch_shapes=[pltpu.VMEM((MOE_BM, fexp), BF16)],
    )
    return pl.pallas_call(
        _moe_kernel,
        grid_spec=grid_spec,
        out_shape=jax.ShapeDtypeStruct(x_sorted.shape, F32),
        compiler_params=_cparams(),
        name="moe_experts",
    )(block_e, n_used, x_sorted, wg, wu, wd)


def moe_layout(eidx, rank, cnt):
    nt = eidx.shape[1]
    n_blocks = -(-(nt * TOP_K) // MOE_BM) + N_EXPERTS
    counts = cnt[:, 0].astype(jnp.int32)
    padded = (counts + MOE_BM - 1) // MOE_BM * MOE_BM
    pad_end = jnp.cumsum(padded)
    pad_start = pad_end - padded
    hot = eidx[:TOP_K, :, None] == jnp.arange(N_EXPERTS, dtype=jnp.int32)
    pos = (rank[:TOP_K] + jnp.sum(jnp.where(hot, pad_start, 0), axis=-1)).reshape(-1)
    blk_start = jnp.arange(n_blocks, dtype=jnp.int32) * MOE_BM
    block_e = jnp.minimum(jnp.sum(blk_start[:, None] >= pad_end[None, :], axis=1), N_EXPERTS - 1).astype(jnp.int32)
    n_used = (pad_end[-1:] // MOE_BM).astype(jnp.int32)
    return pos, counts, padded, pad_start, block_e, n_used


def _final_kernel(pos_ref, x_ref, y_hbm, wt_ref, mod_ref, g_ref, out_ref, gbuf, sem, *, nt, tile0):
    x = x_ref[...] + mod_ref[0][5:6] * _moe_combine(pos_ref, y_hbm, wt_ref, gbuf, sem, nt, tile0)
    ms = jnp.mean(x * x, axis=-1, keepdims=True)
    out_ref[...] = x * lax.rsqrt(ms + EPS) * g_ref[...]


def final_norm(x, moe, mods_l, g_final, n_ctx_blocks):
    nt, d = x.shape
    nlat = nt // TM - n_ctx_blocks
    y_sorted, pos, wts_t = moe
    grid_spec = pltpu.PrefetchScalarGridSpec(
        num_scalar_prefetch=1,
        grid=(nlat,),
        in_specs=[pl.BlockSpec((TM, d), lambda i, *_: (i + n_ctx_blocks, 0)),
                  pl.BlockSpec(memory_space=pl.ANY),
                  pl.BlockSpec((TM, 8), lambda i, *_: (i + n_ctx_blocks, 0)),
                  pl.BlockSpec((1, 6, d), lambda i, *_: (0, 0, 0)),
                  pl.BlockSpec((1, d), lambda i, *_: (0, 0))],
        out_specs=pl.BlockSpec((TM, d), lambda i, *_: (i, 0)),
        scratch_shapes=_COMBINE_SCRATCH,
    )
    return pl.pallas_call(
        functools.partial(_final_kernel, nt=nt, tile0=n_ctx_blocks),
        grid_spec=grid_spec,
        out_shape=jax.ShapeDtypeStruct((nlat * TM, d), F32),
        compiler_params=_cparams(),
        name="final_norm",
    )(pos, x, y_sorted, wts_t, mods_l, g_final.reshape(1, d))


def kernel(x, c, ctx, c_ctx, w_mod, b_mod, g_mix, g_ffn, w_in_even, w_out_even, sgu_ln_g, sgu_ln_b, sgu_w, sgu_b, conv_w, conv_b, conv_ln_g, conv_ln_b, w_in_odd, w_out_odd, attn_sink, ssm_lam_re, ssm_lam_im, ssm_log_dt, ssm_b_re, ssm_b_im, ssm_c_re, ssm_c_im, ssm_d, glu_w, glu_b, w_router, b_router, w_gate, w_up, w_down, g_final):
    bsz, n_lat, d = x.shape
    n_ctx = ctx.shape[1]
    assert bsz == 1 and d == D_MODEL and n_ctx % TM == 0 and n_lat % TM == 0
    nt = n_ctx + n_lat
    ncb = n_ctx // TM
    n_chunks = nt // S_CHUNK
    ncp = -(-n_chunks // 16) * 16

    xs = (ctx[0], x[0])
    cond8 = jnp.concatenate([c, c_ctx[None, :], jnp.zeros((6, d), F32)], axis=0)
    mods = adaln_all(cond8, w_mod, b_mod)[:, :2].reshape(DEPTH, 2, 6, d)
    cos_t, sin_t = rope_tables(n_ctx, n_lat)
    w_router_t = w_router.T
    b_router_col = jnp.broadcast_to(b_router.astype(F32)[:, None], (N_EXPERTS, 128))
    fexp = w_gate.shape[-1]
    w_stack = (w_gate.reshape(-1, fexp), w_up.reshape(-1, fexp), w_down.reshape(-1, d))
    expert_bf = {}

    def job(layer, which):
        return (w_stack[which], layer)

    def done(jobs, outs):
        for (layer, which), o in zip(jobs, outs):
            expert_bf[(layer, which)] = o

    moe = None
    for l in range(DEPTH):
        j = l // 2
        odd = l % 2 == 1
        if odd:
            wi = w_in_odd[j]
            w_in = jnp.concatenate([wi[:, :C_WIDTH], wi[:, C_WIDTH + 2 * KV_WIDTH:],
                                    wi[:, C_WIDTH:C_WIDTH + 2 * KV_WIDTH]], axis=1).astype(BF16)
            w_out = w_out_odd[j].astype(BF16)
        else:
            w_in = w_in_even[j].astype(BF16)
            w_out = w_out_even[j].astype(BF16)
        jobs = [(l, 0)] if (l == 0 or odd) else []
        x_new, p, outs = inproj(xs, g_mix[l], mods[l], w_in, ncb, moe, mods[l - 1] if l > 0 else None,
                                casts=[job(*jb) for jb in jobs])
        done(jobs, outs)
        if x_new is not None:
            xs = x_new
        if odd:
            jobs = [(l, 1)]
            ya, outs = attention(p, attn_sink[j], cos_t, sin_t, n_ctx, casts=[job(*jb) for jb in jobs])
            done(jobs, outs)
            pw_tab, bc_tab, lam_rows = s5_weights(ssm_lam_re[j], ssm_lam_im[j], ssm_log_dt[j], ssm_b_re[j],
                                                  ssm_b_im[j], ssm_c_re[j], ssm_c_im[j])
            y_ssm = s5_scan(p, pw_tab, bc_tab, lam_rows, n_chunks, n_ctx // S_CHUNK)
            jobs = [(l + 1, 0)] if l + 1 < DEPTH else []
            yb, outs = s5_glu(y_ssm, p, ssm_d[j], glu_w[j].astype(BF16), glu_b[j], casts=[job(*jb) for jb in jobs])
            done(jobs, outs)
        else:
            bs_full = jnp.broadcast_to(sgu_b[j][:, :, None], (A_GROUPS, CHUNK, CHUNK)).astype(F32)
            jobs = [(l, 1)]
            ya, yb, outs = even_mixer(p, sgu_ln_g[j], sgu_ln_b[j], sgu_w[j].astype(BF16), bs_full,
                                      conv_w[j], conv_b[j], conv_ln_g[j], conv_ln_b[j], ncb,
                                      casts=[job(*jb) for jb in jobs])
            done(jobs, outs)
        jobs = [(l, 2)]
        (xs, f_slab, eidx, wts, rank, cnt), outs = outproj(
            ya, yb, w_out[:1024], w_out[1024:], xs, mods[l], g_ffn[l], w_router_t, b_router_col, ncb,
            casts=[job(*jb) for jb in jobs])
        done(jobs, outs)
        pos, counts, padded, pad_start, block_e, n_used = moe_layout(eidx, rank, cnt)
        x_sorted = moe_dispatch(f_slab, pos, counts, padded, pad_start, n_used, block_e.shape[0])
        y_sorted = moe_experts(x_sorted, block_e, n_used,
                               expert_bf[(l, 0)].reshape(N_EXPERTS, d, fexp),
                               expert_bf[(l, 1)].reshape(N_EXPERTS, d, fexp),
                               expert_bf[(l, 2)].reshape(N_EXPERTS, fexp, d))
        moe = (y_sorted, pos, wts.T)
    out = final_norm(xs, moe, mods[DEPTH - 1], g_final, ncb)
    return out.reshape(bsz, n_lat, d)
```
